```python
import jax, jax.numpy as jnp
from jax import lax
import numpy as np

D_MODEL = 1024
BATCH = 8
SEQ = 2048
DEPTH = 2

GRID_W = 64
CTX_LEN = 256

HEAD_DIM = 64
ATTN_HEADS = (D_MODEL // 2) // HEAD_DIM
ATTN_KV_HEADS = 2
ATTN_WIDTH = ATTN_HEADS * HEAD_DIM
KV_WIDTH = ATTN_KV_HEADS * HEAD_DIM
Q_BLOCK = 128
ROPE_THETA = 10000.0

HG_DIM = 64
HG_HEADS = (D_MODEL // 4) // HG_DIM
HG_WIDTH = HG_HEADS * HG_DIM
HG_CHUNK = 64
F_MIN = 1e-30

FT_DIM = 64
FT_GROUPS = (D_MODEL // 4) // FT_DIM
FT_WIDTH = FT_GROUPS * FT_DIM

MIX_WIDTH = ATTN_WIDTH + HG_WIDTH + FT_WIDTH
PROJ_WIDTH = ATTN_WIDTH + 2 * KV_WIDTH + 5 * HG_WIDTH + FT_WIDTH

D_FF = 2816
CONV_W = 3

ALPHA = (2 * DEPTH) ** 0.25
BETA = (8 * DEPTH) ** -0.25
EPS = 1e-6

kernel_name = 'hymba_style_hgrn2_fnet_gqa_prefix_dit'


def split_proj(p):
    sizes = [ATTN_WIDTH, KV_WIDTH, KV_WIDTH, HG_WIDTH, HG_WIDTH, HG_WIDTH, HG_WIDTH, HG_WIDTH, FT_WIDTH]
    return jnp.split(p, [int(s) for s in np.cumsum(sizes)[:-1]], axis=-1)


def rms_norm(x, g):
    xf = x.astype(jnp.float32)
    y = xf * lax.rsqrt(jnp.mean(xf * xf, axis=-1, keepdims=True) + EPS)
    return (y * g.astype(jnp.float32)).astype(x.dtype)


def layer_norm(x, g, b):
    xf = x.astype(jnp.float32)
    mu = jnp.mean(xf, axis=-1, keepdims=True)
    var = jnp.mean(jnp.square(xf - mu), axis=-1, keepdims=True)
    y = (xf - mu) * lax.rsqrt(var + EPS)
    return (y * g.astype(jnp.float32) + b.astype(jnp.float32)).astype(x.dtype)


def to_heads(a, n):
    b, t, _ = a.shape
    return a.reshape(b, t, n, -1).transpose(0, 2, 1, 3)


def axial_rope(x, row, col):
    half = x.shape[-1] // 2
    nf = half // 2
    inv = ROPE_THETA ** (-jnp.arange(nf, dtype=jnp.float32) / nf)

    def rotate(xa, pos):
        ang = pos.astype(jnp.float32)[:, None] * inv
        cos = jnp.cos(ang).astype(x.dtype)
        sin = jnp.sin(ang).astype(x.dtype)
        x1, x2 = xa[..., :nf], xa[..., nf:]
        return jnp.concatenate([x1 * cos - x2 * sin, x1 * sin + x2 * cos], axis=-1)

    return jnp.concatenate([rotate(x[..., :half], row), rotate(x[..., half:], col)], axis=-1)


def group_queries(q):
    b, nh, t, hd = q.shape
    return q.reshape(b, ATTN_KV_HEADS, nh // ATTN_KV_HEADS, t, hd)


def attend(q, k, v):
    s = jnp.einsum('bhgqd,bhkd->bhgqk', q, k).astype(jnp.float32) * (HEAD_DIM ** -0.5)
    p = jax.nn.softmax(s, axis=-1).astype(v.dtype)
    return jnp.einsum('bhgqk,bhkd->bhgqd', p, v)


def latent_attention(q, k_all, v_all):
    b, kvh, g, t, hd = q.shape
    nb = t // Q_BLOCK
    qb = jnp.moveaxis(q.reshape(b, kvh, g, nb, Q_BLOCK, hd), 3, 0)
    o = lax.map(lambda qblk: attend(qblk, k_all, v_all), qb)
    return o.transpose(1, 0, 4, 2, 3, 5).reshape(b, t, kvh * g * hd)


def gla_scan(q, k, v, logf, s0, reverse):
    if reverse:
        q, k, v, logf = (jnp.flip(a, axis=2) for a in (q, k, v, logf))
    b, h, t, dk = q.shape
    n = t // HG_CHUNK

    def chunks(a):
        return a.reshape(b, h, n, HG_CHUNK, a.shape[-1]).transpose(2, 0, 1, 3, 4)

    tri = jnp.tril(jnp.ones((HG_CHUNK, HG_CHUNK), dtype=bool))[:, :, None]

    def step(state, inp):
        qc, kc, vc, lf = inp
        cum = jnp.cumsum(lf, axis=2)
        diff = cum[:, :, :, None, :] - cum[:, :, None, :, :]
        decay = jnp.where(tri, jnp.exp(jnp.where(tri, diff, 0.0)), 0.0)
        scores = jnp.einsum('bhtk,bhtsk,bhsk->bhts', qc, decay, kc)
        o = (jnp.einsum('bhts,bhsv->bhtv', scores, vc)
             + jnp.einsum('bhtk,bhkv->bhtv', qc * jnp.exp(cum), state))
        last = cum[:, :, -1, :]
        state = (jnp.exp(last)[..., None] * state
                 + jnp.einsum('bhsk,bhsv->bhkv', kc * jnp.exp(last[:, :, None, :] - cum), vc))
        return state, o

    s_final, o = lax.scan(step, s0, (chunks(q), chunks(k), chunks(v), chunks(logf)))
    o = o.transpose(1, 2, 0, 3, 4).reshape(b, h, t, v.shape[-1])
    if reverse:
        o = jnp.flip(o, axis=2)
    return o, s_final


def hgrn_qv(hq, hi):
    q = to_heads(jax.nn.silu(hq), HG_HEADS).astype(jnp.float32)
    v = to_heads(hi, HG_HEADS).astype(jnp.float32)
    return q, v


def hgrn_forget(hf, lb):
    lbh = lb.reshape(HG_HEADS, 1, HG_DIM)
    z = to_heads(hf, HG_HEADS).astype(jnp.float32)
    f = lbh + (1.0 - lbh) * jax.nn.sigmoid(z)
    logf = jnp.log(jnp.maximum(f, F_MIN))
    return logf, 1.0 - f


def hgrn_out(o, hg, g):
    b, h, t, dv = o.shape
    on = rms_norm(o.transpose(0, 2, 1, 3), g)
    gate = jax.nn.silu(hg.astype(jnp.float32)).reshape(b, t, h, dv)
    return (on * gate).reshape(b, t, h * dv).astype(hg.dtype)


def fourier_mix(u, w):
    b, t, _ = u.shape
    ug = u.reshape(b, t, FT_GROUPS, FT_DIM).astype(jnp.float32)
    z = jnp.fft.fft2(ug, axes=(1, 3), norm='ortho').real.astype(u.dtype)
    return jnp.einsum('btgc,gcd->btgd', z, w).reshape(b, t, FT_WIDTH)


def conv_ffn(h, w_up, conv_w, conv_b, w_down):
    u = h @ w_up
    up = jnp.pad(u, ((0, 0), (1, 1), (0, 0)))
    u = up[:, :-2] * conv_w[0] + up[:, 1:-1] * conv_w[1] + up[:, 2:] * conv_w[2] + conv_b
    a, g = jnp.split(u, 2, axis=-1)
    return (a * jax.nn.silu(g)) @ w_down


def setup_inputs(seed: int = 0) -> dict:
    key = jax.random.key(seed)
    ks = jax.random.split(key, 21)
    f32 = jnp.float32

    def nrm(k, shape, s):
        return jax.random.normal(k, shape, f32) * s

    return {
        'x': nrm(ks[0], (BATCH, SEQ, D_MODEL), 1.0),
        'c': nrm(ks[1], (BATCH, D_MODEL), 1.0),
        'ctx': nrm(ks[2], (BATCH, CTX_LEN, D_MODEL), 1.0),
        'c_ctx': nrm(ks[3], (D_MODEL,), 1.0),
        'w_ada': nrm(ks[4], (DEPTH, D_MODEL, 6 * D_MODEL), 0.5 * D_MODEL ** -0.5),
        'b_ada': nrm(ks[5], (DEPTH, 6 * D_MODEL), 0.02),
        'w_in': nrm(ks[6], (DEPTH, D_MODEL, PROJ_WIDTH), D_MODEL ** -0.5),
        'q_norm': 1.0 + nrm(ks[7], (DEPTH, HEAD_DIM), 0.02),
        'k_norm': 1.0 + nrm(ks[8], (DEPTH, HEAD_DIM), 0.02),
        'hg_lb': nrm(ks[9], (DEPTH, 2, HG_WIDTH), 0.5),
        'hg_norm': 1.0 + nrm(ks[10], (DEPTH, HG_DIM), 0.02),
        'ft_w': nrm(ks[11], (DEPTH, FT_GROUPS, FT_DIM, FT_DIM), FT_DIM ** -0.5),
        'w_out': nrm(ks[12], (DEPTH, MIX_WIDTH, D_MODEL), BETA * MIX_WIDTH ** -0.5),
        'ln1_g': 1.0 + nrm(ks[13], (DEPTH, D_MODEL), 0.02),
        'ln1_b': nrm(ks[14], (DEPTH, D_MODEL), 0.02),
        'w_up': nrm(ks[15], (DEPTH, D_MODEL, 2 * D_FF), D_MODEL ** -0.5),
        'conv_w': nrm(ks[16], (DEPTH, CONV_W, 2 * D_FF), CONV_W ** -0.5),
        'conv_b': nrm(ks[17], (DEPTH, 2 * D_FF), 0.02),
        'w_down': nrm(ks[18], (DEPTH, D_FF, D_MODEL), BETA * D_FF ** -0.5),
        'ln2_g': 1.0 + nrm(ks[19], (DEPTH, D_MODEL), 0.02),
        'ln2_b': nrm(ks[20], (DEPTH, D_MODEL), 0.02),
    }


def reference(x, c, ctx, c_ctx, w_ada, b_ada, w_in, q_norm, k_norm, hg_lb, hg_norm, ft_w, w_out,
              ln1_g, ln1_b, w_up, conv_w, conv_b, w_down, ln2_g, ln2_b):
    b, t, _ = x.shape
    rows = t // GRID_W
    row = jnp.repeat(jnp.arange(rows, dtype=jnp.int32), GRID_W)
    col = jnp.tile(jnp.arange(GRID_W, dtype=jnp.int32), rows)

    lb_soft = jax.nn.softmax(hg_lb.astype(jnp.float32), axis=0)
    lower = jnp.cumsum(lb_soft, axis=0) - lb_soft[:1]
    zero_state = jnp.zeros((b, HG_HEADS, HG_DIM, HG_DIM), jnp.float32)

    for l in range(DEPTH):
        need_ctx = l < DEPTH - 1

        mod = (jax.nn.silu(c) @ w_ada[l] + b_ada[l])[:, None, :]
        sh1, sc1, g1, sh2, sc2, g2 = jnp.split(mod, 6, axis=-1)
        mod_c = jax.nn.silu(c_ctx) @ w_ada[l] + b_ada[l]
        sh1c, sc1c, g1c, sh2c, sc2c, g2c = jnp.split(mod_c, 6, axis=-1)

        h = x * (1 + sc1) + sh1
        hc = ctx * (1 + sc1c) + sh1c
        q, k, v, hq, hi, hff, hfb, hg, ft = split_proj(h @ w_in[l])
        qc, kc, vc, hqc, hic, hffc, hfbc, hgc, ftc = split_proj(hc @ w_in[l])

        kc_h = rms_norm(to_heads(kc, ATTN_KV_HEADS), k_norm[l])
        vc_h = to_heads(vc, ATTN_KV_HEADS)
        q_h = axial_rope(rms_norm(to_heads(q, ATTN_HEADS), q_norm[l]), row, col)
        k_h = axial_rope(rms_norm(to_heads(k, ATTN_KV_HEADS), k_norm[l]), row, col)
        k_all = jnp.concatenate([kc_h, k_h], axis=2)
        v_all = jnp.concatenate([vc_h, to_heads(v, ATTN_KV_HEADS)], axis=2)
        attn = latent_attention(group_queries(q_h), k_all, v_all)

        lb_f, lb_b = lower[l, 0], lower[l, 1]
        qa_c, va_c = hgrn_qv(hqc, hic)
        lfc_f, kfc_f = hgrn_forget(hffc, lb_f)
        lfc_b, kfc_b = hgrn_forget(hfbc, lb_b)
        oc_f, s_f = gla_scan(qa_c, kfc_f, va_c, lfc_f, zero_state, False)
        oc_b, s_b = gla_scan(qa_c, kfc_b, va_c, lfc_b, zero_state, True)
        qa, va = hgrn_qv(hq, hi)
        lf_f, kf_f = hgrn_forget(hff, lb_f)
        lf_b, kf_b = hgrn_forget(hfb, lb_b)
        o_f, _ = gla_scan(qa, kf_f, va, lf_f, s_f, False)
        o_b, _ = gla_scan(qa, kf_b, va, lf_b, s_b, True)
        rec = hgrn_out(o_f + o_b, hg, hg_norm[l])

        four = fourier_mix(ft, ft_w[l])

        mix = jnp.concatenate([attn, rec, four], axis=-1) @ w_out[l]
        x1 = layer_norm(ALPHA * x + g1 * mix, ln1_g[l], ln1_b[l])
        ffn = conv_ffn(x1 * (1 + sc2) + sh2, w_up[l], conv_w[l], conv_b[l], w_down[l])
        x = layer_norm(ALPHA * x1 + g2 * ffn, ln2_g[l], ln2_b[l])

        if need_ctx:
            qc_h = group_queries(rms_norm(to_heads(qc, ATTN_HEADS), q_norm[l]))
            attn_c = attend(qc_h, kc_h, vc_h).transpose(0, 3, 1, 2, 4).reshape(b, ctx.shape[1], ATTN_WIDTH)
            rec_c = hgrn_out(oc_f + oc_b, hgc, hg_norm[l])
            four_c = fourier_mix(ftc, ft_w[l])
            mix_c = jnp.concatenate([attn_c, rec_c, four_c], axis=-1) @ w_out[l]
            ctx1 = layer_norm(ALPHA * ctx + g1c * mix_c, ln1_g[l], ln1_b[l])
            ffn_c = conv_ffn(ctx1 * (1 + sc2c) + sh2c, w_up[l], conv_w[l], conv_b[l], w_down[l])
            ctx = layer_norm(ALPHA * ctx1 + g2c * ffn_c, ln2_g[l], ln2_b[l])

    return x
```

```python
import functools

import numpy as np
import jax
import jax.numpy as jnp
from jax import lax
from jax.experimental import pallas as pl
from jax.experimental.pallas import tpu as pltpu

F32 = jnp.float32
BF16 = jnp.bfloat16

GRID_W = 64
HEAD_DIM = 64
ATTN_KV_HEADS = 2
HG_DIM = 64
FT_DIM = 64
ROPE_THETA = 10000.0
F_MIN = 1e-30
EPS = 1e-6
CONV_W = 3

HG_CHUNK = 64
HG_BLOCK = 16
HG_NB = HG_CHUNK // HG_BLOCK
HG_TILE = 512
NEG = -1e30
LOG2_E = 1.4426950408889634

V7X_VMEM_BYTES = 64 * 1024 * 1024
VMEM_LIMIT = int(V7X_VMEM_BYTES * 0.85)
BF16_ROWS = 16

_NT = (((1,), (1,)), ((), ()))


def _cparams(*sem):
    return pltpu.CompilerParams(dimension_semantics=sem, vmem_limit_bytes=VMEM_LIMIT)


def _dot(a, b):
    return jnp.dot(a, b, preferred_element_type=F32)


def _silu(x):
    return x * jax.nn.sigmoid(x)


def _layer_norm(y, g, b):
    mu = jnp.mean(y, axis=-1, keepdims=True)
    d = y - mu
    var = jnp.mean(d * d, axis=-1, keepdims=True)
    return d * lax.rsqrt(var + EPS) * g + b


def _full(shape):
    n = len(shape)
    return pl.BlockSpec(shape, lambda *_: (0,) * n)


def _resident(shape):
    n = len(shape)
    return pl.BlockSpec(shape, lambda *_: (0,) * n, pipeline_mode=pl.Buffered(1))


def _ada_kernel(c_ref, w_ref, b_ref, o_ref):
    s = _silu(c_ref[...]).astype(BF16)
    o_ref[0] = _dot(s, w_ref[0].astype(BF16)) + b_ref[0]


def _ada(cond, w_ada, b_ada):
    depth, d, n = w_ada.shape
    rows = cond.shape[0]
    tn = n // 4
    return pl.pallas_call(
        _ada_kernel,
        out_shape=jax.ShapeDtypeStruct((depth, rows, n), F32),
        grid=(depth, n // tn),
        in_specs=[
            pl.BlockSpec((rows, d), lambda l, j: (0, 0)),
            pl.BlockSpec((1, d, tn), lambda l, j: (l, 0, j)),
            pl.BlockSpec((1, 1, tn), lambda l, j: (l, 0, j)),
        ],
        out_specs=pl.BlockSpec((1, rows, tn), lambda l, j: (l, 0, j)),
        compiler_params=_cparams("arbitrary", "arbitrary"),
        name="ada",
    )(cond, w_ada, b_ada.reshape(depth, 1, n))


def _rope128(y, cos, sin_signed):
    lane = lax.broadcasted_iota(jnp.int32, y.shape, 1)
    upper = (lane % 32) >= 16
    partner = jnp.where(upper, pltpu.roll(y, 16, 1), pltpu.roll(y, 112, 1))
    return y * cos + partner * sin_signed


def _inproj_kernel(x_ref, sh_ref, sc_ref, w_ref, wvt_ref, cos_ref, sin_ref, qg_ref, kg_ref, bdq_ref, bdk_ref,
                   dftc_ref, q_ref, k_ref, v_ref, hq_ref, hi_ref, hg_ref, hf_ref, hit_ref, fab_ref, *, rope, offs):
    h = (x_ref[0] * (1.0 + sc_ref[0]) + sh_ref[0]).astype(BF16)

    def proj(name):
        lo, hi = offs[name]
        return _dot(h, w_ref[:, lo:hi])

    def head_norm(y, bd_ref, g):
        ms = jnp.concatenate(
            [_dot((y[:, c:c + bd_ref.shape[0]] ** 2).astype(BF16), bd_ref[...])
             for c in range(0, y.shape[1], bd_ref.shape[0])], axis=1)
        return y * lax.rsqrt(ms + EPS) * g

    def rotary(y):
        if not rope:
            return y
        cos = cos_ref[...]
        sin = sin_ref[...]
        return jnp.concatenate([_rope128(y[:, c:c + 128], cos, sin) for c in range(0, y.shape[1], 128)], axis=1)

    q = rotary(head_norm(proj("q"), bdq_ref, qg_ref[...]))
    q_ref[0] = (q * (HEAD_DIM ** -0.5 * LOG2_E)).astype(BF16)
    k = rotary(head_norm(proj("k"), bdk_ref, kg_ref[...]))
    k_ref[0] = k.astype(BF16)
    v_ref[0] = proj("v").astype(BF16)
    hq_ref[0] = proj("hq").astype(BF16)
    hi_ref[0] = proj("hi").astype(BF16)
    hg_ref[0] = proj("hg").astype(BF16)
    hf_ref[0] = proj("hf")
    vt = lax.dot_general(wvt_ref[...], h, _NT, preferred_element_type=F32).astype(BF16)
    for j in range(vt.shape[1] // HG_CHUNK):
        hit_ref[0, j] = vt[:, j * HG_CHUNK:(j + 1) * HG_CHUNK]
    ab = _dot(proj("ft").astype(BF16), dftc_ref[...])
    half = ab.shape[1] // 2
    fab_ref[0, 0] = ab[:, :half].astype(BF16)
    fab_ref[0, 1] = ab[:, half:].astype(BF16)


def _inproj(x, mod3, mod_row, l, w_in, w_vt, cos, sin, qg, kg, bdq, bdk, dftc, *, offs, rope, tm):
    b, t, d = x.shape
    aw = offs["q"][1] - offs["q"][0]
    kw = offs["k"][1] - offs["k"][0]
    hw = offs["hq"][1] - offs["hq"][0]
    fw = offs["ft"][1] - offs["ft"][0]
    row = lambda bi: mod_row(bi)
    bt = lambda width: pl.BlockSpec((1, tm, width), lambda bi, i: (bi, i, 0))
    out_shapes = (
        jax.ShapeDtypeStruct((b, t, aw), BF16), jax.ShapeDtypeStruct((b, t, kw), BF16),
        jax.ShapeDtypeStruct((b, t, kw), BF16), jax.ShapeDtypeStruct((b, t, hw), BF16),
        jax.ShapeDtypeStruct((b, t, hw), BF16), jax.ShapeDtypeStruct((b, t, hw), BF16),
        jax.ShapeDtypeStruct((b, t, 2 * hw), F32),
        jax.ShapeDtypeStruct((b, t // HG_CHUNK, hw, HG_CHUNK), BF16),
        jax.ShapeDtypeStruct((b, 2, t, fw), BF16),
    )
    out_specs = (
        bt(aw), bt(kw), bt(kw), bt(hw), bt(hw), bt(hw), bt(2 * hw),
        pl.BlockSpec((1, tm // HG_CHUNK, hw, HG_CHUNK), lambda bi, i: (bi, i, 0, 0)),
        pl.BlockSpec((1, 2, tm, fw), lambda bi, i: (bi, 0, i, 0)),
    )
    return pl.pallas_call(
        functools.partial(_inproj_kernel, rope=rope, offs=offs),
        out_shape=out_shapes,
        grid=(b, t // tm),
        in_specs=[
            pl.BlockSpec((1, tm, d), lambda bi, i: (bi, i, 0)),
            pl.BlockSpec((1, 1, d), lambda bi, i: (row(bi), 0, 0)),
            pl.BlockSpec((1, 1, d), lambda bi, i: (row(bi), 0, 1)),
            _full(w_in.shape), _full(w_vt.shape),
            pl.BlockSpec((tm, 128), lambda bi, i: (i, 0)),
            pl.BlockSpec((tm, 128), lambda bi, i: (i, 0)),
            _full(qg.shape), _full(kg.shape), _full(bdq.shape), _full(bdk.shape), _full(dftc.shape),
        ],
        out_specs=out_specs,
        compiler_params=_cparams("parallel", "parallel"),
        name="inproj",
    )(x, mod3, mod3, w_in, w_vt, cos, sin, qg, kg, bdq, bdk, dftc)


def _attn_kernel(q_ref, kt_ref, v_ref, o_ref, *, group):
    kt = kt_ref[0, 0]
    v = v_ref[0, 0]
    outs = []
    half = (v.shape[0] // 2) // 256 * 256
    for h in range(group):
        q = q_ref[0, :, h * HEAD_DIM:(h + 1) * HEAD_DIM]
        s = _dot(q, kt)
        m = jnp.max(s, axis=-1, keepdims=True)
        p = jnp.exp2(s - m)
        denom = jnp.sum(p, axis=-1, keepdims=True)
        p = p.astype(BF16)
        pv = _dot(p[:, :half], v[:half]) + _dot(p[:, half:], v[half:]) if half else _dot(p, v)
        outs.append(pv / denom)
    o_ref[0] = jnp.concatenate(outs, axis=1).astype(BF16)


def _attention(q, kt_all, v_all, *, tq):
    b, t, aw = q.shape
    kvh, s = v_all.shape[1], v_all.shape[2]
    group = aw // HEAD_DIM // kvh
    gw = group * HEAD_DIM
    return pl.pallas_call(
        functools.partial(_attn_kernel, group=group),
        out_shape=jax.ShapeDtypeStruct((b, t, aw), BF16),
        grid=(b, kvh, t // tq),
        in_specs=[
            pl.BlockSpec((1, tq, gw), lambda bi, g, i: (bi, i, g)),
            pl.BlockSpec((1, 1, HEAD_DIM, s), lambda bi, g, i: (bi, g, 0, 0)),
            pl.BlockSpec((1, 1, s, HEAD_DIM), lambda bi, g, i: (bi, g, 0, 0)),
        ],
        out_specs=pl.BlockSpec((1, tq, gw), lambda bi, g, i: (bi, i, g)),
        compiler_params=_cparams("parallel", "parallel", "parallel"),
        name="attention",
    )(q, kt_all, v_all)


def _hg_prepare(rev, hq, z, lb, tri_ref, sel_ref, qb_ref, ks_ref, ad_ref, dl_ref):
    r, w = z.shape
    n = r // HG_CHUNK
    f = lb + (1.0 - lb) * jax.nn.sigmoid(z)
    logf = jnp.log(jnp.maximum(f, F_MIN))
    kk = 1.0 - f
    qs = _silu(hq.astype(F32))
    hi = logf.astype(BF16)
    lo = (logf - hi.astype(F32)).astype(BF16)
    tri = tri_ref[0:r, 0:r]
    cum = _dot(tri, hi) + _dot(tri, lo)

    c3 = cum.reshape(n, HG_CHUNK, w)
    q3 = qs.reshape(n, HG_CHUNK, w)
    k3 = kk.reshape(n, HG_CHUNK, w)
    rowi = lax.broadcasted_iota(jnp.int32, (n, HG_CHUNK, w), 1)

    def ref_of(j):
        at = j * HG_BLOCK if rev else j * HG_BLOCK + HG_BLOCK - 1
        return c3[:, at:at + 1, :]

    e_last = c3[:, 0:1, :] if rev else c3[:, HG_CHUNK - 1:HG_CHUNK, :]
    qparts = [q3 * jnp.exp(c3)]
    kparts = []
    for g in range(1, HG_NB):
        j = HG_NB - g if rev else g - 1
        e = ref_of(j)
        past = (rowi < j * HG_BLOCK) if rev else (rowi >= (j + 1) * HG_BLOCK)
        inside = (rowi >= j * HG_BLOCK) & (rowi < (j + 1) * HG_BLOCK)
        qparts.append(q3 * jnp.exp(jnp.where(past, c3 - e, NEG)))
        kparts.append(k3 * jnp.exp(jnp.where(inside, e - c3, NEG)))
    kparts.append(k3 * jnp.exp(e_last - c3))
    qb_ref[0:r, :] = jnp.concatenate(qparts, axis=-1).reshape(r, HG_NB * w).astype(BF16)
    ks_ref[0:r, :] = jnp.concatenate(kparts, axis=-1).reshape(r, HG_NB * w).astype(BF16)
    dl_ref[0:n, :] = jnp.exp(e_last).reshape(n, w)

    nb = r // HG_BLOCK
    c4 = cum.reshape(nb, HG_BLOCK, w)
    q4 = qs.reshape(nb, HG_BLOCK, w)
    k4 = kk.reshape(nb, HG_BLOCK, w)
    tau = lax.broadcasted_iota(jnp.int32, (nb, HG_BLOCK, w), 1)
    eparts = []
    for sg in range(HG_BLOCK):
        keep = (tau <= sg) if rev else (tau >= sg)
        dec = jnp.exp(jnp.where(keep, c4 - c4[:, sg:sg + 1, :], NEG))
        eparts.append((dec * q4 * k4[:, sg:sg + 1, :]).reshape(r, w).astype(BF16))
    a = _dot(jnp.concatenate(eparts, axis=-1), sel_ref[...])
    ti = lax.broadcasted_iota(jnp.int32, (r, w), 0)
    ci = lax.broadcasted_iota(jnp.int32, (r, w), 1)
    same_block = ((ti % HG_CHUNK) // HG_BLOCK) == ((ci % HG_CHUNK) // HG_BLOCK)
    ad_ref[0:r, :] = jnp.where(same_block, a, 0.0).astype(BF16)


def _hg_step(qb_ref, ks_ref, ad_ref, dl_ref, st_ref, vt, v, r0, ci):
    w = v.shape[1]
    heads_r = lax.broadcasted_iota(jnp.int32, (w, w), 0) // HG_DIM
    heads_c = lax.broadcasted_iota(jnp.int32, (w, w), 1) // HG_DIM
    same_head = heads_r == heads_c
    x = _dot(vt, ks_ref[pl.ds(r0, HG_CHUNK), :])
    st = st_ref[...]
    wo = [st.astype(BF16)]
    for g in range(HG_NB - 1):
        wo.append(jnp.where(same_head, x[:, g * w:(g + 1) * w], 0.0).astype(BF16))
    wo = jnp.concatenate(wo, axis=1)
    vbd = jnp.where(same_head, jnp.concatenate([v] * (w // HG_CHUNK), axis=0), jnp.zeros((), v.dtype))
    o = lax.dot_general(qb_ref[pl.ds(r0, HG_CHUNK), :], wo, _NT, preferred_element_type=F32)
    o = o + _dot(ad_ref[pl.ds(r0, HG_CHUNK), :], vbd)
    st_ref[...] = st * dl_ref[pl.ds(ci, 1), :] + jnp.where(same_head, x[:, (HG_NB - 1) * w:], 0.0)
    return o


def _hgrn_kernel(hq_ref, hi_ref, hit_ref, hf_ref, hg_ref, hqc_ref, hic_ref, hitc_ref, hfc_ref, hgc_ref,
                 lb_ref, gn_ref, tri_ref, sel_ref, bd_ref, *rest, need_ctx):
    if need_ctx:
        rec_ref, recc_ref = rest[0], rest[1]
        scr = rest[2:]
    else:
        rec_ref, recc_ref = rest[0], None
        scr = rest[1:]
    (of_ref, ob_ref, ofc_ref, obc_ref, stf_ref, stb_ref,
     qbf_ref, ksf_ref, adf_ref, dlf_ref, qbb_ref, ksb_ref, adb_ref, dlb_ref) = scr
    w = hq_ref.shape[2]
    lb_f = lb_ref[0]
    lb_b = lb_ref[1]
    stf_ref[...] = jnp.zeros_like(stf_ref)
    stb_ref[...] = jnp.zeros_like(stb_ref)

    def run(hq, hi, hit, hf, o_f, o_b):
        t = hq.shape[1]
        tile = min(HG_TILE, t)
        nt = t // tile
        cpt = tile // HG_CHUNK
        for s in range(nt):
            rf = s * tile
            rb = (nt - 1 - s) * tile
            _hg_prepare(False, hq[0, rf:rf + tile, :], hf[0, rf:rf + tile, 0:w], lb_f, tri_ref.at[0], sel_ref,
                        qbf_ref, ksf_ref, adf_ref, dlf_ref)
            _hg_prepare(True, hq[0, rb:rb + tile, :], hf[0, rb:rb + tile, w:2 * w], lb_b, tri_ref.at[1], sel_ref,
                        qbb_ref, ksb_ref, adb_ref, dlb_ref)

            def body(i, carry):
                cf = i
                cb = cpt - 1 - i
                r0f = pl.multiple_of(cf * HG_CHUNK, HG_CHUNK)
                r0b = pl.multiple_of(cb * HG_CHUNK, HG_CHUNK)
                gf = pl.multiple_of(rf + cf * HG_CHUNK, HG_CHUNK)
                gb = pl.multiple_of(rb + cb * HG_CHUNK, HG_CHUNK)
                o_f[pl.ds(gf, HG_CHUNK), :] = _hg_step(
                    qbf_ref, ksf_ref, adf_ref, dlf_ref, stf_ref,
                    hit[0, rf // HG_CHUNK + cf], hi[0, pl.ds(gf, HG_CHUNK), :], r0f, cf)
                o_b[pl.ds(gb, HG_CHUNK), :] = _hg_step(
                    qbb_ref, ksb_ref, adb_ref, dlb_ref, stb_ref,
                    hit[0, rb // HG_CHUNK + cb], hi[0, pl.ds(gb, HG_CHUNK), :], r0b, cb)
                return carry

            lax.fori_loop(0, cpt, body, 0)

    def finish(o_f, o_b, hg, out):
        t = hg.shape[1]
        tile = min(HG_TILE, t)
        for s in range(t // tile):
            rows = slice(s * tile, (s + 1) * tile)
            o = o_f[rows, :] + o_b[rows, :]
            ms = _dot((o * o).astype(BF16), bd_ref[...])
            on = o * lax.rsqrt(ms + EPS) * gn_ref[...]
            out[0, rows, :] = (on * _silu(hg[0, rows, :].astype(F32))).astype(BF16)

    run(hqc_ref, hic_ref, hitc_ref, hfc_ref, ofc_ref, obc_ref)
    run(hq_ref, hi_ref, hit_ref, hf_ref, of_ref, ob_ref)
    finish(of_ref, ob_ref, hg_ref, rec_ref)
    if need_ctx:
        finish(ofc_ref, obc_ref, hgc_ref, recc_ref)


def _hgrn(lat, cx, lb, gn, tri, sel, bd, *, need_ctx):
    hq, hi, hit, hf, hg = lat
    hqc, hic, hitc, hfc, hgc = cx
    b, t, w = hq.shape
    tc = hqc.shape[1]
    tile = min(HG_TILE, t)

    def seq(a):
        nd = a.ndim
        return pl.BlockSpec((1,) + a.shape[1:], lambda bi: (bi,) + (0,) * (nd - 1))

    out_shape = [jax.ShapeDtypeStruct((b, t, w), BF16)]
    out_specs = [pl.BlockSpec((1, t, w), lambda bi: (bi, 0, 0))]
    if need_ctx:
        out_shape.append(jax.ShapeDtypeStruct((b, tc, w), BF16))
        out_specs.append(pl.BlockSpec((1, tc, w), lambda bi: (bi, 0, 0)))
    dir_scratch = [
        pltpu.VMEM((tile, HG_NB * w), BF16), pltpu.VMEM((tile, HG_NB * w), BF16),
        pltpu.VMEM((tile, w), BF16), pltpu.VMEM((tile // HG_CHUNK, w), F32),
    ]
    res = pl.pallas_call(
        functools.partial(_hgrn_kernel, need_ctx=need_ctx),
        out_shape=tuple(out_shape),
        grid=(b,),
        in_specs=[seq(hq), seq(hi), seq(hit), seq(hf), seq(hg), seq(hqc), seq(hic), seq(hitc), seq(hfc), seq(hgc),
                  _full(lb.shape), _full(gn.shape), _full(tri.shape), _full(sel.shape), _full(bd.shape)],
        out_specs=tuple(out_specs),
        scratch_shapes=[pltpu.VMEM((t, w), F32), pltpu.VMEM((t, w), F32),
                        pltpu.VMEM((tc, w), F32), pltpu.VMEM((tc, w), F32),
                        pltpu.VMEM((w, w), F32), pltpu.VMEM((w, w), F32)] + dir_scratch + dir_scratch,
        compiler_params=_cparams("parallel"),
        name="hgrn",
    )(hq, hi, hit, hf, hg, hqc, hic, hitc, hfc, hgc, lb, gn, tri, sel, bd)
    return res if need_ctx else (res[0], None)


def _dft_kernel(cs_ref, ab_ref, w_ref, o_ref):
    t = ab_ref.shape[2]
    z = _dot(cs_ref[:, 0:t], ab_ref[0, 0]) + _dot(cs_ref[:, t:2 * t], ab_ref[0, 1])
    o_ref[0] = _dot(z.astype(BF16), w_ref[...]).astype(BF16)


def _dft(cs, fab, wbd, *, tf):
    b, _, t, w = fab.shape
    return pl.pallas_call(
        _dft_kernel,
        out_shape=jax.ShapeDtypeStruct((b, t, w), BF16),
        grid=(t // tf, b),
        in_specs=[
            pl.BlockSpec((tf, 2 * t), lambda i, bi: (i, 0)),
            pl.BlockSpec((1, 2, t, w), lambda i, bi: (bi, 0, 0, 0)),
            _full(wbd.shape),
        ],
        out_specs=pl.BlockSpec((1, tf, w), lambda i, bi: (bi, i, 0)),
        compiler_params=_cparams("parallel", "parallel"),
        name="dft",
    )(cs, fab, wbd)


def _outproj_kernel(a_ref, r_ref, f_ref, x_ref, g1_ref, w_ref, lg_ref, lb_ref, o_ref, *, alpha):
    aw = a_ref.shape[2]
    rw = r_ref.shape[2]
    mix = (_dot(a_ref[0], w_ref[0:aw, :]) + _dot(r_ref[0], w_ref[aw:aw + rw, :])
           + _dot(f_ref[0], w_ref[aw + rw:, :]))
    o_ref[0] = _layer_norm(alpha * x_ref[0] + g1_ref[0] * mix, lg_ref[...], lb_ref[...])


def _outproj(attn, rec, four, x, mod3, mod_row, w_out, lg, lb, *, alpha, tm):
    b, t, d = x.shape
    bt = lambda a: pl.BlockSpec((1, tm, a.shape[2]), lambda bi, i: (bi, i, 0))
    return pl.pallas_call(
        functools.partial(_outproj_kernel, alpha=alpha),
        out_shape=jax.ShapeDtypeStruct((b, t, d), F32),
        grid=(b, t // tm),
        in_specs=[bt(attn), bt(rec), bt(four), bt(x),
                  pl.BlockSpec((1, 1, d), lambda bi, i: (mod_row(bi), 0, 2)),
                  _full(w_out.shape), _full(lg.shape), _full(lb.shape)],
        out_specs=pl.BlockSpec((1, tm, d), lambda bi, i: (bi, i, 0)),
        compiler_params=_cparams("parallel", "parallel"),
        name="outproj",
    )(attn, rec, four, x, mod3, w_out, lg, lb)


def _ffn_kernel(x_ref, xp_ref, xn_ref, sh_ref, sc_ref, g2_ref, wup_ref, cw_ref, cb_ref, wdn_ref, lg_ref, lb_ref,
                o_ref, h_scr, u_scr, act_scr, *, alpha):
    i = pl.program_id(1)
    last = pl.num_programs(1) - 1
    tm = x_ref.shape[1]
    halo = BF16_ROWS
    sc = sc_ref[0]
    sh = sh_ref[0]

    def mod(v):
        return (v * (1.0 + sc) + sh).astype(BF16)

    h_scr[0:halo, :] = jnp.where(i > 0, mod(xp_ref[0]), jnp.zeros((), BF16))
    h_scr[halo:halo + tm, :] = mod(x_ref[0])
    h_scr[halo + tm:, :] = jnp.where(i < last, mod(xn_ref[0]), jnp.zeros((), BF16))
    nff = wup_ref.shape[0]
    cf = wup_ref.shape[2] // 2
    for j in range(nff):
        ub = u_scr.at[j % 2]
        ub[...] = _dot(h_scr[...], wup_ref[j])
        cw = cw_ref[j]
        u = (ub[halo - 1:halo - 1 + tm, :] * cw[0:1, :] + ub[halo:halo + tm, :] * cw[1:2, :]
             + ub[halo + 1:halo + 1 + tm, :] * cw[2:3, :] + cb_ref[j])
        act_scr[:, j * cf:(j + 1) * cf] = (u[:, :cf] * _silu(u[:, cf:])).astype(BF16)
    ffn = _dot(act_scr[...], wdn_ref[...])
    o_ref[0] = _layer_norm(alpha * x_ref[0] + g2_ref[0] * ffn, lg_ref[...], lb_ref[...])


def _ffn(x1, mod3, mod_row, wup, cw, cb, wdn, lg, lb, *, alpha, tm):
    b, t, d = x1.shape
    halo = BF16_ROWS
    per = tm // halo
    nblk = t // halo
    nff, _, cf2 = wup.shape
    return pl.pallas_call(
        functools.partial(_ffn_kernel, alpha=alpha),
        out_shape=jax.ShapeDtypeStruct((b, t, d), F32),
        grid=(b, t // tm),
        in_specs=[
            pl.BlockSpec((1, tm, d), lambda bi, i: (bi, i, 0)),
            pl.BlockSpec((1, halo, d), lambda bi, i: (bi, jnp.maximum(i * per - 1, 0), 0)),
            pl.BlockSpec((1, halo, d), lambda bi, i: (bi, jnp.minimum((i + 1) * per, nblk - 1), 0)),
            pl.BlockSpec((1, 1, d), lambda bi, i: (mod_row(bi), 0, 3)),
            pl.BlockSpec((1, 1, d), lambda bi, i: (mod_row(bi), 0, 4)),
            pl.BlockSpec((1, 1, d), lambda bi, i: (mod_row(bi), 0, 5)),
            _resident(wup.shape), _full(cw.shape), _full(cb.shape), _resident(wdn.shape), _full(lg.shape),
            _full(lb.shape),
        ],
        out_specs=pl.BlockSpec((1, tm, d), lambda bi, i: (bi, i, 0)),
        scratch_shapes=[pltpu.VMEM((tm + 2 * halo, d), BF16), pltpu.VMEM((2, tm + 2 * halo, cf2), F32),
                        pltpu.VMEM((tm, wdn.shape[0]), BF16)],
        compiler_params=_cparams("parallel", "arbitrary"),
        name="ffn",
    )(x1, x1, x1, mod3, mod3, mod3, wup, cw, cb, wdn, lg, lb)


def _block_diag_mean(width, seg):
    i = np.arange(width)
    return jnp.asarray(((i[:, None] // seg) == (i[None, :] // seg)).astype(np.float32) / seg, dtype=BF16)


def _hg_tri(n):
    i = np.arange(n)
    same = (i[:, None] // HG_CHUNK) == (i[None, :] // HG_CHUNK)
    lower = same & (i[None, :] <= i[:, None])
    upper = same & (i[None, :] >= i[:, None])
    return jnp.asarray(np.stack([lower, upper]).astype(np.float32), dtype=BF16)


def _hg_select(w):
    r = np.arange(HG_BLOCK * w)
    c = np.arange(w)
    sg, head_r = r // w, (r % w) // HG_DIM
    head_c, s = c // HG_DIM, c % HG_DIM
    m = (head_r[:, None] == head_c[None, :]) & ((s[None, :] % HG_BLOCK) == sg[:, None])
    return jnp.asarray(m.astype(np.float32), dtype=BF16)


def _position_dft(t):
    lo = 64
    f = jnp.arange(t, dtype=jnp.int32)[:, None]
    step = 2.0 * np.pi / t
    a = ((f * (lo * jnp.arange(t // lo, dtype=jnp.int32))[None, :]) % t).astype(F32) * step
    bb = ((f * jnp.arange(lo, dtype=jnp.int32)[None, :]) % t).astype(F32) * step
    ca, sa = jnp.cos(a)[:, :, None], jnp.sin(a)[:, :, None]
    cb, sb = jnp.cos(bb)[:, None, :], jnp.sin(bb)[:, None, :]
    norm = 1.0 / np.sqrt(t)
    cos = ((ca * cb - sa * sb) * norm).reshape(t, t)
    msin = ((sa * cb + ca * sb) * -norm).reshape(t, t)
    return jnp.concatenate([cos, msin], axis=1).astype(BF16)


def _channel_dft(groups):
    c = np.arange(FT_DIM, dtype=np.int64)
    angc = 2.0 * np.pi * ((c[:, None] * c[None, :]) % FT_DIM).astype(np.float64) / FT_DIM
    eye = np.eye(groups)
    dftc = np.concatenate([np.kron(eye, np.cos(angc)), np.kron(eye, np.sin(angc))], axis=1) / np.sqrt(FT_DIM)
    return jnp.asarray(dftc, dtype=F32).astype(BF16)


def _rope_tables(t):
    rows = t // GRID_W
    row = jnp.repeat(jnp.arange(rows, dtype=jnp.int32), GRID_W)
    col = jnp.tile(jnp.arange(GRID_W, dtype=jnp.int32), rows)
    nf = HEAD_DIM // 4
    inv = ROPE_THETA ** (-jnp.arange(nf, dtype=F32) / nf)
    ar = row.astype(F32)[:, None] * inv
    ac = col.astype(F32)[:, None] * inv
    cos = jnp.concatenate([jnp.cos(ar)] * 2 + [jnp.cos(ac)] * 2, axis=1)
    sin = jnp.concatenate([-jnp.sin(ar), jnp.sin(ar), -jnp.sin(ac), jnp.sin(ac)], axis=1)
    return jnp.tile(cos, (1, 128 // HEAD_DIM)), jnp.tile(sin, (1, 128 // HEAD_DIM))


def _to_kv_heads(a):
    b, s, wdt = a.shape
    return a.reshape(b, s, wdt // HEAD_DIM, HEAD_DIM).transpose(0, 2, 1, 3)


def kernel(x, c, ctx, c_ctx, w_ada, b_ada, w_in, q_norm, k_norm, hg_lb, hg_norm, ft_w, w_out, ln1_g, ln1_b, w_up,
           conv_w, conv_b, w_down, ln2_g, ln2_b):
    b, t, d = x.shape
    tc = ctx.shape[1]
    depth = w_ada.shape[0]
    d_ff = w_down.shape[1]
    hgw = hg_lb.shape[2]
    ftw = ft_w.shape[1] * ft_w.shape[2]
    kvw = ATTN_KV_HEADS * HEAD_DIM
    aw = w_in.shape[2] - 2 * kvw - 5 * hgw - ftw
    alpha = (2 * depth) ** 0.25

    sizes = [("q", aw), ("k", kvw), ("v", kvw), ("hq", hgw), ("hi", hgw), ("hf", 2 * hgw), ("hg", hgw), ("ft", ftw)]
    offs, lo = {}, 0
    for name, size in sizes:
        offs[name] = (lo, lo + size)
        lo += size

    mod_rows = -(-(b + 1) // 8) * 8
    cond = jnp.concatenate([c, c_ctx[None, :], jnp.zeros((mod_rows - b - 1, d), F32)], axis=0)
    mod = _ada(cond, w_ada, b_ada)
    mod3 = mod.reshape(depth * mod_rows, 1, 6 * d)

    lb_soft = jax.nn.softmax(hg_lb.astype(F32), axis=0)
    lower = (jnp.cumsum(lb_soft, axis=0) - lb_soft[:1]).reshape(depth, 2, 1, hgw)

    cos, sin = _rope_tables(t)
    bdq = _block_diag_mean(2 * 128, HEAD_DIM)
    bdk = _block_diag_mean(kvw, HEAD_DIM)
    bdh = _block_diag_mean(hgw, HG_DIM)
    tri = _hg_tri(min(HG_TILE, t))
    sel = _hg_select(hgw)
    dftc = _channel_dft(ftw // FT_DIM)
    cs_lat = _position_dft(t)
    cs_ctx = _position_dft(tc)

    cf = 256
    nff = d_ff // cf

    for l in range(depth):
        need_ctx = l < depth - 1
        lat_row = lambda bi, l=l: l * mod_rows + bi
        ctx_row = lambda bi, l=l: l * mod_rows + b

        w_in_l = w_in[l].astype(BF16)
        w_vt = w_in[l][:, offs["hi"][0]:offs["hi"][1]].T.astype(BF16)
        qg = jnp.tile(q_norm[l].astype(F32), aw // HEAD_DIM)[None, :]
        kg = jnp.tile(k_norm[l].astype(F32), kvw // HEAD_DIM)[None, :]
        proj_args = (w_in_l, w_vt, cos, sin, qg, kg, bdq, bdk, dftc)
        q, k, v, hq, hi, hg, hf, hit, fab = _inproj(x, mod3, lat_row, l, *proj_args, offs=offs, rope=True, tm=512)
        qc, kc, vc, hqc, hic, hgc, hfc, hitc, fabc = _inproj(ctx, mod3, ctx_row, l, *proj_args, offs=offs,
                                                               rope=False, tm=tc)

        kc_t, vc_h = _to_kv_heads(kc).swapaxes(2, 3), _to_kv_heads(vc)
        kt_all = jnp.concatenate([kc_t, _to_kv_heads(k).swapaxes(2, 3)], axis=3)
        v_all = jnp.concatenate([vc_h, _to_kv_heads(v)], axis=2)
        attn = _attention(q, kt_all, v_all, tq=512)

        gn = jnp.tile(hg_norm[l].astype(F32), hgw // HG_DIM)[None, :]
        rec, rec_c = _hgrn((hq, hi, hit, hf, hg), (hqc, hic, hitc, hfc, hgc), lower[l], gn, tri, sel, bdh,
                           need_ctx=need_ctx)

        wbd = jax.scipy.linalg.block_diag(*[ft_w[l, g] for g in range(ft_w.shape[1])]).astype(BF16)
        four = _dft(cs_lat, fab, wbd, tf=512)

        w_out_l = w_out[l].astype(BF16)
        lg1, lb1 = ln1_g[l][None, :], ln1_b[l][None, :]
        lg2, lb2 = ln2_g[l][None, :], ln2_b[l][None, :]
        wup = w_up[l].reshape(d, 2, nff, cf).transpose(2, 0, 1, 3).reshape(nff, d, 2 * cf).astype(BF16)
        cw = conv_w[l].reshape(CONV_W, 2, nff, cf).transpose(2, 0, 1, 3).reshape(nff, CONV_W, 2 * cf)
        cb = conv_b[l].reshape(2, nff, cf).transpose(1, 0, 2).reshape(nff, 1, 2 * cf)
        wdn = w_down[l].astype(BF16)
        ffn_args = (wup, cw, cb, wdn, lg2, lb2)

        x1 = _outproj(attn, rec, four, x, mod3, lat_row, w_out_l, lg1, lb1, alpha=alpha, tm=512)
        x_next = _ffn(x1, mod3, lat_row, *ffn_args, alpha=alpha, tm=512)

        if need_ctx:
            attn_c = _attention(qc, kc_t, vc_h, tq=tc)
            four_c = _dft(cs_ctx, fabc, wbd, tf=tc)
            ctx1 = _outproj(attn_c, rec_c, four_c, ctx, mod3, ctx_row, w_out_l, lg1, lb1, alpha=alpha, tm=tc)
            ctx = _ffn(ctx1, mod3, ctx_row, *ffn_args, alpha=alpha, tm=tc)
        x = x_next

    return x
```

```python
import functools

import numpy as np
import jax
import jax.numpy as jnp
from jax import lax
from jax.experimental import pallas as pl
from jax.experimental.pallas import tpu as pltpu

F32 = jnp.float32
BF16 = jnp.bfloat16

GRID_W = 64
HEAD_DIM = 64
ATTN_KV_HEADS = 2
HG_DIM = 64
FT_DIM = 64
ROPE_THETA = 10000.0
F_MIN = 1e-30
EPS = 1e-6
CONV_W = 3

HG_CHUNK = 64
HG_BLOCK = 16
HG_NB = HG_CHUNK // HG_BLOCK
HG_PAIR = 2 * HG_DIM
HG_TILE = 512
NEG = -1e30
LOG2_E = 1.4426950408889634

V7X_VMEM_BYTES = 64 * 1024 * 1024
VMEM_LIMIT = int(V7X_VMEM_BYTES * 0.85)
BF16_ROWS = 16

_NT = (((1,), (1,)), ((), ()))


def _cparams(*sem):
    return pltpu.CompilerParams(dimension_semantics=sem, vmem_limit_bytes=VMEM_LIMIT)


def _dot(a, b):
    return jnp.dot(a, b, preferred_element_type=F32)


def _silu(x):
    return x * jax.nn.sigmoid(x)


def _layer_norm(y, g, b):
    mu = jnp.mean(y, axis=-1, keepdims=True)
    d = y - mu
    var = jnp.mean(d * d, axis=-1, keepdims=True)
    return d * lax.rsqrt(var + EPS) * g + b


def _full(shape):
    n = len(shape)
    return pl.BlockSpec(shape, lambda *_: (0,) * n)


def _resident(shape):
    n = len(shape)
    return pl.BlockSpec(shape, lambda *_: (0,) * n, pipeline_mode=pl.Buffered(1))


def _ada_kernel(c_ref, w_ref, b_ref, o_ref):
    s = _silu(c_ref[...]).astype(BF16)
    o_ref[0] = _dot(s, w_ref[0].astype(BF16)) + b_ref[0]


def _ada(cond, w_ada, b_ada):
    depth, d, n = w_ada.shape
    rows = cond.shape[0]
    tn = n // 4
    return pl.pallas_call(
        _ada_kernel,
        out_shape=jax.ShapeDtypeStruct((depth, rows, n), F32),
        grid=(depth, n // tn),
        in_specs=[
            pl.BlockSpec((rows, d), lambda l, j: (0, 0)),
            pl.BlockSpec((1, d, tn), lambda l, j: (l, 0, j)),
            pl.BlockSpec((1, 1, tn), lambda l, j: (l, 0, j)),
        ],
        out_specs=pl.BlockSpec((1, rows, tn), lambda l, j: (l, 0, j)),
        compiler_params=_cparams("arbitrary", "arbitrary"),
        name="ada",
    )(cond, w_ada, b_ada.reshape(depth, 1, n))


def _rope128(y, cos, sin_signed):
    lane = lax.broadcasted_iota(jnp.int32, y.shape, 1)
    upper = (lane % 32) >= 16
    partner = jnp.where(upper, pltpu.roll(y, 16, 1), pltpu.roll(y, 112, 1))
    return y * cos + partner * sin_signed


def _inproj_kernel(x_ref, sh_ref, sc_ref, w_ref, wvt_ref, cos_ref, sin_ref, qg_ref, kg_ref, bdq_ref, bdk_ref,
                   dftc_ref, q_ref, k_ref, v_ref, hq_ref, hi_ref, hg_ref, hf_ref, hit_ref, fab_ref, *, rope, offs):
    h = (x_ref[0] * (1.0 + sc_ref[0]) + sh_ref[0]).astype(BF16)

    def proj(name):
        lo, hi = offs[name]
        return _dot(h, w_ref[:, lo:hi])

    def head_norm(y, bd_ref, g):
        ms = jnp.concatenate(
            [_dot((y[:, c:c + bd_ref.shape[0]] ** 2).astype(BF16), bd_ref[...])
             for c in range(0, y.shape[1], bd_ref.shape[0])], axis=1)
        return y * lax.rsqrt(ms + EPS) * g

    def rotary(y):
        if not rope:
            return y
        cos = cos_ref[...]
        sin = sin_ref[...]
        return jnp.concatenate([_rope128(y[:, c:c + 128], cos, sin) for c in range(0, y.shape[1], 128)], axis=1)

    q = rotary(head_norm(proj("q"), bdq_ref, qg_ref[...]))
    q_ref[0] = (q * (HEAD_DIM ** -0.5 * LOG2_E)).astype(BF16)
    k = rotary(head_norm(proj("k"), bdk_ref, kg_ref[...]))
    k_ref[0] = k.astype(BF16)
    v_ref[0] = proj("v").astype(BF16)
    hq_ref[0] = proj("hq").astype(BF16)
    hi_ref[0] = proj("hi").astype(BF16)
    hg_ref[0] = proj("hg").astype(BF16)
    hf_ref[0] = proj("hf")
    vt = lax.dot_general(wvt_ref[...], h, _NT, preferred_element_type=F32).astype(BF16)
    for j in range(vt.shape[1] // HG_CHUNK):
        hit_ref[0, j] = vt[:, j * HG_CHUNK:(j + 1) * HG_CHUNK]
    ab = _dot(proj("ft").astype(BF16), dftc_ref[...])
    half = ab.shape[1] // 2
    fab_ref[0, 0] = ab[:, :half].astype(BF16)
    fab_ref[0, 1] = ab[:, half:].astype(BF16)


def _inproj(x, mod3, mod_row, l, w_in, w_vt, cos, sin, qg, kg, bdq, bdk, dftc, *, offs, rope, tm):
    b, t, d = x.shape
    aw = offs["q"][1] - offs["q"][0]
    kw = offs["k"][1] - offs["k"][0]
    hw = offs["hq"][1] - offs["hq"][0]
    fw = offs["ft"][1] - offs["ft"][0]
    row = lambda bi: mod_row(bi)
    bt = lambda width: pl.BlockSpec((1, tm, width), lambda bi, i: (bi, i, 0))
    out_shapes = (
        jax.ShapeDtypeStruct((b, t, aw), BF16), jax.ShapeDtypeStruct((b, t, kw), BF16),
        jax.ShapeDtypeStruct((b, t, kw), BF16), jax.ShapeDtypeStruct((b, t, hw), BF16),
        jax.ShapeDtypeStruct((b, t, hw), BF16), jax.ShapeDtypeStruct((b, t, hw), BF16),
        jax.ShapeDtypeStruct((b, t, 2 * hw), F32),
        jax.ShapeDtypeStruct((b, t // HG_CHUNK, hw, HG_CHUNK), BF16),
        jax.ShapeDtypeStruct((b, 2, t, fw), BF16),
    )
    out_specs = (
        bt(aw), bt(kw), bt(kw), bt(hw), bt(hw), bt(hw), bt(2 * hw),
        pl.BlockSpec((1, tm // HG_CHUNK, hw, HG_CHUNK), lambda bi, i: (bi, i, 0, 0)),
        pl.BlockSpec((1, 2, tm, fw), lambda bi, i: (bi, 0, i, 0)),
    )
    return pl.pallas_call(
        functools.partial(_inproj_kernel, rope=rope, offs=offs),
        out_shape=out_shapes,
        grid=(b, t // tm),
        in_specs=[
            pl.BlockSpec((1, tm, d), lambda bi, i: (bi, i, 0)),
            pl.BlockSpec((1, 1, d), lambda bi, i: (row(bi), 0, 0)),
            pl.BlockSpec((1, 1, d), lambda bi, i: (row(bi), 0, 1)),
            _full(w_in.shape), _full(w_vt.shape),
            pl.BlockSpec((tm, 128), lambda bi, i: (i, 0)),
            pl.BlockSpec((tm, 128), lambda bi, i: (i, 0)),
            _full(qg.shape), _full(kg.shape), _full(bdq.shape), _full(bdk.shape), _full(dftc.shape),
        ],
        out_specs=out_specs,
        compiler_params=_cparams("parallel", "parallel"),
        name="inproj",
    )(x, mod3, mod3, w_in, w_vt, cos, sin, qg, kg, bdq, bdk, dftc)


def _attn_kernel(q_ref, kt_ref, v_ref, o_ref, *, group):
    kt = kt_ref[0, 0]
    v = v_ref[0, 0]
    outs = []
    half = (v.shape[0] // 2) // 256 * 256
    for h in range(group):
        q = q_ref[0, :, h * HEAD_DIM:(h + 1) * HEAD_DIM]
        s = _dot(q, kt)
        m = jnp.max(s, axis=-1, keepdims=True)
        p = jnp.exp2(s - m)
        denom = jnp.sum(p, axis=-1, keepdims=True)
        p = p.astype(BF16)
        pv = _dot(p[:, :half], v[:half]) + _dot(p[:, half:], v[half:]) if half else _dot(p, v)
        outs.append(pv / denom)
    o_ref[0] = jnp.concatenate(outs, axis=1).astype(BF16)


def _attention(q, kt_all, v_all, *, tq):
    b, t, aw = q.shape
    kvh, s = v_all.shape[1], v_all.shape[2]
    group = aw // HEAD_DIM // kvh
    gw = group * HEAD_DIM
    return pl.pallas_call(
        functools.partial(_attn_kernel, group=group),
        out_shape=jax.ShapeDtypeStruct((b, t, aw), BF16),
        grid=(b, kvh, t // tq),
        in_specs=[
            pl.BlockSpec((1, tq, gw), lambda bi, g, i: (bi, i, g)),
            pl.BlockSpec((1, 1, HEAD_DIM, s), lambda bi, g, i: (bi, g, 0, 0)),
            pl.BlockSpec((1, 1, s, HEAD_DIM), lambda bi, g, i: (bi, g, 0, 0)),
        ],
        out_specs=pl.BlockSpec((1, tq, gw), lambda bi, g, i: (bi, i, g)),
        compiler_params=_cparams("parallel", "parallel", "parallel"),
        name="attention",
    )(q, kt_all, v_all)


def _hg_prepare(rev, hq, z, v, hit_ref, c0, lb, tri_ref, sel_ref, o_ref, row0, qb_ref, qs_ref, ks_ref, si_ref, dl_ref,
                d_ref):
    r, w = z.shape
    n = r // HG_CHUNK
    f = lb + (1.0 - lb) * jax.nn.sigmoid(z)
    logf = jnp.log2(jnp.maximum(f, F_MIN))
    kk = 1.0 - f
    qs = _silu(hq.astype(F32))
    hi = logf.astype(BF16)
    lo = (logf - hi.astype(F32)).astype(BF16)
    tri = tri_ref[0:r, 0:r]
    cum = _dot(tri, hi) + _dot(tri, lo)

    c3 = cum.reshape(n, HG_CHUNK, w)
    q3 = qs.reshape(n, HG_CHUNK, w)
    k3 = kk.reshape(n, HG_CHUNK, w)

    def rows(a, lo_, hi_):
        parts = []
        if lo_ > 0:
            parts.append(jnp.zeros((n, lo_, w), F32))
        parts.append(a)
        if hi_ < HG_CHUNK:
            parts.append(jnp.zeros((n, HG_CHUNK - hi_, w), F32))
        return jnp.concatenate(parts, axis=1) if len(parts) > 1 else a

    e_last = c3[:, 0:1, :] if rev else c3[:, HG_CHUNK - 1:HG_CHUNK, :]
    qparts, kparts = [], []
    for g in range(1, HG_NB):
        j = HG_NB - g if rev else g - 1
        b0, b1 = j * HG_BLOCK, (j + 1) * HG_BLOCK
        e = c3[:, b0:b0 + 1, :] if rev else c3[:, b1 - 1:b1, :]
        p0, p1 = (0, b0) if rev else (b1, HG_CHUNK)
        qparts.append(rows(q3[:, p0:p1, :] * jnp.exp2(c3[:, p0:p1, :] - e), p0, p1))
        kparts.append(rows(k3[:, b0:b1, :] * jnp.exp2(e - c3[:, b0:b1, :]), b0, b1))
    qparts.append(q3 * jnp.exp2(c3))
    kparts.append(k3 * jnp.exp2(e_last - c3))
    dl = jnp.exp2(e_last).reshape(n, w)
    pairs = w // HG_PAIR
    ng = HG_NB - 1
    for pp in range(pairs):
        lanes = slice(pp * HG_PAIR, (pp + 1) * HG_PAIR)
        qb_ref[pp, 0:r, :] = jnp.concatenate([p[:, :, lanes] for p in qparts[:ng]], axis=-1).reshape(
            r, ng * HG_PAIR).astype(BF16)
        qs_ref[pp, 0:r, :] = qparts[ng][:, :, lanes].reshape(r, HG_PAIR).astype(BF16)
        ks_ref[pp, 0:r, :] = jnp.concatenate([p[:, :, lanes] for p in kparts], axis=-1).reshape(
            r, HG_NB * HG_PAIR).astype(BF16)
        dl_ref[pp, 0:n, :] = dl[:, lanes]

    nb = r // HG_BLOCK
    hb = HG_BLOCK // 2
    c5 = cum.reshape(nb, 2, hb, w)
    q5 = qs.reshape(nb, 2, hb, w)
    k5 = kk.reshape(nb, 2, hb, w)
    tau = lax.broadcasted_iota(jnp.int32, (nb, hb, w), 1)
    zero_half = jnp.zeros((nb, hb, w), F32)
    eparts = []
    for sg in range(HG_BLOCK):
        sh, sl = sg // hb, sg % hb
        cs = c5[:, sh, sl:sl + 1, :]
        ksg = k5[:, sh, sl:sl + 1, :]
        halves = []
        for half in range(2):
            if (half > sh) if rev else (half < sh):
                halves.append(zero_half)
                continue
            arg = c5[:, half] - cs
            if half == sh:
                arg = jnp.where((tau <= sl) if rev else (tau >= sl), arg, NEG)
            halves.append(jnp.exp2(arg) * q5[:, half] * ksg)
        eparts.append(jnp.stack(halves, axis=1).reshape(r, w).astype(BF16))
    a = _dot(jnp.concatenate(eparts, axis=-1), sel_ref[...])
    ti = lax.broadcasted_iota(jnp.int32, (r, w), 0)
    ci = lax.broadcasted_iota(jnp.int32, (r, w), 1)
    same_block = ((ti % HG_CHUNK) // HG_BLOCK) == ((ci % HG_CHUNK) // HG_BLOCK)
    ad = jnp.where(same_block, a, 0.0).astype(BF16)

    def same_head(shape):
        return (lax.broadcasted_iota(jnp.int32, shape, 0) // HG_DIM) == (lax.broadcasted_iota(jnp.int32, shape, 1) // HG_DIM)

    pair_mask = same_head((HG_PAIR, HG_PAIR))
    head_mask = same_head((w, w))
    for c in range(n):
        rws = slice(c * HG_CHUNK, (c + 1) * HG_CHUNK)
        vt = hit_ref[0, c0 + c]
        for pp in range(pairs):
            x = _dot(vt[pp * HG_PAIR:(pp + 1) * HG_PAIR, :], ks_ref[pp, rws, :])
            d_ref[c, pp] = jnp.concatenate(
                [jnp.where(pair_mask, x[:, g * HG_PAIR:(g + 1) * HG_PAIR], 0.0).astype(BF16) for g in range(ng)],
                axis=1)
            si_ref[c, pp] = jnp.where(pair_mask, x[:, ng * HG_PAIR:], 0.0)
    for c in range(n):
        rws = slice(c * HG_CHUNK, (c + 1) * HG_CHUNK)
        outs = [lax.dot_general(qb_ref[pp, rws, :], d_ref[c, pp], _NT, preferred_element_type=F32)
                for pp in range(pairs)]
        vc = v[rws]
        vbd = jnp.where(head_mask, jnp.concatenate([vc] * (w // HG_CHUNK), axis=0), jnp.zeros((), vc.dtype))
        o_ref[row0 + c * HG_CHUNK:row0 + (c + 1) * HG_CHUNK, :] = jnp.concatenate(outs, axis=1) + _dot(ad[rws], vbd)


def _hg_step(qs_ref, si_ref, dl_ref, st_ref, o_ref, r0, g0, ci):
    outs = []
    for pp in range(st_ref.shape[0]):
        st = st_ref[pp]
        outs.append(lax.dot_general(qs_ref[pp, pl.ds(r0, HG_CHUNK), :], st.astype(BF16), _NT,
                                    preferred_element_type=F32))
        st_ref[pp] = st * dl_ref[pp, pl.ds(ci, 1), :] + si_ref[ci, pp]
    o_ref[pl.ds(g0, HG_CHUNK), :] += jnp.concatenate(outs, axis=1)


def _hgrn_kernel(hq_ref, hi_ref, hit_ref, hf_ref, hg_ref, hqc_ref, hic_ref, hitc_ref, hfc_ref, hgc_ref,
                 lb_ref, gn_ref, tri_ref, sel_ref, bd_ref, *rest, need_ctx):
    if need_ctx:
        rec_ref, recc_ref = rest[0], rest[1]
        scr = rest[2:]
    else:
        rec_ref, recc_ref = rest[0], None
        scr = rest[1:]
    of_ref, ob_ref, ofc_ref, obc_ref, stf_ref, stb_ref = scr[:6]
    fwd, bwd = scr[6:12], scr[12:18]
    w = hq_ref.shape[2]
    lb_f = lb_ref[0]
    lb_b = lb_ref[1]
    stf_ref[...] = jnp.zeros_like(stf_ref)
    stb_ref[...] = jnp.zeros_like(stb_ref)

    def run(hq, hi, hit, hf, o_f, o_b):
        t = hq.shape[1]
        tile = min(HG_TILE, t)
        nt = t // tile
        cpt = tile // HG_CHUNK
        for s in range(nt):
            rf = s * tile
            rb = (nt - 1 - s) * tile
            _hg_prepare(False, hq[0, rf:rf + tile, :], hf[0, rf:rf + tile, 0:w], hi[0, rf:rf + tile, :], hit,
                        rf // HG_CHUNK, lb_f, tri_ref.at[0], sel_ref, o_f, rf, *fwd)
            _hg_prepare(True, hq[0, rb:rb + tile, :], hf[0, rb:rb + tile, w:2 * w], hi[0, rb:rb + tile, :], hit,
                        rb // HG_CHUNK, lb_b, tri_ref.at[1], sel_ref, o_b, rb, *bwd)

            def body(i, carry):
                cf = i
                cb = cpt - 1 - i
                r0f = pl.multiple_of(cf * HG_CHUNK, HG_CHUNK)
                r0b = pl.multiple_of(cb * HG_CHUNK, HG_CHUNK)
                gf = pl.multiple_of(rf + cf * HG_CHUNK, HG_CHUNK)
                gb = pl.multiple_of(rb + cb * HG_CHUNK, HG_CHUNK)
                _hg_step(fwd[1], fwd[3], fwd[4], stf_ref, o_f, r0f, gf, cf)
                _hg_step(bwd[1], bwd[3], bwd[4], stb_ref, o_b, r0b, gb, cb)
                return carry

            lax.fori_loop(0, cpt, body, 0)

    def finish(o_f, o_b, hg, out):
        t = hg.shape[1]
        tile = min(HG_TILE, t)
        for s in range(t // tile):
            rows = slice(s * tile, (s + 1) * tile)
            o = o_f[rows, :] + o_b[rows, :]
            ms = _dot((o * o).astype(BF16), bd_ref[...])
            on = o * lax.rsqrt(ms + EPS) * gn_ref[...]
            out[0, rows, :] = (on * _silu(hg[0, rows, :].astype(F32))).astype(BF16)

    run(hqc_ref, hic_ref, hitc_ref, hfc_ref, ofc_ref, obc_ref)
    run(hq_ref, hi_ref, hit_ref, hf_ref, of_ref, ob_ref)
    finish(of_ref, ob_ref, hg_ref, rec_ref)
    if need_ctx:
        finish(ofc_ref, obc_ref, hgc_ref, recc_ref)


def _hgrn(lat, cx, lb, gn, tri, sel, bd, *, need_ctx):
    hq, hi, hit, hf, hg = lat
    hqc, hic, hitc, hfc, hgc = cx
    b, t, w = hq.shape
    tc = hqc.shape[1]
    tile = min(HG_TILE, t)

    def seq(a):
        nd = a.ndim
        return pl.BlockSpec((1,) + a.shape[1:], lambda bi: (bi,) + (0,) * (nd - 1))

    out_shape = [jax.ShapeDtypeStruct((b, t, w), BF16)]
    out_specs = [pl.BlockSpec((1, t, w), lambda bi: (bi, 0, 0))]
    if need_ctx:
        out_shape.append(jax.ShapeDtypeStruct((b, tc, w), BF16))
        out_specs.append(pl.BlockSpec((1, tc, w), lambda bi: (bi, 0, 0)))
    pairs = w // HG_PAIR
    cpt = tile // HG_CHUNK
    dir_scratch = [
        pltpu.VMEM((pairs, tile, (HG_NB - 1) * HG_PAIR), BF16),
        pltpu.VMEM((pairs, tile, HG_PAIR), BF16),
        pltpu.VMEM((pairs, tile, HG_NB * HG_PAIR), BF16),
        pltpu.VMEM((cpt, pairs, HG_PAIR, HG_PAIR), F32),
        pltpu.VMEM((pairs, cpt, HG_PAIR), F32),
        pltpu.VMEM((cpt, pairs, HG_PAIR, (HG_NB - 1) * HG_PAIR), BF16),
    ]
    res = pl.pallas_call(
        functools.partial(_hgrn_kernel, need_ctx=need_ctx),
        out_shape=tuple(out_shape),
        grid=(b,),
        in_specs=[seq(hq), seq(hi), seq(hit), seq(hf), seq(hg), seq(hqc), seq(hic), seq(hitc), seq(hfc), seq(hgc),
                  _full(lb.shape), _full(gn.shape), _full(tri.shape), _full(sel.shape), _full(bd.shape)],
        out_specs=tuple(out_specs),
        scratch_shapes=[pltpu.VMEM((t, w), F32), pltpu.VMEM((t, w), F32),
                        pltpu.VMEM((tc, w), F32), pltpu.VMEM((tc, w), F32),
                        pltpu.VMEM((pairs, HG_PAIR, HG_PAIR), F32), pltpu.VMEM((pairs, HG_PAIR, HG_PAIR), F32)]
        + dir_scratch + dir_scratch,
        compiler_params=_cparams("parallel"),
        name="hgrn",
    )(hq, hi, hit, hf, hg, hqc, hic, hitc, hfc, hgc, lb, gn, tri, sel, bd)
    return res if need_ctx else (res[0], None)


def _dft_kernel(cs_ref, ab_ref, w_ref, o_ref):
    t = ab_ref.shape[2]
    z = _dot(cs_ref[:, 0:t], ab_ref[0, 0]) + _dot(cs_ref[:, t:2 * t], ab_ref[0, 1])
    o_ref[0] = _dot(z.astype(BF16), w_ref[...]).astype(BF16)


def _dft(cs, fab, wbd, *, tf):
    b, _, t, w = fab.shape
    return pl.pallas_call(
        _dft_kernel,
        out_shape=jax.ShapeDtypeStruct((b, t, w), BF16),
        grid=(t // tf, b),
        in_specs=[
            pl.BlockSpec((tf, 2 * t), lambda i, bi: (i, 0)),
            pl.BlockSpec((1, 2, t, w), lambda i, bi: (bi, 0, 0, 0)),
            _full(wbd.shape),
        ],
        out_specs=pl.BlockSpec((1, tf, w), lambda i, bi: (bi, i, 0)),
        compiler_params=_cparams("parallel", "parallel"),
        name="dft",
    )(cs, fab, wbd)


def _outproj_kernel(a_ref, r_ref, f_ref, x_ref, g1_ref, w_ref, lg_ref, lb_ref, o_ref, *, alpha):
    aw = a_ref.shape[2]
    rw = r_ref.shape[2]
    mix = (_dot(a_ref[0], w_ref[0:aw, :]) + _dot(r_ref[0], w_ref[aw:aw + rw, :])
           + _dot(f_ref[0], w_ref[aw + rw:, :]))
    o_ref[0] = _layer_norm(alpha * x_ref[0] + g1_ref[0] * mix, lg_ref[...], lb_ref[...])


def _outproj(attn, rec, four, x, mod3, mod_row, w_out, lg, lb, *, alpha, tm):
    b, t, d = x.shape
    bt = lambda a: pl.BlockSpec((1, tm, a.shape[2]), lambda bi, i: (bi, i, 0))
    return pl.pallas_call(
        functools.partial(_outproj_kernel, alpha=alpha),
        out_shape=jax.ShapeDtypeStruct((b, t, d), F32),
        grid=(b, t // tm),
        in_specs=[bt(attn), bt(rec), bt(four), bt(x),
                  pl.BlockSpec((1, 1, d), lambda bi, i: (mod_row(bi), 0, 2)),
                  _full(w_out.shape), _full(lg.shape), _full(lb.shape)],
        out_specs=pl.BlockSpec((1, tm, d), lambda bi, i: (bi, i, 0)),
        compiler_params=_cparams("parallel", "parallel"),
        name="outproj",
    )(attn, rec, four, x, mod3, w_out, lg, lb)


def _ffn_kernel(x_ref, xp_ref, xn_ref, sh_ref, sc_ref, g2_ref, wup_ref, cw_ref, cb_ref, wdn_ref, lg_ref, lb_ref,
                o_ref, h_scr, u_scr, act_scr, *, alpha):
    i = pl.program_id(1)
    last = pl.num_programs(1) - 1
    tm = x_ref.shape[1]
    halo = BF16_ROWS
    sc = sc_ref[0]
    sh = sh_ref[0]

    def mod(v):
        return (v * (1.0 + sc) + sh).astype(BF16)

    h_scr[0:halo, :] = jnp.where(i > 0, mod(xp_ref[0]), jnp.zeros((), BF16))
    h_scr[halo:halo + tm, :] = mod(x_ref[0])
    h_scr[halo + tm:, :] = jnp.where(i < last, mod(xn_ref[0]), jnp.zeros((), BF16))
    nff = wup_ref.shape[0]
    cf = wup_ref.shape[2] // 2
    for j in range(nff):
        ub = u_scr.at[j % 2]
        ub[...] = _dot(h_scr[...], wup_ref[j])
        cw = cw_ref[j]
        u = (ub[halo - 1:halo - 1 + tm, :] * cw[0:1, :] + ub[halo:halo + tm, :] * cw[1:2, :]
             + ub[halo + 1:halo + 1 + tm, :] * cw[2:3, :] + cb_ref[j])
        act_scr[:, j * cf:(j + 1) * cf] = (u[:, :cf] * _silu(u[:, cf:])).astype(BF16)
    ffn = _dot(act_scr[...], wdn_ref[...])
    o_ref[0] = _layer_norm(alpha * x_ref[0] + g2_ref[0] * ffn, lg_ref[...], lb_ref[...])


def _ffn(x1, mod3, mod_row, wup, cw, cb, wdn, lg, lb, *, alpha, tm):
    b, t, d = x1.shape
    halo = BF16_ROWS
    per = tm // halo
    nblk = t // halo
    nff, _, cf2 = wup.shape
    return pl.pallas_call(
        functools.partial(_ffn_kernel, alpha=alpha),
        out_shape=jax.ShapeDtypeStruct((b, t, d), F32),
        grid=(b, t // tm),
        in_specs=[
            pl.BlockSpec((1, tm, d), lambda bi, i: (bi, i, 0)),
            pl.BlockSpec((1, halo, d), lambda bi, i: (bi, jnp.maximum(i * per - 1, 0), 0)),
            pl.BlockSpec((1, halo, d), lambda bi, i: (bi, jnp.minimum((i + 1) * per, nblk - 1), 0)),
            pl.BlockSpec((1, 1, d), lambda bi, i: (mod_row(bi), 0, 3)),
            pl.BlockSpec((1, 1, d), lambda bi, i: (mod_row(bi), 0, 4)),
            pl.BlockSpec((1, 1, d), lambda bi, i: (mod_row(bi), 0, 5)),
            _resident(wup.shape), _full(cw.shape), _full(cb.shape), _resident(wdn.shape), _full(lg.shape),
            _full(lb.shape),
        ],
        out_specs=pl.BlockSpec((1, tm, d), lambda bi, i: (bi, i, 0)),
        scratch_shapes=[pltpu.VMEM((tm + 2 * halo, d), BF16), pltpu.VMEM((2, tm + 2 * halo, cf2), F32),
                        pltpu.VMEM((tm, wdn.shape[0]), BF16)],
        compiler_params=_cparams("parallel", "arbitrary"),
        name="ffn",
    )(x1, x1, x1, mod3, mod3, mod3, wup, cw, cb, wdn, lg, lb)


def _block_diag_mean(width, seg):
    i = np.arange(width)
    return jnp.asarray(((i[:, None] // seg) == (i[None, :] // seg)).astype(np.float32) / seg, dtype=BF16)


def _hg_tri(n):
    i = np.arange(n)
    same = (i[:, None] // HG_CHUNK) == (i[None, :] // HG_CHUNK)
    lower = same & (i[None, :] <= i[:, None])
    upper = same & (i[None, :] >= i[:, None])
    return jnp.asarray(np.stack([lower, upper]).astype(np.float32), dtype=BF16)


def _hg_select(w):
    r = np.arange(HG_BLOCK * w)
    c = np.arange(w)
    sg, head_r = r // w, (r % w) // HG_DIM
    head_c, s = c // HG_DIM, c % HG_DIM
    m = (head_r[:, None] == head_c[None, :]) & ((s[None, :] % HG_BLOCK) == sg[:, None])
    return jnp.asarray(m.astype(np.float32), dtype=BF16)


def _position_dft(t):
    lo = 64
    f = jnp.arange(t, dtype=jnp.int32)[:, None]
    step = 2.0 * np.pi / t
    a = ((f * (lo * jnp.arange(t // lo, dtype=jnp.int32))[None, :]) % t).astype(F32) * step
    bb = ((f * jnp.arange(lo, dtype=jnp.int32)[None, :]) % t).astype(F32) * step
    ca, sa = jnp.cos(a)[:, :, None], jnp.sin(a)[:, :, None]
    cb, sb = jnp.cos(bb)[:, None, :], jnp.sin(bb)[:, None, :]
    norm = 1.0 / np.sqrt(t)
    cos = ((ca * cb - sa * sb) * norm).reshape(t, t)
    msin = ((sa * cb + ca * sb) * -norm).reshape(t, t)
    return jnp.concatenate([cos, msin], axis=1).astype(BF16)


def _channel_dft(groups):
    c = np.arange(FT_DIM, dtype=np.int64)
    angc = 2.0 * np.pi * ((c[:, None] * c[None, :]) % FT_DIM).astype(np.float64) / FT_DIM
    eye = np.eye(groups)
    dftc = np.concatenate([np.kron(eye, np.cos(angc)), np.kron(eye, np.sin(angc))], axis=1) / np.sqrt(FT_DIM)
    return jnp.asarray(dftc, dtype=F32).astype(BF16)


def _rope_tables(t):
    rows = t // GRID_W
    row = jnp.repeat(jnp.arange(rows, dtype=jnp.int32), GRID_W)
    col = jnp.tile(jnp.arange(GRID_W, dtype=jnp.int32), rows)
    nf = HEAD_DIM // 4
    inv = ROPE_THETA ** (-jnp.arange(nf, dtype=F32) / nf)
    ar = row.astype(F32)[:, None] * inv
    ac = col.astype(F32)[:, None] * inv
    cos = jnp.concatenate([jnp.cos(ar)] * 2 + [jnp.cos(ac)] * 2, axis=1)
    sin = jnp.concatenate([-jnp.sin(ar), jnp.sin(ar), -jnp.sin(ac), jnp.sin(ac)], axis=1)
    return jnp.tile(cos, (1, 128 // HEAD_DIM)), jnp.tile(sin, (1, 128 // HEAD_DIM))


def _to_kv_heads(a):
    b, s, wdt = a.shape
    return a.reshape(b, s, wdt // HEAD_DIM, HEAD_DIM).transpose(0, 2, 1, 3)


def kernel(x, c, ctx, c_ctx, w_ada, b_ada, w_in, q_norm, k_norm, hg_lb, hg_norm, ft_w, w_out, ln1_g, ln1_b, w_up,
           conv_w, conv_b, w_down, ln2_g, ln2_b):
    b, t, d = x.shape
    tc = ctx.shape[1]
    depth = w_ada.shape[0]
    d_ff = w_down.shape[1]
    hgw = hg_lb.shape[2]
    ftw = ft_w.shape[1] * ft_w.shape[2]
    kvw = ATTN_KV_HEADS * HEAD_DIM
    aw = w_in.shape[2] - 2 * kvw - 5 * hgw - ftw
    alpha = (2 * depth) ** 0.25

    sizes = [("q", aw), ("k", kvw), ("v", kvw), ("hq", hgw), ("hi", hgw), ("hf", 2 * hgw), ("hg", hgw), ("ft", ftw)]
    offs, lo = {}, 0
    for name, size in sizes:
        offs[name] = (lo, lo + size)
        lo += size

    mod_rows = -(-(b + 1) // 8) * 8
    cond = jnp.concatenate([c, c_ctx[None, :], jnp.zeros((mod_rows - b - 1, d), F32)], axis=0)
    mod = _ada(cond, w_ada, b_ada)
    mod3 = mod.reshape(depth * mod_rows, 1, 6 * d)

    lb_soft = jax.nn.softmax(hg_lb.astype(F32), axis=0)
    lower = (jnp.cumsum(lb_soft, axis=0) - lb_soft[:1]).reshape(depth, 2, 1, hgw)

    cos, sin = _rope_tables(t)
    bdq = _block_diag_mean(2 * 128, HEAD_DIM)
    bdk = _block_diag_mean(kvw, HEAD_DIM)
    bdh = _block_diag_mean(hgw, HG_DIM)
    tri = _hg_tri(min(HG_TILE, t))
    sel = _hg_select(hgw)
    dftc = _channel_dft(ftw // FT_DIM)
    cs_lat = _position_dft(t)
    cs_ctx = _position_dft(tc)

    cf = 256
    nff = d_ff // cf

    for l in range(depth):
        need_ctx = l < depth - 1
        lat_row = lambda bi, l=l: l * mod_rows + bi
        ctx_row = lambda bi, l=l: l * mod_rows + b

        w_in_l = w_in[l].astype(BF16)
        w_vt = w_in[l][:, offs["hi"][0]:offs["hi"][1]].T.astype(BF16)
        qg = jnp.tile(q_norm[l].astype(F32), aw // HEAD_DIM)[None, :]
        kg = jnp.tile(k_norm[l].astype(F32), kvw // HEAD_DIM)[None, :]
        proj_args = (w_in_l, w_vt, cos, sin, qg, kg, bdq, bdk, dftc)
        q, k, v, hq, hi, hg, hf, hit, fab = _inproj(x, mod3, lat_row, l, *proj_args, offs=offs, rope=True, tm=512)
        qc, kc, vc, hqc, hic, hgc, hfc, hitc, fabc = _inproj(ctx, mod3, ctx_row, l, *proj_args, offs=offs,
                                                               rope=False, tm=tc)

        kc_t, vc_h = _to_kv_heads(kc).swapaxes(2, 3), _to_kv_heads(vc)
        kt_all = jnp.concatenate([kc_t, _to_kv_heads(k).swapaxes(2, 3)], axis=3)
        v_all = jnp.concatenate([vc_h, _to_kv_heads(v)], axis=2)
        attn = _attention(q, kt_all, v_all, tq=512)

        gn = jnp.tile(hg_norm[l].astype(F32), hgw // HG_DIM)[None, :]
        rec, rec_c = _hgrn((hq, hi, hit, hf, hg), (hqc, hic, hitc, hfc, hgc), lower[l], gn, tri, sel, bdh,
                           need_ctx=need_ctx)

        wbd = jax.scipy.linalg.block_diag(*[ft_w[l, g] for g in range(ft_w.shape[1])]).astype(BF16)
        four = _dft(cs_lat, fab, wbd, tf=512)

        w_out_l = w_out[l].astype(BF16)
        lg1, lb1 = ln1_g[l][None, :], ln1_b[l][None, :]
        lg2, lb2 = ln2_g[l][None, :], ln2_b[l][None, :]
        wup = w_up[l].reshape(d, 2, nff, cf).transpose(2, 0, 1, 3).reshape(nff, d, 2 * cf).astype(BF16)
        cw = conv_w[l].reshape(CONV_W, 2, nff, cf).transpose(2, 0, 1, 3).reshape(nff, CONV_W, 2 * cf)
        cb = conv_b[l].reshape(2, nff, cf).transpose(1, 0, 2).reshape(nff, 1, 2 * cf)
        wdn = w_down[l].astype(BF16)
        ffn_args = (wup, cw, cb, wdn, lg2, lb2)

        x1 = _outproj(attn, rec, four, x, mod3, lat_row, w_out_l, lg1, lb1, alpha=alpha, tm=512)
        x_next = _ffn(x1, mod3, lat_row, *ffn_args, alpha=alpha, tm=512)

        if need_ctx:
            attn_c = _attention(qc, kc_t, vc_h, tq=tc)
            four_c = _dft(cs_ctx, fabc, wbd, tf=tc)
            ctx1 = _outproj(attn_c, rec_c, four_c, ctx, mod3, ctx_row, w_out_l, lg1, lb1, alpha=alpha, tm=tc)
            ctx = _ffn(ctx1, mod3, ctx_row, *ffn_args, alpha=alpha, tm=tc)
        x = x_next

    return x
```

```python
import functools

import numpy as np
import jax
import jax.numpy as jnp
from jax import lax
from jax.experimental import pallas as pl
from jax.experimental.pallas import tpu as pltpu

F32 = jnp.float32
BF16 = jnp.bfloat16

GRID_W = 64
HEAD_DIM = 64
ATTN_KV_HEADS = 2
HG_DIM = 64
FT_DIM = 64
ROPE_THETA = 10000.0
F_MIN = 1e-30
EPS = 1e-6
CONV_W = 3

HG_CHUNK = 64
HG_BLOCK = 16
HG_NB = HG_CHUNK // HG_BLOCK
HG_PAIR = 2 * HG_DIM
HG_TILE = 512
NEG = -1e30
LOG2_E = 1.4426950408889634

V7X_VMEM_BYTES = 64 * 1024 * 1024
VMEM_LIMIT = int(V7X_VMEM_BYTES * 0.85)
BF16_ROWS = 16

_NT = (((1,), (1,)), ((), ()))


def _cparams(*sem):
    return pltpu.CompilerParams(dimension_semantics=sem, vmem_limit_bytes=VMEM_LIMIT)


def _dot(a, b):
    return jnp.dot(a, b, preferred_element_type=F32)


def _silu(x):
    return x * jax.nn.sigmoid(x)


def _layer_norm(y, g, b):
    mu = jnp.mean(y, axis=-1, keepdims=True)
    d = y - mu
    var = jnp.mean(d * d, axis=-1, keepdims=True)
    return d * lax.rsqrt(var + EPS) * g + b


def _full(shape):
    n = len(shape)
    return pl.BlockSpec(shape, lambda *_: (0,) * n)


def _resident(shape):
    n = len(shape)
    return pl.BlockSpec(shape, lambda *_: (0,) * n, pipeline_mode=pl.Buffered(1))


def _ada_kernel(c_ref, w_ref, b_ref, o_ref):
    s = _silu(c_ref[...]).astype(BF16)
    o_ref[0] = _dot(s, w_ref[0].astype(BF16)) + b_ref[0]


def _ada(cond, w_ada, b_ada):
    depth, d, n = w_ada.shape
    rows = cond.shape[0]
    tn = n // 4
    return pl.pallas_call(
        _ada_kernel,
        out_shape=jax.ShapeDtypeStruct((depth, rows, n), F32),
        grid=(depth, n // tn),
        in_specs=[
            pl.BlockSpec((rows, d), lambda l, j: (0, 0)),
            pl.BlockSpec((1, d, tn), lambda l, j: (l, 0, j)),
            pl.BlockSpec((1, 1, tn), lambda l, j: (l, 0, j)),
        ],
        out_specs=pl.BlockSpec((1, rows, tn), lambda l, j: (l, 0, j)),
        compiler_params=_cparams("arbitrary", "arbitrary"),
        name="ada",
    )(cond, w_ada, b_ada.reshape(depth, 1, n))


def _rope128(y, cos, sin_signed):
    lane = lax.broadcasted_iota(jnp.int32, y.shape, 1)
    upper = (lane % 32) >= 16
    partner = jnp.where(upper, pltpu.roll(y, 16, 1), pltpu.roll(y, 112, 1))
    return y * cos + partner * sin_signed


def _inproj_kernel(x_ref, sh_ref, sc_ref, w_ref, wvt_ref, cos_ref, sin_ref, qg_ref, kg_ref, bdq_ref, bdk_ref,
                   dftc_ref, q_ref, k_ref, v_ref, hq_ref, hi_ref, hg_ref, hf_ref, hit_ref, fab_ref, *, rope, offs):
    h = (x_ref[0] * (1.0 + sc_ref[0]) + sh_ref[0]).astype(BF16)

    def proj(name):
        lo, hi = offs[name]
        return _dot(h, w_ref[:, lo:hi])

    def head_norm(y, bd_ref, g):
        ms = jnp.concatenate(
            [_dot((y[:, c:c + bd_ref.shape[0]] ** 2).astype(BF16), bd_ref[...])
             for c in range(0, y.shape[1], bd_ref.shape[0])], axis=1)
        return y * lax.rsqrt(ms + EPS) * g

    def rotary(y):
        if not rope:
            return y
        cos = cos_ref[...]
        sin = sin_ref[...]
        return jnp.concatenate([_rope128(y[:, c:c + 128], cos, sin) for c in range(0, y.shape[1], 128)], axis=1)

    q = rotary(head_norm(proj("q"), bdq_ref, qg_ref[...]))
    q_ref[0] = (q * (HEAD_DIM ** -0.5 * LOG2_E)).astype(BF16)
    k = rotary(head_norm(proj("k"), bdk_ref, kg_ref[...]))
    k_ref[0] = k.astype(BF16)
    v_ref[0] = proj("v").astype(BF16)
    hq_ref[0] = proj("hq").astype(BF16)
    hi_ref[0] = proj("hi").astype(BF16)
    hg_ref[0] = proj("hg").astype(BF16)
    hf_ref[0] = proj("hf")
    vt = lax.dot_general(wvt_ref[...], h, _NT, preferred_element_type=F32).astype(BF16)
    for j in range(vt.shape[1] // HG_CHUNK):
        hit_ref[0, j] = vt[:, j * HG_CHUNK:(j + 1) * HG_CHUNK]
    ab = _dot(proj("ft").astype(BF16), dftc_ref[...])
    half = ab.shape[1] // 2
    fab_ref[0, 0] = ab[:, :half].astype(BF16)
    fab_ref[0, 1] = ab[:, half:].astype(BF16)


def _inproj(x, mod3, mod_row, l, w_in, w_vt, cos, sin, qg, kg, bdq, bdk, dftc, *, offs, rope, tm):
    b, t, d = x.shape
    aw = offs["q"][1] - offs["q"][0]
    kw = offs["k"][1] - offs["k"][0]
    hw = offs["hq"][1] - offs["hq"][0]
    fw = offs["ft"][1] - offs["ft"][0]
    row = lambda bi: mod_row(bi)
    bt = lambda width: pl.BlockSpec((1, tm, width), lambda bi, i: (bi, i, 0))
    out_shapes = (
        jax.ShapeDtypeStruct((b, t, aw), BF16), jax.ShapeDtypeStruct((b, t, kw), BF16),
        jax.ShapeDtypeStruct((b, t, kw), BF16), jax.ShapeDtypeStruct((b, t, hw), BF16),
        jax.ShapeDtypeStruct((b, t, hw), BF16), jax.ShapeDtypeStruct((b, t, hw), BF16),
        jax.ShapeDtypeStruct((b, t, 2 * hw), F32),
        jax.ShapeDtypeStruct((b, t // HG_CHUNK, hw, HG_CHUNK), BF16),
        jax.ShapeDtypeStruct((b, 2, t, fw), BF16),
    )
    out_specs = (
        bt(aw), bt(kw), bt(kw), bt(hw), bt(hw), bt(hw), bt(2 * hw),
        pl.BlockSpec((1, tm // HG_CHUNK, hw, HG_CHUNK), lambda bi, i: (bi, i, 0, 0)),
        pl.BlockSpec((1, 2, tm, fw), lambda bi, i: (bi, 0, i, 0)),
    )
    return pl.pallas_call(
        functools.partial(_inproj_kernel, rope=rope, offs=offs),
        out_shape=out_shapes,
        grid=(b, t // tm),
        in_specs=[
            pl.BlockSpec((1, tm, d), lambda bi, i: (bi, i, 0)),
            pl.BlockSpec((1, 1, d), lambda bi, i: (row(bi), 0, 0)),
            pl.BlockSpec((1, 1, d), lambda bi, i: (row(bi), 0, 1)),
            _full(w_in.shape), _full(w_vt.shape),
            pl.BlockSpec((tm, 128), lambda bi, i: (i, 0)),
            pl.BlockSpec((tm, 128), lambda bi, i: (i, 0)),
            _full(qg.shape), _full(kg.shape), _full(bdq.shape), _full(bdk.shape), _full(dftc.shape),
        ],
        out_specs=out_specs,
        compiler_params=_cparams("parallel", "parallel"),
        name="inproj",
    )(x, mod3, mod3, w_in, w_vt, cos, sin, qg, kg, bdq, bdk, dftc)


def _attn_kernel(q_ref, kt_ref, v_ref, o_ref, s_scr, m_scr, p_scr, *, group):
    kt = kt_ref[0, 0]
    v = v_ref[0, 0]
    half = (v.shape[0] // 2) // 256 * 256

    def scores(h):
        s = _dot(q_ref[0, :, h * HEAD_DIM:(h + 1) * HEAD_DIM], kt)
        s_scr[h] = s
        m_scr[h] = jnp.max(s, axis=-1, keepdims=True)

    def probs(h):
        p_scr[h] = jnp.exp2(s_scr[h] - m_scr[h]).astype(BF16)

    def values(h):
        if half:
            pv = _dot(p_scr[h, :, :half], v[:half]) + _dot(p_scr[h, :, half:], v[half:])
        else:
            pv = _dot(p_scr[h], v)
        return pv[:, :HEAD_DIM] / pv[:, HEAD_DIM:HEAD_DIM + 1]

    outs = [None] * group
    for step in range(group + 2):
        if step < group:
            scores(step)
        if 0 <= step - 1 < group:
            probs(step - 1)
        if 0 <= step - 2 < group:
            outs[step - 2] = values(step - 2)
    o_ref[0] = jnp.concatenate(outs, axis=1).astype(BF16)


def _attention(q, kt_all, v_all, *, tq):
    b, t, aw = q.shape
    kvh, s = v_all.shape[1], v_all.shape[2]
    group = aw // HEAD_DIM // kvh
    gw = group * HEAD_DIM
    return pl.pallas_call(
        functools.partial(_attn_kernel, group=group),
        out_shape=jax.ShapeDtypeStruct((b, t, aw), BF16),
        grid=(b, kvh, t // tq),
        in_specs=[
            pl.BlockSpec((1, tq, gw), lambda bi, g, i: (bi, i, g)),
            pl.BlockSpec((1, 1, HEAD_DIM, s), lambda bi, g, i: (bi, g, 0, 0)),
            pl.BlockSpec((1, 1, s, 2 * HEAD_DIM), lambda bi, g, i: (bi, g, 0, 0)),
        ],
        out_specs=pl.BlockSpec((1, tq, gw), lambda bi, g, i: (bi, i, g)),
        scratch_shapes=[pltpu.VMEM((group, tq, s), F32), pltpu.VMEM((group, tq, 1), F32),
                        pltpu.VMEM((group, tq, s), BF16)],
        compiler_params=_cparams("parallel", "parallel", "parallel"),
        name="attention",
    )(q, kt_all, v_all)


def _hg_prepare(rev, hq, z, v, hit_ref, c0, lb, tri_ref, sel_ref, o_ref, row0, qb_ref, qs_ref, ks_ref, si_ref, dl_ref,
                d_ref):
    r, w = z.shape
    n = r // HG_CHUNK
    f = lb + (1.0 - lb) * jax.nn.sigmoid(z)
    logf = jnp.log2(jnp.maximum(f, F_MIN))
    kk = 1.0 - f
    qs = _silu(hq.astype(F32))
    hi = logf.astype(BF16)
    lo = (logf - hi.astype(F32)).astype(BF16)
    tri = tri_ref[0:r, 0:r]
    cum = _dot(tri, hi) + _dot(tri, lo)

    c3 = cum.reshape(n, HG_CHUNK, w)
    q3 = qs.reshape(n, HG_CHUNK, w)
    k3 = kk.reshape(n, HG_CHUNK, w)

    def rows(a, lo_, hi_):
        parts = []
        if lo_ > 0:
            parts.append(jnp.zeros((n, lo_, w), F32))
        parts.append(a)
        if hi_ < HG_CHUNK:
            parts.append(jnp.zeros((n, HG_CHUNK - hi_, w), F32))
        return jnp.concatenate(parts, axis=1) if len(parts) > 1 else a

    e_last = c3[:, 0:1, :] if rev else c3[:, HG_CHUNK - 1:HG_CHUNK, :]
    qparts, kparts = [], []
    for g in range(1, HG_NB):
        j = HG_NB - g if rev else g - 1
        b0, b1 = j * HG_BLOCK, (j + 1) * HG_BLOCK
        e = c3[:, b0:b0 + 1, :] if rev else c3[:, b1 - 1:b1, :]
        p0, p1 = (0, b0) if rev else (b1, HG_CHUNK)
        qparts.append(rows(q3[:, p0:p1, :] * jnp.exp2(c3[:, p0:p1, :] - e), p0, p1))
        kparts.append(rows(k3[:, b0:b1, :] * jnp.exp2(e - c3[:, b0:b1, :]), b0, b1))
    qparts.append(q3 * jnp.exp2(c3))
    kparts.append(k3 * jnp.exp2(e_last - c3))
    dl = jnp.exp2(e_last).reshape(n, w)
    pairs = w // HG_PAIR
    ng = HG_NB - 1
    for pp in range(pairs):
        lanes = slice(pp * HG_PAIR, (pp + 1) * HG_PAIR)
        qb_ref[pp, 0:r, :] = jnp.concatenate([p[:, :, lanes] for p in qparts[:ng]], axis=-1).reshape(
            r, ng * HG_PAIR).astype(BF16)
        qs_ref[pp, 0:r, :] = qparts[ng][:, :, lanes].reshape(r, HG_PAIR).astype(BF16)
        ks_ref[pp, 0:r, :] = jnp.concatenate([p[:, :, lanes] for p in kparts], axis=-1).reshape(
            r, HG_NB * HG_PAIR).astype(BF16)
        dl_ref[pp, 0:n, :] = dl[:, lanes]

    nb = r // HG_BLOCK
    hb = HG_BLOCK // 2
    c5 = cum.reshape(nb, 2, hb, w)
    q5 = qs.reshape(nb, 2, hb, w)
    k5 = kk.reshape(nb, 2, hb, w)
    tau = lax.broadcasted_iota(jnp.int32, (nb, hb, w), 1)
    zero_half = jnp.zeros((nb, hb, w), F32)
    eparts = []
    for sg in range(HG_BLOCK):
        sh, sl = sg // hb, sg % hb
        cs = c5[:, sh, sl:sl + 1, :]
        ksg = k5[:, sh, sl:sl + 1, :]
        halves = []
        for half in range(2):
            if (half > sh) if rev else (half < sh):
                halves.append(zero_half)
                continue
            arg = c5[:, half] - cs
            if half == sh:
                arg = jnp.where((tau <= sl) if rev else (tau >= sl), arg, NEG)
            halves.append(jnp.exp2(arg) * q5[:, half] * ksg)
        eparts.append(jnp.stack(halves, axis=1).reshape(r, w).astype(BF16))
    a = _dot(jnp.concatenate(eparts, axis=-1), sel_ref[...])
    ti = lax.broadcasted_iota(jnp.int32, (r, w), 0)
    ci = lax.broadcasted_iota(jnp.int32, (r, w), 1)
    same_block = ((ti % HG_CHUNK) // HG_BLOCK) == ((ci % HG_CHUNK) // HG_BLOCK)
    ad = jnp.where(same_block, a, 0.0).astype(BF16)

    def same_head(shape):
        return (lax.broadcasted_iota(jnp.int32, shape, 0) // HG_DIM) == (lax.broadcasted_iota(jnp.int32, shape, 1) // HG_DIM)

    pair_mask = same_head((HG_PAIR, HG_PAIR))
    head_mask = same_head((w, w))
    for c in range(n):
        rws = slice(c * HG_CHUNK, (c + 1) * HG_CHUNK)
        vt = hit_ref[0, c0 + c]
        for pp in range(pairs):
            x = _dot(vt[pp * HG_PAIR:(pp + 1) * HG_PAIR, :], ks_ref[pp, rws, :])
            d_ref[c, pp] = jnp.concatenate(
                [jnp.where(pair_mask, x[:, g * HG_PAIR:(g + 1) * HG_PAIR], 0.0).astype(BF16) for g in range(ng)],
                axis=1)
            si_ref[c, pp] = jnp.where(pair_mask, x[:, ng * HG_PAIR:], 0.0)
    for c in range(n):
        rws = slice(c * HG_CHUNK, (c + 1) * HG_CHUNK)
        outs = [lax.dot_general(qb_ref[pp, rws, :], d_ref[c, pp], _NT, preferred_element_type=F32)
                for pp in range(pairs)]
        vc = v[rws]
        vbd = jnp.where(head_mask, jnp.concatenate([vc] * (w // HG_CHUNK), axis=0), jnp.zeros((), vc.dtype))
        o_ref[row0 + c * HG_CHUNK:row0 + (c + 1) * HG_CHUNK, :] = jnp.concatenate(outs, axis=1) + _dot(ad[rws], vbd)


def _hg_step(qs_ref, si_ref, dl_ref, st_ref, o_ref, r0, g0, ci):
    outs = []
    for pp in range(st_ref.shape[0]):
        st = st_ref[pp]
        outs.append(lax.dot_general(qs_ref[pp, pl.ds(r0, HG_CHUNK), :], st.astype(BF16), _NT,
                                    preferred_element_type=F32))
        st_ref[pp] = st * dl_ref[pp, pl.ds(ci, 1), :] + si_ref[ci, pp]
    o_ref[pl.ds(g0, HG_CHUNK), :] += jnp.concatenate(outs, axis=1)


def _hgrn_kernel(hq_ref, hi_ref, hit_ref, hf_ref, hg_ref, hqc_ref, hic_ref, hitc_ref, hfc_ref, hgc_ref,
                 lb_ref, gn_ref, tri_ref, sel_ref, bd_ref, *rest, need_ctx):
    if need_ctx:
        rec_ref, recc_ref = rest[0], rest[1]
        scr = rest[2:]
    else:
        rec_ref, recc_ref = rest[0], None
        scr = rest[1:]
    of_ref, ob_ref, ofc_ref, obc_ref, stf_ref, stb_ref = scr[:6]
    fwd, bwd = scr[6:12], scr[12:18]
    w = hq_ref.shape[2]
    lb_f = lb_ref[0]
    lb_b = lb_ref[1]
    stf_ref[...] = jnp.zeros_like(stf_ref)
    stb_ref[...] = jnp.zeros_like(stb_ref)

    def run(hq, hi, hit, hf, o_f, o_b):
        t = hq.shape[1]
        tile = min(HG_TILE, t)
        nt = t // tile
        cpt = tile // HG_CHUNK
        for s in range(nt):
            rf = s * tile
            rb = (nt - 1 - s) * tile
            _hg_prepare(False, hq[0, rf:rf + tile, :], hf[0, rf:rf + tile, 0:w], hi[0, rf:rf + tile, :], hit,
                        rf // HG_CHUNK, lb_f, tri_ref.at[0], sel_ref, o_f, rf, *fwd)
            _hg_prepare(True, hq[0, rb:rb + tile, :], hf[0, rb:rb + tile, w:2 * w], hi[0, rb:rb + tile, :], hit,
                        rb // HG_CHUNK, lb_b, tri_ref.at[1], sel_ref, o_b, rb, *bwd)

            def body(i, carry):
                cf = i
                cb = cpt - 1 - i
                r0f = pl.multiple_of(cf * HG_CHUNK, HG_CHUNK)
                r0b = pl.multiple_of(cb * HG_CHUNK, HG_CHUNK)
                gf = pl.multiple_of(rf + cf * HG_CHUNK, HG_CHUNK)
                gb = pl.multiple_of(rb + cb * HG_CHUNK, HG_CHUNK)
                _hg_step(fwd[1], fwd[3], fwd[4], stf_ref, o_f, r0f, gf, cf)
                _hg_step(bwd[1], bwd[3], bwd[4], stb_ref, o_b, r0b, gb, cb)
                return carry

            lax.fori_loop(0, cpt, body, 0)

    def finish(o_f, o_b, hg, out):
        t = hg.shape[1]
        tile = min(HG_TILE, t)
        for s in range(t // tile):
            rows = slice(s * tile, (s + 1) * tile)
            o = o_f[rows, :] + o_b[rows, :]
            ms = _dot((o * o).astype(BF16), bd_ref[...])
            on = o * lax.rsqrt(ms + EPS) * gn_ref[...]
            out[0, rows, :] = (on * _silu(hg[0, rows, :].astype(F32))).astype(BF16)

    run(hqc_ref, hic_ref, hitc_ref, hfc_ref, ofc_ref, obc_ref)
    run(hq_ref, hi_ref, hit_ref, hf_ref, of_ref, ob_ref)
    finish(of_ref, ob_ref, hg_ref, rec_ref)
    if need_ctx:
        finish(ofc_ref, obc_ref, hgc_ref, recc_ref)


def _hgrn(lat, cx, lb, gn, tri, sel, bd, *, need_ctx):
    hq, hi, hit, hf, hg = lat
    hqc, hic, hitc, hfc, hgc = cx
    b, t, w = hq.shape
    tc = hqc.shape[1]
    tile = min(HG_TILE, t)

    def seq(a):
        nd = a.ndim
        return pl.BlockSpec((1,) + a.shape[1:], lambda bi: (bi,) + (0,) * (nd - 1))

    out_shape = [jax.ShapeDtypeStruct((b, t, w), BF16)]
    out_specs = [pl.BlockSpec((1, t, w), lambda bi: (bi, 0, 0))]
    if need_ctx:
        out_shape.append(jax.ShapeDtypeStruct((b, tc, w), BF16))
        out_specs.append(pl.BlockSpec((1, tc, w), lambda bi: (bi, 0, 0)))
    pairs = w // HG_PAIR
    cpt = tile // HG_CHUNK
    dir_scratch = [
        pltpu.VMEM((pairs, tile, (HG_NB - 1) * HG_PAIR), BF16),
        pltpu.VMEM((pairs, tile, HG_PAIR), BF16),
        pltpu.VMEM((pairs, tile, HG_NB * HG_PAIR), BF16),
        pltpu.VMEM((cpt, pairs, HG_PAIR, HG_PAIR), F32),
        pltpu.VMEM((pairs, cpt, HG_PAIR), F32),
        pltpu.VMEM((cpt, pairs, HG_PAIR, (HG_NB - 1) * HG_PAIR), BF16),
    ]
    res = pl.pallas_call(
        functools.partial(_hgrn_kernel, need_ctx=need_ctx),
        out_shape=tuple(out_shape),
        grid=(b,),
        in_specs=[seq(hq), seq(hi), seq(hit), seq(hf), seq(hg), seq(hqc), seq(hic), seq(hitc), seq(hfc), seq(hgc),
                  _full(lb.shape), _full(gn.shape), _full(tri.shape), _full(sel.shape), _full(bd.shape)],
        out_specs=tuple(out_specs),
        scratch_shapes=[pltpu.VMEM((t, w), F32), pltpu.VMEM((t, w), F32),
                        pltpu.VMEM((tc, w), F32), pltpu.VMEM((tc, w), F32),
                        pltpu.VMEM((pairs, HG_PAIR, HG_PAIR), F32), pltpu.VMEM((pairs, HG_PAIR, HG_PAIR), F32)]
        + dir_scratch + dir_scratch,
        compiler_params=_cparams("parallel"),
        name="hgrn",
    )(hq, hi, hit, hf, hg, hqc, hic, hitc, hfc, hgc, lb, gn, tri, sel, bd)
    return res if need_ctx else (res[0], None)


def _dft_kernel(cs_ref, ab_ref, w_ref, o_ref):
    t = ab_ref.shape[2]
    z = _dot(cs_ref[:, 0:t], ab_ref[0, 0]) + _dot(cs_ref[:, t:2 * t], ab_ref[0, 1])
    o_ref[0] = _dot(z.astype(BF16), w_ref[...]).astype(BF16)


def _dft(cs, fab, wbd, *, tf):
    b, _, t, w = fab.shape
    return pl.pallas_call(
        _dft_kernel,
        out_shape=jax.ShapeDtypeStruct((b, t, w), BF16),
        grid=(t // tf, b),
        in_specs=[
            pl.BlockSpec((tf, 2 * t), lambda i, bi: (i, 0)),
            pl.BlockSpec((1, 2, t, w), lambda i, bi: (bi, 0, 0, 0)),
            _full(wbd.shape),
        ],
        out_specs=pl.BlockSpec((1, tf, w), lambda i, bi: (bi, i, 0)),
        compiler_params=_cparams("parallel", "parallel"),
        name="dft",
    )(cs, fab, wbd)


def _outproj_kernel(a_ref, r_ref, f_ref, x_ref, g1_ref, w_ref, lg_ref, lb_ref, o_ref, *, alpha):
    aw = a_ref.shape[2]
    rw = r_ref.shape[2]
    mix = (_dot(a_ref[0], w_ref[0:aw, :]) + _dot(r_ref[0], w_ref[aw:aw + rw, :])
           + _dot(f_ref[0], w_ref[aw + rw:, :]))
    o_ref[0] = _layer_norm(alpha * x_ref[0] + g1_ref[0] * mix, lg_ref[...], lb_ref[...])


def _outproj(attn, rec, four, x, mod3, mod_row, w_out, lg, lb, *, alpha, tm):
    b, t, d = x.shape
    bt = lambda a: pl.BlockSpec((1, tm, a.shape[2]), lambda bi, i: (bi, i, 0))
    return pl.pallas_call(
        functools.partial(_outproj_kernel, alpha=alpha),
        out_shape=jax.ShapeDtypeStruct((b, t, d), F32),
        grid=(b, t // tm),
        in_specs=[bt(attn), bt(rec), bt(four), bt(x),
                  pl.BlockSpec((1, 1, d), lambda bi, i: (mod_row(bi), 0, 2)),
                  _full(w_out.shape), _full(lg.shape), _full(lb.shape)],
        out_specs=pl.BlockSpec((1, tm, d), lambda bi, i: (bi, i, 0)),
        compiler_params=_cparams("parallel", "parallel"),
        name="outproj",
    )(attn, rec, four, x, mod3, w_out, lg, lb)


def _ffn_kernel(x_ref, xp_ref, xn_ref, sh_ref, sc_ref, g2_ref, wup_ref, cw_ref, cb_ref, wdn_ref, lg_ref, lb_ref,
                o_ref, h_scr, u_scr, act_scr, *, alpha):
    i = pl.program_id(1)
    last = pl.num_programs(1) - 1
    tm = x_ref.shape[1]
    halo = BF16_ROWS
    sc = sc_ref[0]
    sh = sh_ref[0]

    def mod(v):
        return (v * (1.0 + sc) + sh).astype(BF16)

    h_scr[0:halo, :] = jnp.where(i > 0, mod(xp_ref[0]), jnp.zeros((), BF16))
    h_scr[halo:halo + tm, :] = mod(x_ref[0])
    h_scr[halo + tm:, :] = jnp.where(i < last, mod(xn_ref[0]), jnp.zeros((), BF16))
    d_ff = wdn_ref.shape[0]
    cf = u_scr.shape[3]

    def conv(ub, cols):
        cw = cw_ref[:, cols]
        return (ub[halo - 1:halo - 1 + tm, :] * cw[0:1, :] + ub[halo:halo + tm, :] * cw[1:2, :]
                + ub[halo + 1:halo + 1 + tm, :] * cw[2:3, :] + cb_ref[:, cols])

    for j in range(d_ff // cf):
        val_cols = slice(j * cf, (j + 1) * cf)
        gate_cols = slice(d_ff + j * cf, d_ff + (j + 1) * cf)
        uv, ug = u_scr.at[j % 2, 0], u_scr.at[j % 2, 1]
        uv[...] = _dot(h_scr[...], wup_ref[:, val_cols])
        ug[...] = _dot(h_scr[...], wup_ref[:, gate_cols])
        act_scr[:, val_cols] = (conv(uv, val_cols) * _silu(conv(ug, gate_cols))).astype(BF16)
    ffn = _dot(act_scr[...], wdn_ref[...])
    o_ref[0] = _layer_norm(alpha * x_ref[0] + g2_ref[0] * ffn, lg_ref[...], lb_ref[...])


def _ffn(x1, mod3, mod_row, wup, cw, cb, wdn, lg, lb, *, alpha, tm, cf):
    b, t, d = x1.shape
    halo = BF16_ROWS
    per = tm // halo
    nblk = t // halo
    return pl.pallas_call(
        functools.partial(_ffn_kernel, alpha=alpha),
        out_shape=jax.ShapeDtypeStruct((b, t, d), F32),
        grid=(b, t // tm),
        in_specs=[
            pl.BlockSpec((1, tm, d), lambda bi, i: (bi, i, 0)),
            pl.BlockSpec((1, halo, d), lambda bi, i: (bi, jnp.maximum(i * per - 1, 0), 0)),
            pl.BlockSpec((1, halo, d), lambda bi, i: (bi, jnp.minimum((i + 1) * per, nblk - 1), 0)),
            pl.BlockSpec((1, 1, d), lambda bi, i: (mod_row(bi), 0, 3)),
            pl.BlockSpec((1, 1, d), lambda bi, i: (mod_row(bi), 0, 4)),
            pl.BlockSpec((1, 1, d), lambda bi, i: (mod_row(bi), 0, 5)),
            _resident(wup.shape), _full(cw.shape), _full(cb.shape), _resident(wdn.shape), _full(lg.shape),
            _full(lb.shape),
        ],
        out_specs=pl.BlockSpec((1, tm, d), lambda bi, i: (bi, i, 0)),
        scratch_shapes=[pltpu.VMEM((tm + 2 * halo, d), BF16), pltpu.VMEM((2, 2, tm + 2 * halo, cf), F32),
                        pltpu.VMEM((tm, wdn.shape[0]), BF16)],
        compiler_params=_cparams("parallel", "arbitrary"),
        name="ffn",
    )(x1, x1, x1, mod3, mod3, mod3, wup, cw, cb, wdn, lg, lb)


def _block_diag_mean(width, seg):
    i = np.arange(width)
    return jnp.asarray(((i[:, None] // seg) == (i[None, :] // seg)).astype(np.float32) / seg, dtype=BF16)


def _hg_tri(n):
    i = np.arange(n)
    same = (i[:, None] // HG_CHUNK) == (i[None, :] // HG_CHUNK)
    lower = same & (i[None, :] <= i[:, None])
    upper = same & (i[None, :] >= i[:, None])
    return jnp.asarray(np.stack([lower, upper]).astype(np.float32), dtype=BF16)


def _hg_select(w):
    r = np.arange(HG_BLOCK * w)
    c = np.arange(w)
    sg, head_r = r // w, (r % w) // HG_DIM
    head_c, s = c // HG_DIM, c % HG_DIM
    m = (head_r[:, None] == head_c[None, :]) & ((s[None, :] % HG_BLOCK) == sg[:, None])
    return jnp.asarray(m.astype(np.float32), dtype=BF16)


def _position_dft(t):
    lo = 64
    f = jnp.arange(t, dtype=jnp.int32)[:, None]
    step = 2.0 * np.pi / t
    a = ((f * (lo * jnp.arange(t // lo, dtype=jnp.int32))[None, :]) % t).astype(F32) * step
    bb = ((f * jnp.arange(lo, dtype=jnp.int32)[None, :]) % t).astype(F32) * step
    ca, sa = jnp.cos(a)[:, :, None], jnp.sin(a)[:, :, None]
    cb, sb = jnp.cos(bb)[:, None, :], jnp.sin(bb)[:, None, :]
    norm = 1.0 / np.sqrt(t)
    cos = ((ca * cb - sa * sb) * norm).reshape(t, t)
    msin = ((sa * cb + ca * sb) * -norm).reshape(t, t)
    return jnp.concatenate([cos, msin], axis=1).astype(BF16)


def _channel_dft(groups):
    c = np.arange(FT_DIM, dtype=np.int64)
    angc = 2.0 * np.pi * ((c[:, None] * c[None, :]) % FT_DIM).astype(np.float64) / FT_DIM
    eye = np.eye(groups)
    dftc = np.concatenate([np.kron(eye, np.cos(angc)), np.kron(eye, np.sin(angc))], axis=1) / np.sqrt(FT_DIM)
    return jnp.asarray(dftc, dtype=F32).astype(BF16)


def _rope_tables(t):
    rows = t // GRID_W
    row = jnp.repeat(jnp.arange(rows, dtype=jnp.int32), GRID_W)
    col = jnp.tile(jnp.arange(GRID_W, dtype=jnp.int32), rows)
    nf = HEAD_DIM // 4
    inv = ROPE_THETA ** (-jnp.arange(nf, dtype=F32) / nf)
    ar = row.astype(F32)[:, None] * inv
    ac = col.astype(F32)[:, None] * inv
    cos = jnp.concatenate([jnp.cos(ar)] * 2 + [jnp.cos(ac)] * 2, axis=1)
    sin = jnp.concatenate([-jnp.sin(ar), jnp.sin(ar), -jnp.sin(ac), jnp.sin(ac)], axis=1)
    return jnp.tile(cos, (1, 128 // HEAD_DIM)), jnp.tile(sin, (1, 128 // HEAD_DIM))


def _to_kv_heads(a):
    b, s, wdt = a.shape
    return a.reshape(b, s, wdt // HEAD_DIM, HEAD_DIM).transpose(0, 2, 1, 3)


def _with_ones(v):
    ones = jnp.ones(v.shape[:-1] + (1,), v.dtype)
    zeros = jnp.zeros(v.shape[:-1] + (HEAD_DIM - 1,), v.dtype)
    return jnp.concatenate([v, ones, zeros], axis=-1)


def kernel(x, c, ctx, c_ctx, w_ada, b_ada, w_in, q_norm, k_norm, hg_lb, hg_norm, ft_w, w_out, ln1_g, ln1_b, w_up,
           conv_w, conv_b, w_down, ln2_g, ln2_b):
    b, t, d = x.shape
    tc = ctx.shape[1]
    depth = w_ada.shape[0]
    d_ff = w_down.shape[1]
    hgw = hg_lb.shape[2]
    ftw = ft_w.shape[1] * ft_w.shape[2]
    kvw = ATTN_KV_HEADS * HEAD_DIM
    aw = w_in.shape[2] - 2 * kvw - 5 * hgw - ftw
    alpha = (2 * depth) ** 0.25

    sizes = [("q", aw), ("k", kvw), ("v", kvw), ("hq", hgw), ("hi", hgw), ("hf", 2 * hgw), ("hg", hgw), ("ft", ftw)]
    offs, lo = {}, 0
    for name, size in sizes:
        offs[name] = (lo, lo + size)
        lo += size

    mod_rows = -(-(b + 1) // 8) * 8
    cond = jnp.concatenate([c, c_ctx[None, :], jnp.zeros((mod_rows - b - 1, d), F32)], axis=0)
    mod = _ada(cond, w_ada, b_ada)
    mod3 = mod.reshape(depth * mod_rows, 1, 6 * d)

    lb_soft = jax.nn.softmax(hg_lb.astype(F32), axis=0)
    lower = (jnp.cumsum(lb_soft, axis=0) - lb_soft[:1]).reshape(depth, 2, 1, hgw)

    cos, sin = _rope_tables(t)
    bdq = _block_diag_mean(2 * 128, HEAD_DIM)
    bdk = _block_diag_mean(kvw, HEAD_DIM)
    bdh = _block_diag_mean(hgw, HG_DIM)
    tri = _hg_tri(min(HG_TILE, t))
    sel = _hg_select(hgw)
    dftc = _channel_dft(ftw // FT_DIM)
    cs_lat = _position_dft(t)
    cs_ctx = _position_dft(tc)

    cf = 256
    assert d_ff % cf == 0

    for l in range(depth):
        need_ctx = l < depth - 1
        lat_row = lambda bi, l=l: l * mod_rows + bi
        ctx_row = lambda bi, l=l: l * mod_rows + b

        w_in_l = w_in[l].astype(BF16)
        w_vt = w_in[l][:, offs["hi"][0]:offs["hi"][1]].T.astype(BF16)
        qg = jnp.tile(q_norm[l].astype(F32), aw // HEAD_DIM)[None, :]
        kg = jnp.tile(k_norm[l].astype(F32), kvw // HEAD_DIM)[None, :]
        proj_args = (w_in_l, w_vt, cos, sin, qg, kg, bdq, bdk, dftc)
        q, k, v, hq, hi, hg, hf, hit, fab = _inproj(x, mod3, lat_row, l, *proj_args, offs=offs, rope=True, tm=512)
        qc, kc, vc, hqc, hic, hgc, hfc, hitc, fabc = _inproj(ctx, mod3, ctx_row, l, *proj_args, offs=offs,
                                                               rope=False, tm=tc)

        kc_t, vc_h = _to_kv_heads(kc).swapaxes(2, 3), _with_ones(_to_kv_heads(vc))
        kt_all = jnp.concatenate([kc_t, _to_kv_heads(k).swapaxes(2, 3)], axis=3)
        v_all = jnp.concatenate([vc_h, _with_ones(_to_kv_heads(v))], axis=2)
        attn = _attention(q, kt_all, v_all, tq=256)

        gn = jnp.tile(hg_norm[l].astype(F32), hgw // HG_DIM)[None, :]
        rec, rec_c = _hgrn((hq, hi, hit, hf, hg), (hqc, hic, hitc, hfc, hgc), lower[l], gn, tri, sel, bdh,
                           need_ctx=need_ctx)

        wbd = jax.scipy.linalg.block_diag(*[ft_w[l, g] for g in range(ft_w.shape[1])]).astype(BF16)
        four = _dft(cs_lat, fab, wbd, tf=512)

        w_out_l = w_out[l].astype(BF16)
        lg1, lb1 = ln1_g[l][None, :], ln1_b[l][None, :]
        lg2, lb2 = ln2_g[l][None, :], ln2_b[l][None, :]
        ffn_args = (w_up[l].astype(BF16), conv_w[l], conv_b[l][None, :], w_down[l].astype(BF16), lg2, lb2)

        x1 = _outproj(attn, rec, four, x, mod3, lat_row, w_out_l, lg1, lb1, alpha=alpha, tm=512)
        x_next = _ffn(x1, mod3, lat_row, *ffn_args, alpha=alpha, tm=512, cf=cf)

        if need_ctx:
            attn_c = _attention(qc, kc_t, vc_h, tq=tc)
            four_c = _dft(cs_ctx, fabc, wbd, tf=tc)
            ctx1 = _outproj(attn_c, rec_c, four_c, ctx, mod3, ctx_row, w_out_l, lg1, lb1, alpha=alpha, tm=tc)
            ctx = _ffn(ctx1, mod3, ctx_row, *ffn_args, alpha=alpha, tm=tc, cf=cf)
        x = x_next

    return x
```

```python
import functools

import numpy as np
import jax
import jax.numpy as jnp
from jax import lax
from jax.experimental import pallas as pl
from jax.experimental.pallas import tpu as pltpu

F32 = jnp.float32
BF16 = jnp.bfloat16

GRID_W = 64
HEAD_DIM = 64
ATTN_KV_HEADS = 2
HG_DIM = 64
FT_DIM = 64
ROPE_THETA = 10000.0
F_MIN = 1e-30
EPS = 1e-6
CONV_W = 3

HG_CHUNK = 64
HG_BLOCK = 16
HG_NB = HG_CHUNK // HG_BLOCK
HG_PAIR = 2 * HG_DIM
HG_TILE = 512
NEG = -1e30
LOG2_E = 1.4426950408889634

V7X_VMEM_BYTES = 64 * 1024 * 1024
VMEM_LIMIT = int(V7X_VMEM_BYTES * 0.85)
BF16_ROWS = 16

_NT = (((1,), (1,)), ((), ()))


def _cparams(*sem):
    return pltpu.CompilerParams(dimension_semantics=sem, vmem_limit_bytes=VMEM_LIMIT)


def _dot(a, b):
    return jnp.dot(a, b, preferred_element_type=F32)


def _silu(x):
    return x * jax.nn.sigmoid(x)


def _layer_norm(y, g, b):
    mu = jnp.mean(y, axis=-1, keepdims=True)
    d = y - mu
    var = jnp.mean(d * d, axis=-1, keepdims=True)
    return d * lax.rsqrt(var + EPS) * g + b


def _full(shape):
    n = len(shape)
    return pl.BlockSpec(shape, lambda *_: (0,) * n)


def _resident(shape):
    n = len(shape)
    return pl.BlockSpec(shape, lambda *_: (0,) * n, pipeline_mode=pl.Buffered(1))


def _ada_kernel(c_ref, w_ref, b_ref, o_ref):
    s = _silu(c_ref[...]).astype(BF16)
    o_ref[0] = _dot(s, w_ref[0].astype(BF16)) + b_ref[0]


def _ada(cond, w_ada, b_ada):
    depth, d, n = w_ada.shape
    rows = cond.shape[0]
    tn = n // 4
    return pl.pallas_call(
        _ada_kernel,
        out_shape=jax.ShapeDtypeStruct((depth, rows, n), F32),
        grid=(depth, n // tn),
        in_specs=[
            pl.BlockSpec((rows, d), lambda l, j: (0, 0)),
            pl.BlockSpec((1, d, tn), lambda l, j: (l, 0, j)),
            pl.BlockSpec((1, 1, tn), lambda l, j: (l, 0, j)),
        ],
        out_specs=pl.BlockSpec((1, rows, tn), lambda l, j: (l, 0, j)),
        compiler_params=_cparams("arbitrary", "arbitrary"),
        name="ada",
    )(cond, w_ada, b_ada.reshape(depth, 1, n))


def _rope128(y, cos, sin_signed):
    lane = lax.broadcasted_iota(jnp.int32, y.shape, 1)
    upper = (lane % 32) >= 16
    partner = jnp.where(upper, pltpu.roll(y, 16, 1), pltpu.roll(y, 112, 1))
    return y * cos + partner * sin_signed


def _inproj_kernel(x_ref, sh_ref, sc_ref, w_ref, wvt_ref, cos_ref, sin_ref, qg_ref, kg_ref, bdq_ref, bdk_ref,
                   dftc_ref, q_ref, k_ref, v_ref, hq_ref, hi_ref, hg_ref, hf_ref, hit_ref, fab_ref, *, rope, offs):
    h = (x_ref[0] * (1.0 + sc_ref[0]) + sh_ref[0]).astype(BF16)

    def proj(name):
        lo, hi = offs[name]
        return _dot(h, w_ref[:, lo:hi])

    def head_norm(y, bd_ref, g):
        ms = jnp.concatenate(
            [_dot((y[:, c:c + bd_ref.shape[0]] ** 2).astype(BF16), bd_ref[...])
             for c in range(0, y.shape[1], bd_ref.shape[0])], axis=1)
        return y * lax.rsqrt(ms + EPS) * g

    def rotary(y):
        if not rope:
            return y
        cos = cos_ref[...]
        sin = sin_ref[...]
        return jnp.concatenate([_rope128(y[:, c:c + 128], cos, sin) for c in range(0, y.shape[1], 128)], axis=1)

    q = rotary(head_norm(proj("q"), bdq_ref, qg_ref[...]))
    q_ref[0] = (q * (HEAD_DIM ** -0.5 * LOG2_E)).astype(BF16)
    k = rotary(head_norm(proj("k"), bdk_ref, kg_ref[...]))
    k_ref[0] = k.astype(BF16)
    v_ref[0] = proj("v").astype(BF16)
    hq_ref[0] = proj("hq").astype(BF16)
    hi_ref[0] = proj("hi").astype(BF16)
    hg_ref[0] = proj("hg").astype(BF16)
    hf_ref[0] = proj("hf")
    vt = lax.dot_general(wvt_ref[...], h, _NT, preferred_element_type=F32).astype(BF16)
    for j in range(vt.shape[1] // HG_CHUNK):
        hit_ref[0, j] = vt[:, j * HG_CHUNK:(j + 1) * HG_CHUNK]
    ab = _dot(proj("ft").astype(BF16), dftc_ref[...])
    half = ab.shape[1] // 2
    fab_ref[0, 0] = ab[:, :half].astype(BF16)
    fab_ref[0, 1] = ab[:, half:].astype(BF16)


def _inproj(x, mod3, mod_row, l, w_in, w_vt, cos, sin, qg, kg, bdq, bdk, dftc, *, offs, rope, tm):
    b, t, d = x.shape
    aw = offs["q"][1] - offs["q"][0]
    kw = offs["k"][1] - offs["k"][0]
    hw = offs["hq"][1] - offs["hq"][0]
    fw = offs["ft"][1] - offs["ft"][0]
    row = lambda bi: mod_row(bi)
    bt = lambda width: pl.BlockSpec((1, tm, width), lambda bi, i: (bi, i, 0))
    out_shapes = (
        jax.ShapeDtypeStruct((b, t, aw), BF16), jax.ShapeDtypeStruct((b, t, kw), BF16),
        jax.ShapeDtypeStruct((b, t, kw), BF16), jax.ShapeDtypeStruct((b, t, hw), BF16),
        jax.ShapeDtypeStruct((b, t, hw), BF16), jax.ShapeDtypeStruct((b, t, hw), BF16),
        jax.ShapeDtypeStruct((b, t, 2 * hw), F32),
        jax.ShapeDtypeStruct((b, t // HG_CHUNK, hw, HG_CHUNK), BF16),
        jax.ShapeDtypeStruct((b, 2, t, fw), BF16),
    )
    out_specs = (
        bt(aw), bt(kw), bt(kw), bt(hw), bt(hw), bt(hw), bt(2 * hw),
        pl.BlockSpec((1, tm // HG_CHUNK, hw, HG_CHUNK), lambda bi, i: (bi, i, 0, 0)),
        pl.BlockSpec((1, 2, tm, fw), lambda bi, i: (bi, 0, i, 0)),
    )
    return pl.pallas_call(
        functools.partial(_inproj_kernel, rope=rope, offs=offs),
        out_shape=out_shapes,
        grid=(b, t // tm),
        in_specs=[
            pl.BlockSpec((1, tm, d), lambda bi, i: (bi, i, 0)),
            pl.BlockSpec((1, 1, d), lambda bi, i: (row(bi), 0, 0)),
            pl.BlockSpec((1, 1, d), lambda bi, i: (row(bi), 0, 1)),
            _full(w_in.shape), _full(w_vt.shape),
            pl.BlockSpec((tm, 128), lambda bi, i: (i, 0)),
            pl.BlockSpec((tm, 128), lambda bi, i: (i, 0)),
            _full(qg.shape), _full(kg.shape), _full(bdq.shape), _full(bdk.shape), _full(dftc.shape),
        ],
        out_specs=out_specs,
        compiler_params=_cparams("parallel", "parallel"),
        name="inproj",
    )(x, mod3, mod3, w_in, w_vt, cos, sin, qg, kg, bdq, bdk, dftc)


def _attn_kernel(q_ref, kt_ref, v_ref, o_ref, s_scr, m_scr, p_scr, *, group):
    kt = kt_ref[0, 0]
    v = v_ref[0, 0]
    half = (v.shape[0] // 2) // 256 * 256

    def scores(h):
        s = _dot(q_ref[0, :, h * HEAD_DIM:(h + 1) * HEAD_DIM], kt)
        s_scr[h] = s
        m_scr[h] = jnp.max(s, axis=-1, keepdims=True)

    def probs(h):
        p_scr[h] = jnp.exp2(s_scr[h] - m_scr[h]).astype(BF16)

    def values(h):
        if half:
            pv = _dot(p_scr[h, :, :half], v[:half]) + _dot(p_scr[h, :, half:], v[half:])
        else:
            pv = _dot(p_scr[h], v)
        return pv[:, :HEAD_DIM] / pv[:, HEAD_DIM:HEAD_DIM + 1]

    outs = [None] * group
    for step in range(group + 2):
        if step < group:
            scores(step)
        if 0 <= step - 1 < group:
            probs(step - 1)
        if 0 <= step - 2 < group:
            outs[step - 2] = values(step - 2)
    o_ref[0] = jnp.concatenate(outs, axis=1).astype(BF16)


def _attention(q, kt_all, v_all, *, tq):
    b, t, aw = q.shape
    kvh, s = v_all.shape[1], v_all.shape[2]
    group = aw // HEAD_DIM // kvh
    gw = group * HEAD_DIM
    return pl.pallas_call(
        functools.partial(_attn_kernel, group=group),
        out_shape=jax.ShapeDtypeStruct((b, t, aw), BF16),
        grid=(b, kvh, t // tq),
        in_specs=[
            pl.BlockSpec((1, tq, gw), lambda bi, g, i: (bi, i, g)),
            pl.BlockSpec((1, 1, HEAD_DIM, s), lambda bi, g, i: (bi, g, 0, 0)),
            pl.BlockSpec((1, 1, s, 2 * HEAD_DIM), lambda bi, g, i: (bi, g, 0, 0)),
        ],
        out_specs=pl.BlockSpec((1, tq, gw), lambda bi, g, i: (bi, i, g)),
        scratch_shapes=[pltpu.VMEM((group, tq, s), F32), pltpu.VMEM((group, tq, 1), F32),
                        pltpu.VMEM((group, tq, s), BF16)],
        compiler_params=_cparams("parallel", "parallel", "parallel"),
        name="attention",
    )(q, kt_all, v_all)


def _hg_prepare(rev, hq, z, v, hit_ref, c0, lb, tri_ref, sel_ref, o_ref, row0, qb_ref, qs_ref, ks_ref, si_ref, dl_ref,
                d_ref):
    r, w = z.shape
    n = r // HG_CHUNK
    f = lb + (1.0 - lb) * jax.nn.sigmoid(z)
    logf = jnp.log2(jnp.maximum(f, F_MIN))
    kk = 1.0 - f
    qs = _silu(hq.astype(F32))
    hi = logf.astype(BF16)
    lo = (logf - hi.astype(F32)).astype(BF16)
    tri = tri_ref[0:r, 0:r]
    cum = _dot(tri, hi) + _dot(tri, lo)

    c3 = cum.reshape(n, HG_CHUNK, w)
    q3 = qs.reshape(n, HG_CHUNK, w)
    k3 = kk.reshape(n, HG_CHUNK, w)

    def rows(a, lo_, hi_):
        parts = []
        if lo_ > 0:
            parts.append(jnp.zeros((n, lo_, w), F32))
        parts.append(a)
        if hi_ < HG_CHUNK:
            parts.append(jnp.zeros((n, HG_CHUNK - hi_, w), F32))
        return jnp.concatenate(parts, axis=1) if len(parts) > 1 else a

    e_last = c3[:, 0:1, :] if rev else c3[:, HG_CHUNK - 1:HG_CHUNK, :]
    qparts, kparts = [], []
    for g in range(1, HG_NB):
        j = HG_NB - g if rev else g - 1
        b0, b1 = j * HG_BLOCK, (j + 1) * HG_BLOCK
        e = c3[:, b0:b0 + 1, :] if rev else c3[:, b1 - 1:b1, :]
        p0, p1 = (0, b0) if rev else (b1, HG_CHUNK)
        qparts.append(rows(q3[:, p0:p1, :] * jnp.exp2(c3[:, p0:p1, :] - e), p0, p1))
        kparts.append(rows(k3[:, b0:b1, :] * jnp.exp2(e - c3[:, b0:b1, :]), b0, b1))
    qparts.append(q3 * jnp.exp2(c3))
    kparts.append(k3 * jnp.exp2(e_last - c3))
    dl = jnp.exp2(e_last).reshape(n, w)
    pairs = w // HG_PAIR
    ng = HG_NB - 1
    for pp in range(pairs):
        lanes = slice(pp * HG_PAIR, (pp + 1) * HG_PAIR)
        qb_ref[pp, 0:r, :] = jnp.concatenate([p[:, :, lanes] for p in qparts[:ng]], axis=-1).reshape(
            r, ng * HG_PAIR).astype(BF16)
        qs_ref[pp, 0:r, :] = qparts[ng][:, :, lanes].reshape(r, HG_PAIR).astype(BF16)
        ks_ref[pp, 0:r, :] = jnp.concatenate([p[:, :, lanes] for p in kparts], axis=-1).reshape(
            r, HG_NB * HG_PAIR).astype(BF16)
        dl_ref[pp, 0:n, :] = dl[:, lanes]

    nb = r // HG_BLOCK
    hb = HG_BLOCK // 2
    c5 = cum.reshape(nb, 2, hb, w)
    q5 = qs.reshape(nb, 2, hb, w)
    d5 = (cum - jnp.log2(jnp.maximum(kk, 0.0))).reshape(nb, 2, hb, w)
    tau = lax.broadcasted_iota(jnp.int32, (nb, hb, w), 1)
    halves = []
    for half in range(2):
        sources = [sg for sg in range(HG_BLOCK) if not ((half > sg // hb) if rev else (half < sg // hb))]
        eparts = []
        for sg in sources:
            sh, sl = sg // hb, sg % hb
            arg = c5[:, half] - d5[:, sh, sl:sl + 1, :]
            if half == sh:
                arg = jnp.where((tau <= sl) if rev else (tau >= sl), arg, NEG)
            eparts.append((jnp.exp2(arg) * q5[:, half]).reshape(nb * hb, w).astype(BF16))
        lo_ = sources[0] * w
        halves.append(_dot(jnp.concatenate(eparts, axis=-1), sel_ref[lo_:lo_ + len(sources) * w, :]))
    a = jnp.stack([h_.reshape(nb, hb, w) for h_ in halves], axis=1).reshape(r, w)
    ti = lax.broadcasted_iota(jnp.int32, (r, w), 0)
    ci = lax.broadcasted_iota(jnp.int32, (r, w), 1)
    same_block = ((ti % HG_CHUNK) // HG_BLOCK) == ((ci % HG_CHUNK) // HG_BLOCK)
    ad = jnp.where(same_block, a, 0.0).astype(BF16)

    def same_head(shape):
        return (lax.broadcasted_iota(jnp.int32, shape, 0) // HG_DIM) == (lax.broadcasted_iota(jnp.int32, shape, 1) // HG_DIM)

    pair_mask = same_head((HG_PAIR, HG_PAIR))
    head_mask = same_head((w, w))
    for c in range(n):
        rws = slice(c * HG_CHUNK, (c + 1) * HG_CHUNK)
        vt = hit_ref[0, c0 + c]
        for pp in range(pairs):
            x = _dot(vt[pp * HG_PAIR:(pp + 1) * HG_PAIR, :], ks_ref[pp, rws, :])
            d_ref[c, pp] = jnp.concatenate(
                [jnp.where(pair_mask, x[:, g * HG_PAIR:(g + 1) * HG_PAIR], 0.0).astype(BF16) for g in range(ng)],
                axis=1)
            si_ref[c, pp] = jnp.where(pair_mask, x[:, ng * HG_PAIR:], 0.0)
    for c in range(n):
        rws = slice(c * HG_CHUNK, (c + 1) * HG_CHUNK)
        outs = [lax.dot_general(qb_ref[pp, rws, :], d_ref[c, pp], _NT, preferred_element_type=F32)
                for pp in range(pairs)]
        vc = v[rws]
        vbd = jnp.where(head_mask, jnp.concatenate([vc] * (w // HG_CHUNK), axis=0), jnp.zeros((), vc.dtype))
        o_ref[row0 + c * HG_CHUNK:row0 + (c + 1) * HG_CHUNK, :] = jnp.concatenate(outs, axis=1) + _dot(ad[rws], vbd)


def _hg_step(qs_ref, si_ref, dl_ref, st_ref, o_ref, r0, g0, ci):
    outs = []
    for pp in range(st_ref.shape[0]):
        st = st_ref[pp]
        outs.append(lax.dot_general(qs_ref[pp, pl.ds(r0, HG_CHUNK), :], st.astype(BF16), _NT,
                                    preferred_element_type=F32))
        st_ref[pp] = st * dl_ref[pp, pl.ds(ci, 1), :] + si_ref[ci, pp]
    o_ref[pl.ds(g0, HG_CHUNK), :] += jnp.concatenate(outs, axis=1)


def _hgrn_kernel(hq_ref, hi_ref, hit_ref, hf_ref, hg_ref, hqc_ref, hic_ref, hitc_ref, hfc_ref, hgc_ref,
                 lb_ref, gn_ref, tri_ref, sel_ref, bd_ref, *rest, need_ctx):
    if need_ctx:
        rec_ref, recc_ref = rest[0], rest[1]
        scr = rest[2:]
    else:
        rec_ref, recc_ref = rest[0], None
        scr = rest[1:]
    of_ref, ob_ref, ofc_ref, obc_ref, stf_ref, stb_ref = scr[:6]
    fwd, bwd = scr[6:12], scr[12:18]
    w = hq_ref.shape[2]
    lb_f = lb_ref[0]
    lb_b = lb_ref[1]
    stf_ref[...] = jnp.zeros_like(stf_ref)
    stb_ref[...] = jnp.zeros_like(stb_ref)

    def run(hq, hi, hit, hf, o_f, o_b):
        t = hq.shape[1]
        tile = min(HG_TILE, t)
        nt = t // tile
        cpt = tile // HG_CHUNK
        for s in range(nt):
            rf = s * tile
            rb = (nt - 1 - s) * tile
            _hg_prepare(False, hq[0, rf:rf + tile, :], hf[0, rf:rf + tile, 0:w], hi[0, rf:rf + tile, :], hit,
                        rf // HG_CHUNK, lb_f, tri_ref.at[0], sel_ref, o_f, rf, *fwd)
            _hg_prepare(True, hq[0, rb:rb + tile, :], hf[0, rb:rb + tile, w:2 * w], hi[0, rb:rb + tile, :], hit,
                        rb // HG_CHUNK, lb_b, tri_ref.at[1], sel_ref, o_b, rb, *bwd)

            def body(i, carry):
                cf = i
                cb = cpt - 1 - i
                r0f = pl.multiple_of(cf * HG_CHUNK, HG_CHUNK)
                r0b = pl.multiple_of(cb * HG_CHUNK, HG_CHUNK)
                gf = pl.multiple_of(rf + cf * HG_CHUNK, HG_CHUNK)
                gb = pl.multiple_of(rb + cb * HG_CHUNK, HG_CHUNK)
                _hg_step(fwd[1], fwd[3], fwd[4], stf_ref, o_f, r0f, gf, cf)
                _hg_step(bwd[1], bwd[3], bwd[4], stb_ref, o_b, r0b, gb, cb)
                return carry

            lax.fori_loop(0, cpt, body, 0)

    def finish(o_f, o_b, hg, out):
        t = hg.shape[1]
        tile = min(HG_TILE, t)
        for s in range(t // tile):
            rows = slice(s * tile, (s + 1) * tile)
            o = o_f[rows, :] + o_b[rows, :]
            ms = _dot((o * o).astype(BF16), bd_ref[...])
            on = o * lax.rsqrt(ms + EPS) * gn_ref[...]
            out[0, rows, :] = (on * _silu(hg[0, rows, :].astype(F32))).astype(BF16)

    run(hqc_ref, hic_ref, hitc_ref, hfc_ref, ofc_ref, obc_ref)
    run(hq_ref, hi_ref, hit_ref, hf_ref, of_ref, ob_ref)
    finish(of_ref, ob_ref, hg_ref, rec_ref)
    if need_ctx:
        finish(ofc_ref, obc_ref, hgc_ref, recc_ref)


def _hgrn(lat, cx, lb, gn, tri, sel, bd, *, need_ctx):
    hq, hi, hit, hf, hg = lat
    hqc, hic, hitc, hfc, hgc = cx
    b, t, w = hq.shape
    tc = hqc.shape[1]
    tile = min(HG_TILE, t)

    def seq(a):
        nd = a.ndim
        return pl.BlockSpec((1,) + a.shape[1:], lambda bi: (bi,) + (0,) * (nd - 1))

    out_shape = [jax.ShapeDtypeStruct((b, t, w), BF16)]
    out_specs = [pl.BlockSpec((1, t, w), lambda bi: (bi, 0, 0))]
    if need_ctx:
        out_shape.append(jax.ShapeDtypeStruct((b, tc, w), BF16))
        out_specs.append(pl.BlockSpec((1, tc, w), lambda bi: (bi, 0, 0)))
    pairs = w // HG_PAIR
    cpt = tile // HG_CHUNK
    dir_scratch = [
        pltpu.VMEM((pairs, tile, (HG_NB - 1) * HG_PAIR), BF16),
        pltpu.VMEM((pairs, tile, HG_PAIR), BF16),
        pltpu.VMEM((pairs, tile, HG_NB * HG_PAIR), BF16),
        pltpu.VMEM((cpt, pairs, HG_PAIR, HG_PAIR), F32),
        pltpu.VMEM((pairs, cpt, HG_PAIR), F32),
        pltpu.VMEM((cpt, pairs, HG_PAIR, (HG_NB - 1) * HG_PAIR), BF16),
    ]
    res = pl.pallas_call(
        functools.partial(_hgrn_kernel, need_ctx=need_ctx),
        out_shape=tuple(out_shape),
        grid=(b,),
        in_specs=[seq(hq), seq(hi), seq(hit), seq(hf), seq(hg), seq(hqc), seq(hic), seq(hitc), seq(hfc), seq(hgc),
                  _full(lb.shape), _full(gn.shape), _full(tri.shape), _full(sel.shape), _full(bd.shape)],
        out_specs=tuple(out_specs),
        scratch_shapes=[pltpu.VMEM((t, w), F32), pltpu.VMEM((t, w), F32),
                        pltpu.VMEM((tc, w), F32), pltpu.VMEM((tc, w), F32),
                        pltpu.VMEM((pairs, HG_PAIR, HG_PAIR), F32), pltpu.VMEM((pairs, HG_PAIR, HG_PAIR), F32)]
        + dir_scratch + dir_scratch,
        compiler_params=_cparams("parallel"),
        name="hgrn",
    )(hq, hi, hit, hf, hg, hqc, hic, hitc, hfc, hgc, lb, gn, tri, sel, bd)
    return res if need_ctx else (res[0], None)


def _dft_kernel(cs_ref, ab_ref, w_ref, o_ref):
    t = ab_ref.shape[2]
    z = _dot(cs_ref[:, 0:t], ab_ref[0, 0]) + _dot(cs_ref[:, t:2 * t], ab_ref[0, 1])
    o_ref[0] = _dot(z.astype(BF16), w_ref[...]).astype(BF16)


def _dft(cs, fab, wbd, *, tf):
    b, _, t, w = fab.shape
    return pl.pallas_call(
        _dft_kernel,
        out_shape=jax.ShapeDtypeStruct((b, t, w), BF16),
        grid=(t // tf, b),
        in_specs=[
            pl.BlockSpec((tf, 2 * t), lambda i, bi: (i, 0)),
            pl.BlockSpec((1, 2, t, w), lambda i, bi: (bi, 0, 0, 0)),
            _full(wbd.shape),
        ],
        out_specs=pl.BlockSpec((1, tf, w), lambda i, bi: (bi, i, 0)),
        compiler_params=_cparams("parallel", "parallel"),
        name="dft",
    )(cs, fab, wbd)


def _outproj_kernel(a_ref, r_ref, f_ref, x_ref, g1_ref, w_ref, lg_ref, lb_ref, o_ref, *, alpha):
    aw = a_ref.shape[2]
    rw = r_ref.shape[2]
    mix = (_dot(a_ref[0], w_ref[0:aw, :]) + _dot(r_ref[0], w_ref[aw:aw + rw, :])
           + _dot(f_ref[0], w_ref[aw + rw:, :]))
    o_ref[0] = _layer_norm(alpha * x_ref[0] + g1_ref[0] * mix, lg_ref[...], lb_ref[...])


def _outproj(attn, rec, four, x, mod3, mod_row, w_out, lg, lb, *, alpha, tm):
    b, t, d = x.shape
    bt = lambda a: pl.BlockSpec((1, tm, a.shape[2]), lambda bi, i: (bi, i, 0))
    return pl.pallas_call(
        functools.partial(_outproj_kernel, alpha=alpha),
        out_shape=jax.ShapeDtypeStruct((b, t, d), F32),
        grid=(b, t // tm),
        in_specs=[bt(attn), bt(rec), bt(four), bt(x),
                  pl.BlockSpec((1, 1, d), lambda bi, i: (mod_row(bi), 0, 2)),
                  _full(w_out.shape), _full(lg.shape), _full(lb.shape)],
        out_specs=pl.BlockSpec((1, tm, d), lambda bi, i: (bi, i, 0)),
        compiler_params=_cparams("parallel", "parallel"),
        name="outproj",
    )(attn, rec, four, x, mod3, w_out, lg, lb)


def _ffn_kernel(x_ref, xp_ref, xn_ref, sh_ref, sc_ref, g2_ref, wup_ref, cw_ref, cb_ref, wdn_ref, lg_ref, lb_ref,
                o_ref, h_scr, u_scr, act_scr, *, alpha):
    i = pl.program_id(1)
    last = pl.num_programs(1) - 1
    tm = x_ref.shape[1]
    halo = BF16_ROWS
    sc = sc_ref[0]
    sh = sh_ref[0]

    def mod(v):
        return (v * (1.0 + sc) + sh).astype(BF16)

    h_scr[0:halo, :] = jnp.where(i > 0, mod(xp_ref[0]), jnp.zeros((), BF16))
    h_scr[halo:halo + tm, :] = mod(x_ref[0])
    h_scr[halo + tm:, :] = jnp.where(i < last, mod(xn_ref[0]), jnp.zeros((), BF16))
    d_ff = wdn_ref.shape[0]
    cf = u_scr.shape[3]

    def conv(ub, cols):
        cw = cw_ref[:, cols]
        u = ub[...]
        rows = u.shape[0]
        prev = pltpu.roll(u, 1, 0)[halo:halo + tm]
        nxt = pltpu.roll(u, rows - 1, 0)[halo:halo + tm]
        return prev * cw[0:1, :] + u[halo:halo + tm] * cw[1:2, :] + nxt * cw[2:3, :] + cb_ref[:, cols]

    for j in range(d_ff // cf):
        val_cols = slice(j * cf, (j + 1) * cf)
        gate_cols = slice(d_ff + j * cf, d_ff + (j + 1) * cf)
        uv, ug = u_scr.at[j % 2, 0], u_scr.at[j % 2, 1]
        uv[...] = _dot(h_scr[...], wup_ref[:, val_cols])
        ug[...] = _dot(h_scr[...], wup_ref[:, gate_cols])
        act_scr[:, val_cols] = (conv(uv, val_cols) * _silu(conv(ug, gate_cols))).astype(BF16)
    ffn = _dot(act_scr[...], wdn_ref[...])
    o_ref[0] = _layer_norm(alpha * x_ref[0] + g2_ref[0] * ffn, lg_ref[...], lb_ref[...])


def _ffn(x1, mod3, mod_row, wup, cw, cb, wdn, lg, lb, *, alpha, tm, cf):
    b, t, d = x1.shape
    halo = BF16_ROWS
    per = tm // halo
    nblk = t // halo
    return pl.pallas_call(
        functools.partial(_ffn_kernel, alpha=alpha),
        out_shape=jax.ShapeDtypeStruct((b, t, d), F32),
        grid=(b, t // tm),
        in_specs=[
            pl.BlockSpec((1, tm, d), lambda bi, i: (bi, i, 0)),
            pl.BlockSpec((1, halo, d), lambda bi, i: (bi, jnp.maximum(i * per - 1, 0), 0)),
            pl.BlockSpec((1, halo, d), lambda bi, i: (bi, jnp.minimum((i + 1) * per, nblk - 1), 0)),
            pl.BlockSpec((1, 1, d), lambda bi, i: (mod_row(bi), 0, 3)),
            pl.BlockSpec((1, 1, d), lambda bi, i: (mod_row(bi), 0, 4)),
            pl.BlockSpec((1, 1, d), lambda bi, i: (mod_row(bi), 0, 5)),
            _resident(wup.shape), _full(cw.shape), _full(cb.shape), _resident(wdn.shape), _full(lg.shape),
            _full(lb.shape),
        ],
        out_specs=pl.BlockSpec((1, tm, d), lambda bi, i: (bi, i, 0)),
        scratch_shapes=[pltpu.VMEM((tm + 2 * halo, d), BF16), pltpu.VMEM((2, 2, tm + 2 * halo, cf), F32),
                        pltpu.VMEM((tm, wdn.shape[0]), BF16)],
        compiler_params=_cparams("parallel", "arbitrary"),
        name="ffn",
    )(x1, x1, x1, mod3, mod3, mod3, wup, cw, cb, wdn, lg, lb)


def _block_diag_mean(width, seg):
    i = np.arange(width)
    return jnp.asarray(((i[:, None] // seg) == (i[None, :] // seg)).astype(np.float32) / seg, dtype=BF16)


def _hg_tri(n):
    i = np.arange(n)
    same = (i[:, None] // HG_CHUNK) == (i[None, :] // HG_CHUNK)
    lower = same & (i[None, :] <= i[:, None])
    upper = same & (i[None, :] >= i[:, None])
    return jnp.asarray(np.stack([lower, upper]).astype(np.float32), dtype=BF16)


def _hg_select(w):
    r = np.arange(HG_BLOCK * w)
    c = np.arange(w)
    sg, head_r = r // w, (r % w) // HG_DIM
    head_c, s = c // HG_DIM, c % HG_DIM
    m = (head_r[:, None] == head_c[None, :]) & ((s[None, :] % HG_BLOCK) == sg[:, None])
    return jnp.asarray(m.astype(np.float32), dtype=BF16)


def _position_dft(t):
    lo = 64
    f = jnp.arange(t, dtype=jnp.int32)[:, None]
    step = 2.0 * np.pi / t
    a = ((f * (lo * jnp.arange(t // lo, dtype=jnp.int32))[None, :]) % t).astype(F32) * step
    bb = ((f * jnp.arange(lo, dtype=jnp.int32)[None, :]) % t).astype(F32) * step
    ca, sa = jnp.cos(a)[:, :, None], jnp.sin(a)[:, :, None]
    cb, sb = jnp.cos(bb)[:, None, :], jnp.sin(bb)[:, None, :]
    norm = 1.0 / np.sqrt(t)
    cos = ((ca * cb - sa * sb) * norm).reshape(t, t)
    msin = ((sa * cb + ca * sb) * -norm).reshape(t, t)
    return jnp.concatenate([cos, msin], axis=1).astype(BF16)


def _channel_dft(groups):
    c = np.arange(FT_DIM, dtype=np.int64)
    angc = 2.0 * np.pi * ((c[:, None] * c[None, :]) % FT_DIM).astype(np.float64) / FT_DIM
    eye = np.eye(groups)
    dftc = np.concatenate([np.kron(eye, np.cos(angc)), np.kron(eye, np.sin(angc))], axis=1) / np.sqrt(FT_DIM)
    return jnp.asarray(dftc, dtype=F32).astype(BF16)


def _rope_tables(t):
    rows = t // GRID_W
    row = jnp.repeat(jnp.arange(rows, dtype=jnp.int32), GRID_W)
    col = jnp.tile(jnp.arange(GRID_W, dtype=jnp.int32), rows)
    nf = HEAD_DIM // 4
    inv = ROPE_THETA ** (-jnp.arange(nf, dtype=F32) / nf)
    ar = row.astype(F32)[:, None] * inv
    ac = col.astype(F32)[:, None] * inv
    cos = jnp.concatenate([jnp.cos(ar)] * 2 + [jnp.cos(ac)] * 2, axis=1)
    sin = jnp.concatenate([-jnp.sin(ar), jnp.sin(ar), -jnp.sin(ac), jnp.sin(ac)], axis=1)
    return jnp.tile(cos, (1, 128 // HEAD_DIM)), jnp.tile(sin, (1, 128 // HEAD_DIM))


def _to_kv_heads(a):
    b, s, wdt = a.shape
    return a.reshape(b, s, wdt // HEAD_DIM, HEAD_DIM).transpose(0, 2, 1, 3)


def _with_ones(v):
    ones = jnp.ones(v.shape[:-1] + (1,), v.dtype)
    zeros = jnp.zeros(v.shape[:-1] + (HEAD_DIM - 1,), v.dtype)
    return jnp.concatenate([v, ones, zeros], axis=-1)


def kernel(x, c, ctx, c_ctx, w_ada, b_ada, w_in, q_norm, k_norm, hg_lb, hg_norm, ft_w, w_out, ln1_g, ln1_b, w_up,
           conv_w, conv_b, w_down, ln2_g, ln2_b):
    b, t, d = x.shape
    tc = ctx.shape[1]
    depth = w_ada.shape[0]
    d_ff = w_down.shape[1]
    hgw = hg_lb.shape[2]
    ftw = ft_w.shape[1] * ft_w.shape[2]
    kvw = ATTN_KV_HEADS * HEAD_DIM
    aw = w_in.shape[2] - 2 * kvw - 5 * hgw - ftw
    alpha = (2 * depth) ** 0.25

    sizes = [("q", aw), ("k", kvw), ("v", kvw), ("hq", hgw), ("hi", hgw), ("hf", 2 * hgw), ("hg", hgw), ("ft", ftw)]
    offs, lo = {}, 0
    for name, size in sizes:
        offs[name] = (lo, lo + size)
        lo += size

    mod_rows = -(-(b + 1) // 8) * 8
    cond = jnp.concatenate([c, c_ctx[None, :], jnp.zeros((mod_rows - b - 1, d), F32)], axis=0)
    mod = _ada(cond, w_ada, b_ada)
    mod3 = mod.reshape(depth * mod_rows, 1, 6 * d)

    lb_soft = jax.nn.softmax(hg_lb.astype(F32), axis=0)
    lower = (jnp.cumsum(lb_soft, axis=0) - lb_soft[:1]).reshape(depth, 2, 1, hgw)

    cos, sin = _rope_tables(t)
    bdq = _block_diag_mean(2 * 128, HEAD_DIM)
    bdk = _block_diag_mean(kvw, HEAD_DIM)
    bdh = _block_diag_mean(hgw, HG_DIM)
    tri = _hg_tri(min(HG_TILE, t))
    sel = _hg_select(hgw)
    dftc = _channel_dft(ftw // FT_DIM)
    cs_lat = _position_dft(t)
    cs_ctx = _position_dft(tc)

    cf = 256
    assert d_ff % cf == 0

    for l in range(depth):
        need_ctx = l < depth - 1
        lat_row = lambda bi, l=l: l * mod_rows + bi
        ctx_row = lambda bi, l=l: l * mod_rows + b

        w_in_l = w_in[l].astype(BF16)
        w_vt = w_in[l][:, offs["hi"][0]:offs["hi"][1]].T.astype(BF16)
        qg = jnp.tile(q_norm[l].astype(F32), aw // HEAD_DIM)[None, :]
        kg = jnp.tile(k_norm[l].astype(F32), kvw // HEAD_DIM)[None, :]
        proj_args = (w_in_l, w_vt, cos, sin, qg, kg, bdq, bdk, dftc)
        q, k, v, hq, hi, hg, hf, hit, fab = _inproj(x, mod3, lat_row, l, *proj_args, offs=offs, rope=True, tm=512)
        qc, kc, vc, hqc, hic, hgc, hfc, hitc, fabc = _inproj(ctx, mod3, ctx_row, l, *proj_args, offs=offs,
                                                               rope=False, tm=tc)

        kc_t, vc_h = _to_kv_heads(kc).swapaxes(2, 3), _with_ones(_to_kv_heads(vc))
        kt_all = jnp.concatenate([kc_t, _to_kv_heads(k).swapaxes(2, 3)], axis=3)
        v_all = jnp.concatenate([vc_h, _with_ones(_to_kv_heads(v))], axis=2)
        attn = _attention(q, kt_all, v_all, tq=256)

        gn = jnp.tile(hg_norm[l].astype(F32), hgw // HG_DIM)[None, :]
        rec, rec_c = _hgrn((hq, hi, hit, hf, hg), (hqc, hic, hitc, hfc, hgc), lower[l], gn, tri, sel, bdh,
                           need_ctx=need_ctx)

        wbd = jax.scipy.linalg.block_diag(*[ft_w[l, g] for g in range(ft_w.shape[1])]).astype(BF16)
        four = _dft(cs_lat, fab, wbd, tf=512)

        w_out_l = w_out[l].astype(BF16)
        lg1, lb1 = ln1_g[l][None, :], ln1_b[l][None, :]
        lg2, lb2 = ln2_g[l][None, :], ln2_b[l][None, :]
        ffn_args = (w_up[l].astype(BF16), conv_w[l], conv_b[l][None, :], w_down[l].astype(BF16), lg2, lb2)

        x1 = _outproj(attn, rec, four, x, mod3, lat_row, w_out_l, lg1, lb1, alpha=alpha, tm=512)
        x_next = _ffn(x1, mod3, lat_row, *ffn_args, alpha=alpha, tm=512, cf=cf)

        if need_ctx:
            attn_c = _attention(qc, kc_t, vc_h, tq=tc)
            four_c = _dft(cs_ctx, fabc, wbd, tf=tc)
            ctx1 = _outproj(attn_c, rec_c, four_c, ctx, mod3, ctx_row, w_out_l, lg1, lb1, alpha=alpha, tm=tc)
            ctx = _ffn(ctx1, mod3, ctx_row, *ffn_args, alpha=alpha, tm=tc, cf=cf)
        x = x_next

    return x
```

```python
import functools

import numpy as np
import jax
import jax.numpy as jnp
from jax import lax
from jax.experimental import pallas as pl
from jax.experimental.pallas import tpu as pltpu

F32 = jnp.float32
BF16 = jnp.bfloat16

GRID_W = 64
HEAD_DIM = 64
ATTN_KV_HEADS = 2
HG_DIM = 64
FT_DIM = 64
ROPE_THETA = 10000.0
F_MIN = 1e-30
EPS = 1e-6
CONV_W = 3

HG_CHUNK = 64
HG_BLOCK = 16
HG_NB = HG_CHUNK // HG_BLOCK
HG_PAIR = 2 * HG_DIM
HG_TILE = 512
NEG = -1e30
LOG2_E = 1.4426950408889634

V7X_VMEM_BYTES = 64 * 1024 * 1024
VMEM_LIMIT = int(V7X_VMEM_BYTES * 0.85)
BF16_ROWS = 16

_NT = (((1,), (1,)), ((), ()))


def _cparams(*sem):
    return pltpu.CompilerParams(dimension_semantics=sem, vmem_limit_bytes=VMEM_LIMIT)


def _dot(a, b):
    return jnp.dot(a, b, preferred_element_type=F32)


def _silu(x):
    return x * jax.nn.sigmoid(x)


def _layer_norm(y, g, b):
    mu = jnp.mean(y, axis=-1, keepdims=True)
    d = y - mu
    var = jnp.mean(d * d, axis=-1, keepdims=True)
    return d * lax.rsqrt(var + EPS) * g + b


def _full(shape):
    n = len(shape)
    return pl.BlockSpec(shape, lambda *_: (0,) * n)


def _resident(shape):
    n = len(shape)
    return pl.BlockSpec(shape, lambda *_: (0,) * n, pipeline_mode=pl.Buffered(1))


def _ada_kernel(c_ref, w_ref, b_ref, o_ref):
    s = _silu(c_ref[...]).astype(BF16)
    o_ref[0] = _dot(s, w_ref[0].astype(BF16)) + b_ref[0]


def _ada(cond, w_ada, b_ada):
    depth, d, n = w_ada.shape
    rows = cond.shape[0]
    tn = n // 4
    return pl.pallas_call(
        _ada_kernel,
        out_shape=jax.ShapeDtypeStruct((depth, rows, n), F32),
        grid=(depth, n // tn),
        in_specs=[
            pl.BlockSpec((rows, d), lambda l, j: (0, 0)),
            pl.BlockSpec((1, d, tn), lambda l, j: (l, 0, j)),
            pl.BlockSpec((1, 1, tn), lambda l, j: (l, 0, j)),
        ],
        out_specs=pl.BlockSpec((1, rows, tn), lambda l, j: (l, 0, j)),
        compiler_params=_cparams("arbitrary", "arbitrary"),
        name="ada",
    )(cond, w_ada, b_ada.reshape(depth, 1, n))


def _rope128(y, cos, sin_signed):
    lane = lax.broadcasted_iota(jnp.int32, y.shape, 1)
    upper = (lane % 32) >= 16
    partner = jnp.where(upper, pltpu.roll(y, 16, 1), pltpu.roll(y, 112, 1))
    return y * cos + partner * sin_signed


def _inproj_kernel(x_ref, sh_ref, sc_ref, w_ref, cos_ref, sin_ref, qg_ref, kg_ref, bdq_ref, bdk_ref,
                   dftc_ref, q_ref, k_ref, v_ref, hq_ref, hi_ref, hg_ref, hf_ref, hit_ref, fab_ref, h_scr, qk_scr,
                   *, rope, offs):
    h_scr[...] = (x_ref[0] * (1.0 + sc_ref[0]) + sh_ref[0]).astype(BF16)

    def proj(name):
        lo, hi = offs[name]
        return _dot(h_scr[...], w_ref[:, lo:hi])

    q_lo, q_hi = offs["q"]
    k_lo, k_hi = offs["k"]

    def norm_rope(lo, hi, bd_ref, g_ref, out_ref, scale):
        base = q_lo if lo < q_hi else k_lo
        y = qk_scr[:, lo - q_lo:hi - q_lo]
        y = y * lax.rsqrt(_dot((y * y).astype(BF16), bd_ref[...]) + EPS) * g_ref[:, lo - base:hi - base]
        if rope:
            cos, sin = cos_ref[...], sin_ref[...]
            y = jnp.concatenate([_rope128(y[:, c:c + 128], cos, sin) for c in range(0, hi - lo, 128)], axis=1)
        out_ref[0, :, lo - base:hi - base] = (y * scale).astype(BF16)

    qk_scr[...] = _dot(h_scr[...], w_ref[:, q_lo:k_hi])
    v_ref[0] = proj("v").astype(BF16)
    hq_ref[0] = proj("hq").astype(BF16)
    qw = bdq_ref.shape[0]
    q_scale = HEAD_DIM ** -0.5 * LOG2_E
    norm_rope(q_lo, q_lo + qw, bdq_ref, qg_ref, q_ref, q_scale)
    hi = proj("hi")
    hi_ref[0] = hi.astype(BF16)
    vt = hi.T.astype(BF16)
    for j in range(vt.shape[1] // HG_CHUNK):
        hit_ref[0, j] = vt[:, j * HG_CHUNK:(j + 1) * HG_CHUNK]
    for lo in range(q_lo + qw, q_hi, qw):
        norm_rope(lo, lo + qw, bdq_ref, qg_ref, q_ref, q_scale)
    hg_ref[0] = proj("hg").astype(BF16)
    norm_rope(k_lo, k_hi, bdk_ref, kg_ref, k_ref, 1.0)
    hf_ref[0] = proj("hf")
    ab = _dot(proj("ft").astype(BF16), dftc_ref[...])
    half = ab.shape[1] // 2
    fab_ref[0, 0] = ab[:, :half].astype(BF16)
    fab_ref[0, 1] = ab[:, half:].astype(BF16)


def _inproj(x, mod3, mod_row, l, w_in, cos, sin, qg, kg, bdq, bdk, dftc, *, offs, rope, tm):
    b, t, d = x.shape
    aw = offs["q"][1] - offs["q"][0]
    kw = offs["k"][1] - offs["k"][0]
    hw = offs["hq"][1] - offs["hq"][0]
    fw = offs["ft"][1] - offs["ft"][0]
    row = lambda bi: mod_row(bi)
    bt = lambda width: pl.BlockSpec((1, tm, width), lambda bi, i: (bi, i, 0))
    out_shapes = (
        jax.ShapeDtypeStruct((b, t, aw), BF16), jax.ShapeDtypeStruct((b, t, kw), BF16),
        jax.ShapeDtypeStruct((b, t, kw), BF16), jax.ShapeDtypeStruct((b, t, hw), BF16),
        jax.ShapeDtypeStruct((b, t, hw), BF16), jax.ShapeDtypeStruct((b, t, hw), BF16),
        jax.ShapeDtypeStruct((b, t, 2 * hw), F32),
        jax.ShapeDtypeStruct((b, t // HG_CHUNK, hw, HG_CHUNK), BF16),
        jax.ShapeDtypeStruct((b, 2, t, fw), BF16),
    )
    out_specs = (
        bt(aw), bt(kw), bt(kw), bt(hw), bt(hw), bt(hw), bt(2 * hw),
        pl.BlockSpec((1, tm // HG_CHUNK, hw, HG_CHUNK), lambda bi, i: (bi, i, 0, 0)),
        pl.BlockSpec((1, 2, tm, fw), lambda bi, i: (bi, 0, i, 0)),
    )
    return pl.pallas_call(
        functools.partial(_inproj_kernel, rope=rope, offs=offs),
        out_shape=out_shapes,
        grid=(b, t // tm),
        in_specs=[
            pl.BlockSpec((1, tm, d), lambda bi, i: (bi, i, 0)),
            pl.BlockSpec((1, 1, d), lambda bi, i: (row(bi), 0, 0)),
            pl.BlockSpec((1, 1, d), lambda bi, i: (row(bi), 0, 1)),
            _full(w_in.shape),
            pl.BlockSpec((tm, 128), lambda bi, i: (i, 0)),
            pl.BlockSpec((tm, 128), lambda bi, i: (i, 0)),
            _full(qg.shape), _full(kg.shape), _full(bdq.shape), _full(bdk.shape), _full(dftc.shape),
        ],
        out_specs=out_specs,
        scratch_shapes=[pltpu.VMEM((tm, d), BF16), pltpu.VMEM((tm, aw + kw), F32)],
        compiler_params=_cparams("parallel", "parallel"),
        name="inproj",
    )(x, mod3, mod3, w_in, cos, sin, qg, kg, bdq, bdk, dftc)


def _attn_kernel(q_ref, kt_ref, v_ref, o_ref, s_scr, m_scr, p_scr, *, group):
    kt = kt_ref[0, 0]
    v = v_ref[0, 0]
    half = (v.shape[0] // 2) // 256 * 256

    def scores(h):
        s = _dot(q_ref[0, :, h * HEAD_DIM:(h + 1) * HEAD_DIM], kt)
        s_scr[h] = s
        m_scr[h] = jnp.max(s, axis=-1, keepdims=True)

    def probs(h):
        p_scr[h] = jnp.exp2(s_scr[h] - m_scr[h]).astype(BF16)

    def values(h):
        if half:
            pv = _dot(p_scr[h, :, :half], v[:half]) + _dot(p_scr[h, :, half:], v[half:])
        else:
            pv = _dot(p_scr[h], v)
        return pv[:, :HEAD_DIM] / pv[:, HEAD_DIM:HEAD_DIM + 1]

    outs = [None] * group
    for step in range(group + 2):
        if step < group:
            scores(step)
        if 0 <= step - 1 < group:
            probs(step - 1)
        if 0 <= step - 2 < group:
            outs[step - 2] = values(step - 2)
    o_ref[0] = jnp.concatenate(outs, axis=1).astype(BF16)


def _attention(q, kt_all, v_all, *, tq):
    b, t, aw = q.shape
    kvh, s = v_all.shape[1], v_all.shape[2]
    group = aw // HEAD_DIM // kvh
    gw = group * HEAD_DIM
    return pl.pallas_call(
        functools.partial(_attn_kernel, group=group),
        out_shape=jax.ShapeDtypeStruct((b, t, aw), BF16),
        grid=(b, kvh, t // tq),
        in_specs=[
            pl.BlockSpec((1, tq, gw), lambda bi, g, i: (bi, i, g)),
            pl.BlockSpec((1, 1, HEAD_DIM, s), lambda bi, g, i: (bi, g, 0, 0)),
            pl.BlockSpec((1, 1, s, 2 * HEAD_DIM), lambda bi, g, i: (bi, g, 0, 0)),
        ],
        out_specs=pl.BlockSpec((1, tq, gw), lambda bi, g, i: (bi, i, g)),
        scratch_shapes=[pltpu.VMEM((group, tq, s), F32), pltpu.VMEM((group, tq, 1), F32),
                        pltpu.VMEM((group, tq, s), BF16)],
        compiler_params=_cparams("parallel", "parallel", "parallel"),
        name="attention",
    )(q, kt_all, v_all)


def _hg_prepare(rev, hq, z, v, hit_ref, c0, lb, tri_ref, sel_ref, o_ref, row0, qb_ref, qs_ref, ks_ref, si_ref, dl_ref,
                d_ref):
    r, w = z.shape
    n = r // HG_CHUNK
    f = lb + (1.0 - lb) * jax.nn.sigmoid(z)
    logf = jnp.log2(jnp.maximum(f, F_MIN))
    kk = 1.0 - f
    qs = _silu(hq.astype(F32))
    hi = logf.astype(BF16)
    lo = (logf - hi.astype(F32)).astype(BF16)
    tri = tri_ref[0:r, 0:r]
    cum = _dot(tri, hi) + _dot(tri, lo)

    c3 = cum.reshape(n, HG_CHUNK, w)
    q3 = qs.reshape(n, HG_CHUNK, w)
    k3 = kk.reshape(n, HG_CHUNK, w)

    def rows(a, lo_, hi_):
        parts = []
        if lo_ > 0:
            parts.append(jnp.zeros((n, lo_, w), F32))
        parts.append(a)
        if hi_ < HG_CHUNK:
            parts.append(jnp.zeros((n, HG_CHUNK - hi_, w), F32))
        return jnp.concatenate(parts, axis=1) if len(parts) > 1 else a

    e_last = c3[:, 0:1, :] if rev else c3[:, HG_CHUNK - 1:HG_CHUNK, :]
    qparts, kparts = [], []
    for g in range(1, HG_NB):
        j = HG_NB - g if rev else g - 1
        b0, b1 = j * HG_BLOCK, (j + 1) * HG_BLOCK
        e = c3[:, b0:b0 + 1, :] if rev else c3[:, b1 - 1:b1, :]
        p0, p1 = (0, b0) if rev else (b1, HG_CHUNK)
        qparts.append(rows(q3[:, p0:p1, :] * jnp.exp2(c3[:, p0:p1, :] - e), p0, p1))
        kparts.append(rows(k3[:, b0:b1, :] * jnp.exp2(e - c3[:, b0:b1, :]), b0, b1))
    qparts.append(q3 * jnp.exp2(c3))
    kparts.append(k3 * jnp.exp2(e_last - c3))
    dl = jnp.exp2(e_last).reshape(n, w)
    pairs = w // HG_PAIR
    ng = HG_NB - 1
    for pp in range(pairs):
        lanes = slice(pp * HG_PAIR, (pp + 1) * HG_PAIR)
        qb_ref[pp, 0:r, :] = jnp.concatenate([p[:, :, lanes] for p in qparts[:ng]], axis=-1).reshape(
            r, ng * HG_PAIR).astype(BF16)
        qs_ref[pp, 0:r, :] = qparts[ng][:, :, lanes].reshape(r, HG_PAIR).astype(BF16)
        ks_ref[pp, 0:r, :] = jnp.concatenate([p[:, :, lanes] for p in kparts], axis=-1).reshape(
            r, HG_NB * HG_PAIR).astype(BF16)
        dl_ref[pp, 0:n, :] = dl[:, lanes]

    nb = r // HG_BLOCK
    hb = HG_BLOCK // 2
    c5 = cum.reshape(nb, 2, hb, w)
    q5 = qs.reshape(nb, 2, hb, w)
    d5 = (cum - jnp.log2(jnp.maximum(kk, 0.0))).reshape(nb, 2, hb, w)
    tau = lax.broadcasted_iota(jnp.int32, (nb, hb, w), 1)
    halves = []
    for half in range(2):
        sources = [sg for sg in range(HG_BLOCK) if not ((half > sg // hb) if rev else (half < sg // hb))]
        eparts = []
        for sg in sources:
            sh, sl = sg // hb, sg % hb
            arg = c5[:, half] - d5[:, sh, sl:sl + 1, :]
            if half == sh:
                arg = jnp.where((tau <= sl) if rev else (tau >= sl), arg, NEG)
            eparts.append((jnp.exp2(arg) * q5[:, half]).reshape(nb * hb, w).astype(BF16))
        lo_ = sources[0] * w
        halves.append(_dot(jnp.concatenate(eparts, axis=-1), sel_ref[lo_:lo_ + len(sources) * w, :]))
    a = jnp.stack([h_.reshape(nb, hb, w) for h_ in halves], axis=1).reshape(r, w)
    ti = lax.broadcasted_iota(jnp.int32, (r, w), 0)
    ci = lax.broadcasted_iota(jnp.int32, (r, w), 1)
    same_block = ((ti % HG_CHUNK) // HG_BLOCK) == ((ci % HG_CHUNK) // HG_BLOCK)
    ad = jnp.where(same_block, a, 0.0).astype(BF16)

    def same_head(shape):
        return (lax.broadcasted_iota(jnp.int32, shape, 0) // HG_DIM) == (lax.broadcasted_iota(jnp.int32, shape, 1) // HG_DIM)

    pair_mask = same_head((HG_PAIR, HG_PAIR))
    head_mask = same_head((w, w))
    for c in range(n):
        rws = slice(c * HG_CHUNK, (c + 1) * HG_CHUNK)
        vt = hit_ref[0, c0 + c]
        for pp in range(pairs):
            x = _dot(vt[pp * HG_PAIR:(pp + 1) * HG_PAIR, :], ks_ref[pp, rws, :])
            d_ref[c, pp] = jnp.concatenate(
                [jnp.where(pair_mask, x[:, g * HG_PAIR:(g + 1) * HG_PAIR], 0.0).astype(BF16) for g in range(ng)],
                axis=1)
            si_ref[c, pp] = jnp.where(pair_mask, x[:, ng * HG_PAIR:], 0.0)
    for c in range(n):
        rws = slice(c * HG_CHUNK, (c + 1) * HG_CHUNK)
        outs = [lax.dot_general(qb_ref[pp, rws, :], d_ref[c, pp], _NT, preferred_element_type=F32)
                for pp in range(pairs)]
        vc = v[rws]
        vbd = jnp.where(head_mask, jnp.concatenate([vc] * (w // HG_CHUNK), axis=0), jnp.zeros((), vc.dtype))
        o_ref[row0 + c * HG_CHUNK:row0 + (c + 1) * HG_CHUNK, :] = jnp.concatenate(outs, axis=1) + _dot(ad[rws], vbd)


def _hg_step(qs_ref, si_ref, dl_ref, st_ref, o_ref, r0, g0, ci):
    outs = []
    for pp in range(st_ref.shape[0]):
        st = st_ref[pp]
        outs.append(lax.dot_general(qs_ref[pp, pl.ds(r0, HG_CHUNK), :], st.astype(BF16), _NT,
                                    preferred_element_type=F32))
        st_ref[pp] = st * dl_ref[pp, pl.ds(ci, 1), :] + si_ref[ci, pp]
    o_ref[pl.ds(g0, HG_CHUNK), :] += jnp.concatenate(outs, axis=1)


def _hgrn_kernel(hq_ref, hi_ref, hit_ref, hf_ref, hg_ref, hqc_ref, hic_ref, hitc_ref, hfc_ref, hgc_ref,
                 lb_ref, gn_ref, tri_ref, sel_ref, bd_ref, *rest, need_ctx):
    if need_ctx:
        rec_ref, recc_ref = rest[0], rest[1]
        scr = rest[2:]
    else:
        rec_ref, recc_ref = rest[0], None
        scr = rest[1:]
    of_ref, ob_ref, ofc_ref, obc_ref, stf_ref, stb_ref = scr[:6]
    fwd, bwd = scr[6:12], scr[12:18]
    w = hq_ref.shape[2]
    lb_f = lb_ref[0]
    lb_b = lb_ref[1]
    stf_ref[...] = jnp.zeros_like(stf_ref)
    stb_ref[...] = jnp.zeros_like(stb_ref)

    def run(hq, hi, hit, hf, o_f, o_b):
        t = hq.shape[1]
        tile = min(HG_TILE, t)
        nt = t // tile
        cpt = tile // HG_CHUNK
        for s in range(nt):
            rf = s * tile
            rb = (nt - 1 - s) * tile
            _hg_prepare(False, hq[0, rf:rf + tile, :], hf[0, rf:rf + tile, 0:w], hi[0, rf:rf + tile, :], hit,
                        rf // HG_CHUNK, lb_f, tri_ref.at[0], sel_ref, o_f, rf, *fwd)
            _hg_prepare(True, hq[0, rb:rb + tile, :], hf[0, rb:rb + tile, w:2 * w], hi[0, rb:rb + tile, :], hit,
                        rb // HG_CHUNK, lb_b, tri_ref.at[1], sel_ref, o_b, rb, *bwd)

            def body(i, carry):
                cf = i
                cb = cpt - 1 - i
                r0f = pl.multiple_of(cf * HG_CHUNK, HG_CHUNK)
                r0b = pl.multiple_of(cb * HG_CHUNK, HG_CHUNK)
                gf = pl.multiple_of(rf + cf * HG_CHUNK, HG_CHUNK)
                gb = pl.multiple_of(rb + cb * HG_CHUNK, HG_CHUNK)
                _hg_step(fwd[1], fwd[3], fwd[4], stf_ref, o_f, r0f, gf, cf)
                _hg_step(bwd[1], bwd[3], bwd[4], stb_ref, o_b, r0b, gb, cb)
                return carry

            lax.fori_loop(0, cpt, body, 0)

    def finish(o_f, o_b, hg, out):
        t = hg.shape[1]
        tile = min(HG_TILE, t)
        for s in range(t // tile):
            rows = slice(s * tile, (s + 1) * tile)
            o = o_f[rows, :] + o_b[rows, :]
            ms = _dot((o * o).astype(BF16), bd_ref[...])
            on = o * lax.rsqrt(ms + EPS) * gn_ref[...]
            out[0, rows, :] = (on * _silu(hg[0, rows, :].astype(F32))).astype(BF16)

    run(hqc_ref, hic_ref, hitc_ref, hfc_ref, ofc_ref, obc_ref)
    run(hq_ref, hi_ref, hit_ref, hf_ref, of_ref, ob_ref)
    finish(of_ref, ob_ref, hg_ref, rec_ref)
    if need_ctx:
        finish(ofc_ref, obc_ref, hgc_ref, recc_ref)


def _hgrn(lat, cx, lb, gn, tri, sel, bd, *, need_ctx):
    hq, hi, hit, hf, hg = lat
    hqc, hic, hitc, hfc, hgc = cx
    b, t, w = hq.shape
    tc = hqc.shape[1]
    tile = min(HG_TILE, t)

    def seq(a):
        nd = a.ndim
        return pl.BlockSpec((1,) + a.shape[1:], lambda bi: (bi,) + (0,) * (nd - 1))

    out_shape = [jax.ShapeDtypeStruct((b, t, w), BF16)]
    out_specs = [pl.BlockSpec((1, t, w), lambda bi: (bi, 0, 0))]
    if need_ctx:
        out_shape.append(jax.ShapeDtypeStruct((b, tc, w), BF16))
        out_specs.append(pl.BlockSpec((1, tc, w), lambda bi: (bi, 0, 0)))
    pairs = w // HG_PAIR
    cpt = tile // HG_CHUNK
    dir_scratch = [
        pltpu.VMEM((pairs, tile, (HG_NB - 1) * HG_PAIR), BF16),
        pltpu.VMEM((pairs, tile, HG_PAIR), BF16),
        pltpu.VMEM((pairs, tile, HG_NB * HG_PAIR), BF16),
        pltpu.VMEM((cpt, pairs, HG_PAIR, HG_PAIR), F32),
        pltpu.VMEM((pairs, cpt, HG_PAIR), F32),
        pltpu.VMEM((cpt, pairs, HG_PAIR, (HG_NB - 1) * HG_PAIR), BF16),
    ]
    res = pl.pallas_call(
        functools.partial(_hgrn_kernel, need_ctx=need_ctx),
        out_shape=tuple(out_shape),
        grid=(b,),
        in_specs=[seq(hq), seq(hi), seq(hit), seq(hf), seq(hg), seq(hqc), seq(hic), seq(hitc), seq(hfc), seq(hgc),
                  _full(lb.shape), _full(gn.shape), _full(tri.shape), _full(sel.shape), _full(bd.shape)],
        out_specs=tuple(out_specs),
        scratch_shapes=[pltpu.VMEM((t, w), F32), pltpu.VMEM((t, w), F32),
                        pltpu.VMEM((tc, w), F32), pltpu.VMEM((tc, w), F32),
                        pltpu.VMEM((pairs, HG_PAIR, HG_PAIR), F32), pltpu.VMEM((pairs, HG_PAIR, HG_PAIR), F32)]
        + dir_scratch + dir_scratch,
        compiler_params=_cparams("parallel"),
        name="hgrn",
    )(hq, hi, hit, hf, hg, hqc, hic, hitc, hfc, hgc, lb, gn, tri, sel, bd)
    return res if need_ctx else (res[0], None)


def _dft_kernel(cs_ref, ab_ref, w_ref, o_ref):
    t = ab_ref.shape[2]
    z = _dot(cs_ref[:, 0:t], ab_ref[0, 0]) + _dot(cs_ref[:, t:2 * t], ab_ref[0, 1])
    o_ref[0] = _dot(z.astype(BF16), w_ref[...]).astype(BF16)


def _dft(cs, fab, wbd, *, tf):
    b, _, t, w = fab.shape
    return pl.pallas_call(
        _dft_kernel,
        out_shape=jax.ShapeDtypeStruct((b, t, w), BF16),
        grid=(t // tf, b),
        in_specs=[
            pl.BlockSpec((tf, 2 * t), lambda i, bi: (i, 0)),
            pl.BlockSpec((1, 2, t, w), lambda i, bi: (bi, 0, 0, 0)),
            _full(wbd.shape),
        ],
        out_specs=pl.BlockSpec((1, tf, w), lambda i, bi: (bi, i, 0)),
        compiler_params=_cparams("parallel", "parallel"),
        name="dft",
    )(cs, fab, wbd)


def _outproj_kernel(a_ref, r_ref, f_ref, x_ref, g1_ref, w_ref, lg_ref, lb_ref, o_ref, mix_scr, *, alpha):
    aw = a_ref.shape[2]
    rw = r_ref.shape[2]
    tm = x_ref.shape[1]
    nsub = 2 if tm % (2 * BF16_ROWS) == 0 else 1
    sub = tm // nsub

    def mix(i):
        rows = slice(i * sub, (i + 1) * sub)
        mix_scr[rows, :] = (_dot(a_ref[0, rows, :], w_ref[0:aw, :]) + _dot(r_ref[0, rows, :], w_ref[aw:aw + rw, :])
                            + _dot(f_ref[0, rows, :], w_ref[aw + rw:, :]))

    def norm(i):
        rows = slice(i * sub, (i + 1) * sub)
        o_ref[0, rows, :] = _layer_norm(alpha * x_ref[0, rows, :] + g1_ref[0] * mix_scr[rows, :],
                                        lg_ref[...], lb_ref[...])

    for step in range(nsub + 1):
        if step < nsub:
            mix(step)
        if step >= 1:
            norm(step - 1)


def _outproj(attn, rec, four, x, mod3, mod_row, w_out, lg, lb, *, alpha, tm):
    b, t, d = x.shape
    bt = lambda a: pl.BlockSpec((1, tm, a.shape[2]), lambda bi, i: (bi, i, 0))
    return pl.pallas_call(
        functools.partial(_outproj_kernel, alpha=alpha),
        out_shape=jax.ShapeDtypeStruct((b, t, d), F32),
        grid=(b, t // tm),
        in_specs=[bt(attn), bt(rec), bt(four), bt(x),
                  pl.BlockSpec((1, 1, d), lambda bi, i: (mod_row(bi), 0, 2)),
                  _full(w_out.shape), _full(lg.shape), _full(lb.shape)],
        out_specs=pl.BlockSpec((1, tm, d), lambda bi, i: (bi, i, 0)),
        scratch_shapes=[pltpu.VMEM((tm, d), F32)],
        compiler_params=_cparams("parallel", "parallel"),
        name="outproj",
    )(attn, rec, four, x, mod3, w_out, lg, lb)


def _ffn_kernel(x_ref, xp_ref, xn_ref, sh_ref, sc_ref, g2_ref, wup_ref, cw_ref, cb_ref, wdn_ref, lg_ref, lb_ref,
                o_ref, h_scr, u_scr, act_scr, y_scr, *, alpha):
    i = pl.program_id(1)
    last = pl.num_programs(1) - 1
    tm = x_ref.shape[1]
    halo = BF16_ROWS
    sc = sc_ref[0]
    sh = sh_ref[0]

    def mod(v):
        return (v * (1.0 + sc) + sh).astype(BF16)

    h_scr[0:halo, :] = jnp.where(i > 0, mod(xp_ref[0]), jnp.zeros((), BF16))
    h_scr[halo:halo + tm, :] = mod(x_ref[0])
    h_scr[halo + tm:, :] = jnp.where(i < last, mod(xn_ref[0]), jnp.zeros((), BF16))
    d_ff = wdn_ref.shape[0]
    cf = u_scr.shape[3]

    def conv(ub, cols):
        cw = cw_ref[:, cols]
        u = ub[...]
        rows = u.shape[0]
        prev = pltpu.roll(u, 1, 0)[halo:halo + tm]
        nxt = pltpu.roll(u, rows - 1, 0)[halo:halo + tm]
        return prev * cw[0:1, :] + u[halo:halo + tm] * cw[1:2, :] + nxt * cw[2:3, :] + cb_ref[:, cols]

    for j in range(d_ff // cf):
        val_cols = slice(j * cf, (j + 1) * cf)
        gate_cols = slice(d_ff + j * cf, d_ff + (j + 1) * cf)
        uv, ug = u_scr.at[j % 2, 0], u_scr.at[j % 2, 1]
        uv[...] = _dot(h_scr[...], wup_ref[:, val_cols])
        ug[...] = _dot(h_scr[...], wup_ref[:, gate_cols])
        act_scr[:, val_cols] = (conv(uv, val_cols) * _silu(conv(ug, gate_cols))).astype(BF16)
    nsub = 2 if tm % (2 * BF16_ROWS) == 0 else 1
    sub = tm // nsub
    for step in range(nsub + 1):
        if step < nsub:
            rows = slice(step * sub, (step + 1) * sub)
            y_scr[rows, :] = _dot(act_scr[rows, :], wdn_ref[...])
        if step >= 1:
            rows = slice((step - 1) * sub, step * sub)
            o_ref[0, rows, :] = _layer_norm(alpha * x_ref[0, rows, :] + g2_ref[0] * y_scr[rows, :],
                                            lg_ref[...], lb_ref[...])


def _ffn(x1, mod3, mod_row, wup, cw, cb, wdn, lg, lb, *, alpha, tm, cf):
    b, t, d = x1.shape
    halo = BF16_ROWS
    per = tm // halo
    nblk = t // halo
    return pl.pallas_call(
        functools.partial(_ffn_kernel, alpha=alpha),
        out_shape=jax.ShapeDtypeStruct((b, t, d), F32),
        grid=(b, t // tm),
        in_specs=[
            pl.BlockSpec((1, tm, d), lambda bi, i: (bi, i, 0)),
            pl.BlockSpec((1, halo, d), lambda bi, i: (bi, jnp.maximum(i * per - 1, 0), 0)),
            pl.BlockSpec((1, halo, d), lambda bi, i: (bi, jnp.minimum((i + 1) * per, nblk - 1), 0)),
            pl.BlockSpec((1, 1, d), lambda bi, i: (mod_row(bi), 0, 3)),
            pl.BlockSpec((1, 1, d), lambda bi, i: (mod_row(bi), 0, 4)),
            pl.BlockSpec((1, 1, d), lambda bi, i: (mod_row(bi), 0, 5)),
            _resident(wup.shape), _full(cw.shape), _full(cb.shape), _resident(wdn.shape), _full(lg.shape),
            _full(lb.shape),
        ],
        out_specs=pl.BlockSpec((1, tm, d), lambda bi, i: (bi, i, 0)),
        scratch_shapes=[pltpu.VMEM((tm + 2 * halo, d), BF16), pltpu.VMEM((2, 2, tm + 2 * halo, cf), F32),
                        pltpu.VMEM((tm, wdn.shape[0]), BF16), pltpu.VMEM((tm, d), F32)],
        compiler_params=_cparams("parallel", "arbitrary"),
        name="ffn",
    )(x1, x1, x1, mod3, mod3, mod3, wup, cw, cb, wdn, lg, lb)


def _block_diag_mean(width, seg):
    i = np.arange(width)
    return jnp.asarray(((i[:, None] // seg) == (i[None, :] // seg)).astype(np.float32) / seg, dtype=BF16)


def _hg_tri(n):
    i = np.arange(n)
    same = (i[:, None] // HG_CHUNK) == (i[None, :] // HG_CHUNK)
    lower = same & (i[None, :] <= i[:, None])
    upper = same & (i[None, :] >= i[:, None])
    return jnp.asarray(np.stack([lower, upper]).astype(np.float32), dtype=BF16)


def _hg_select(w):
    r = np.arange(HG_BLOCK * w)
    c = np.arange(w)
    sg, head_r = r // w, (r % w) // HG_DIM
    head_c, s = c // HG_DIM, c % HG_DIM
    m = (head_r[:, None] == head_c[None, :]) & ((s[None, :] % HG_BLOCK) == sg[:, None])
    return jnp.asarray(m.astype(np.float32), dtype=BF16)


def _position_dft(t):
    lo = 64
    f = jnp.arange(t, dtype=jnp.int32)[:, None]
    step = 2.0 * np.pi / t
    a = ((f * (lo * jnp.arange(t // lo, dtype=jnp.int32))[None, :]) % t).astype(F32) * step
    bb = ((f * jnp.arange(lo, dtype=jnp.int32)[None, :]) % t).astype(F32) * step
    ca, sa = jnp.cos(a)[:, :, None], jnp.sin(a)[:, :, None]
    cb, sb = jnp.cos(bb)[:, None, :], jnp.sin(bb)[:, None, :]
    norm = 1.0 / np.sqrt(t)
    cos = ((ca * cb - sa * sb) * norm).reshape(t, t)
    msin = ((sa * cb + ca * sb) * -norm).reshape(t, t)
    return jnp.concatenate([cos, msin], axis=1).astype(BF16)


def _channel_dft(groups):
    c = np.arange(FT_DIM, dtype=np.int64)
    angc = 2.0 * np.pi * ((c[:, None] * c[None, :]) % FT_DIM).astype(np.float64) / FT_DIM
    eye = np.eye(groups)
    dftc = np.concatenate([np.kron(eye, np.cos(angc)), np.kron(eye, np.sin(angc))], axis=1) / np.sqrt(FT_DIM)
    return jnp.asarray(dftc, dtype=F32).astype(BF16)


def _rope_tables(t):
    rows = t // GRID_W
    row = jnp.repeat(jnp.arange(rows, dtype=jnp.int32), GRID_W)
    col = jnp.tile(jnp.arange(GRID_W, dtype=jnp.int32), rows)
    nf = HEAD_DIM // 4
    inv = ROPE_THETA ** (-jnp.arange(nf, dtype=F32) / nf)
    ar = row.astype(F32)[:, None] * inv
    ac = col.astype(F32)[:, None] * inv
    cos = jnp.concatenate([jnp.cos(ar)] * 2 + [jnp.cos(ac)] * 2, axis=1)
    sin = jnp.concatenate([-jnp.sin(ar), jnp.sin(ar), -jnp.sin(ac), jnp.sin(ac)], axis=1)
    return jnp.tile(cos, (1, 128 // HEAD_DIM)), jnp.tile(sin, (1, 128 // HEAD_DIM))


def _to_kv_heads(a):
    b, s, wdt = a.shape
    return a.reshape(b, s, wdt // HEAD_DIM, HEAD_DIM).transpose(0, 2, 1, 3)


def _with_ones(v):
    ones = jnp.ones(v.shape[:-1] + (1,), v.dtype)
    zeros = jnp.zeros(v.shape[:-1] + (HEAD_DIM - 1,), v.dtype)
    return jnp.concatenate([v, ones, zeros], axis=-1)


def kernel(x, c, ctx, c_ctx, w_ada, b_ada, w_in, q_norm, k_norm, hg_lb, hg_norm, ft_w, w_out, ln1_g, ln1_b, w_up,
           conv_w, conv_b, w_down, ln2_g, ln2_b):
    b, t, d = x.shape
    tc = ctx.shape[1]
    depth = w_ada.shape[0]
    d_ff = w_down.shape[1]
    hgw = hg_lb.shape[2]
    ftw = ft_w.shape[1] * ft_w.shape[2]
    kvw = ATTN_KV_HEADS * HEAD_DIM
    aw = w_in.shape[2] - 2 * kvw - 5 * hgw - ftw
    alpha = (2 * depth) ** 0.25

    sizes = [("q", aw), ("k", kvw), ("v", kvw), ("hq", hgw), ("hi", hgw), ("hf", 2 * hgw), ("hg", hgw), ("ft", ftw)]
    offs, lo = {}, 0
    for name, size in sizes:
        offs[name] = (lo, lo + size)
        lo += size

    mod_rows = -(-(b + 1) // 8) * 8
    cond = jnp.concatenate([c, c_ctx[None, :], jnp.zeros((mod_rows - b - 1, d), F32)], axis=0)
    mod = _ada(cond, w_ada, b_ada)
    mod3 = mod.reshape(depth * mod_rows, 1, 6 * d)

    lb_soft = jax.nn.softmax(hg_lb.astype(F32), axis=0)
    lower = (jnp.cumsum(lb_soft, axis=0) - lb_soft[:1]).reshape(depth, 2, 1, hgw)

    cos, sin = _rope_tables(t)
    bdq = _block_diag_mean(2 * 128, HEAD_DIM)
    bdk = _block_diag_mean(kvw, HEAD_DIM)
    bdh = _block_diag_mean(hgw, HG_DIM)
    tri = _hg_tri(min(HG_TILE, t))
    sel = _hg_select(hgw)
    dftc = _channel_dft(ftw // FT_DIM)
    cs_lat = _position_dft(t)
    cs_ctx = _position_dft(tc)

    cf = 256
    assert d_ff % cf == 0

    for l in range(depth):
        need_ctx = l < depth - 1
        lat_row = lambda bi, l=l: l * mod_rows + bi
        ctx_row = lambda bi, l=l: l * mod_rows + b

        w_in_l = w_in[l].astype(BF16)
        qg = jnp.tile(q_norm[l].astype(F32), aw // HEAD_DIM)[None, :]
        kg = jnp.tile(k_norm[l].astype(F32), kvw // HEAD_DIM)[None, :]
        proj_args = (w_in_l, cos, sin, qg, kg, bdq, bdk, dftc)
        q, k, v, hq, hi, hg, hf, hit, fab = _inproj(x, mod3, lat_row, l, *proj_args, offs=offs, rope=True, tm=512)
        qc, kc, vc, hqc, hic, hgc, hfc, hitc, fabc = _inproj(ctx, mod3, ctx_row, l, *proj_args, offs=offs,
                                                               rope=False, tm=tc)

        kc_t, vc_h = _to_kv_heads(kc).swapaxes(2, 3), _with_ones(_to_kv_heads(vc))
        kt_all = jnp.concatenate([kc_t, _to_kv_heads(k).swapaxes(2, 3)], axis=3)
        v_all = jnp.concatenate([vc_h, _with_ones(_to_kv_heads(v))], axis=2)
        attn = _attention(q, kt_all, v_all, tq=256)

        gn = jnp.tile(hg_norm[l].astype(F32), hgw // HG_DIM)[None, :]
        rec, rec_c = _hgrn((hq, hi, hit, hf, hg), (hqc, hic, hitc, hfc, hgc), lower[l], gn, tri, sel, bdh,
                           need_ctx=need_ctx)

        wbd = jax.scipy.linalg.block_diag(*[ft_w[l, g] for g in range(ft_w.shape[1])]).astype(BF16)
        four = _dft(cs_lat, fab, wbd, tf=512)

        w_out_l = w_out[l].astype(BF16)
        lg1, lb1 = ln1_g[l][None, :], ln1_b[l][None, :]
        lg2, lb2 = ln2_g[l][None, :], ln2_b[l][None, :]
        ffn_args = (w_up[l].astype(BF16), conv_w[l], conv_b[l][None, :], w_down[l].astype(BF16), lg2, lb2)

        x1 = _outproj(attn, rec, four, x, mod3, lat_row, w_out_l, lg1, lb1, alpha=alpha, tm=512)
        x_next = _ffn(x1, mod3, lat_row, *ffn_args, alpha=alpha, tm=512, cf=cf)

        if need_ctx:
            attn_c = _attention(qc, kc_t, vc_h, tq=tc)
            four_c = _dft(cs_ctx, fabc, wbd, tf=tc)
            ctx1 = _outproj(attn_c, rec_c, four_c, ctx, mod3, ctx_row, w_out_l, lg1, lb1, alpha=alpha, tm=tc)
            ctx = _ffn(ctx1, mod3, ctx_row, *ffn_args, alpha=alpha, tm=tc, cf=cf)
        x = x_next

    return x
```

```python
import functools

import numpy as np
import jax
import jax.numpy as jnp
from jax import lax
from jax.experimental import pallas as pl
from jax.experimental.pallas import tpu as pltpu

F32 = jnp.float32
BF16 = jnp.bfloat16

GRID_W = 64
HEAD_DIM = 64
ATTN_KV_HEADS = 2
HG_DIM = 64
FT_DIM = 64
ROPE_THETA = 10000.0
F_MIN = 1e-30
EPS = 1e-6
CONV_W = 3

HG_CHUNK = 64
HG_BLOCK = 16
HG_NB = HG_CHUNK // HG_BLOCK
HG_PAIR = 2 * HG_DIM
HG_TILE = 512
NEG = -1e30
LOG2_E = 1.4426950408889634

V7X_VMEM_BYTES = 64 * 1024 * 1024
VMEM_LIMIT = int(V7X_VMEM_BYTES * 0.85)
BF16_ROWS = 16

_NT = (((1,), (1,)), ((), ()))


def _cparams(*sem):
    return pltpu.CompilerParams(dimension_semantics=sem, vmem_limit_bytes=VMEM_LIMIT)


def _dot(a, b):
    return jnp.dot(a, b, preferred_element_type=F32)


def _silu(x):
    return x * jax.nn.sigmoid(x)


def _layer_norm(y, g, b):
    mu = jnp.mean(y, axis=-1, keepdims=True)
    d = y - mu
    var = jnp.mean(d * d, axis=-1, keepdims=True)
    return d * lax.rsqrt(var + EPS) * g + b


def _full(shape):
    n = len(shape)
    return pl.BlockSpec(shape, lambda *_: (0,) * n)


def _resident(shape):
    n = len(shape)
    return pl.BlockSpec(shape, lambda *_: (0,) * n, pipeline_mode=pl.Buffered(1))


def _ada_kernel(c_ref, w_ref, b_ref, o_ref):
    s = _silu(c_ref[...]).astype(BF16)
    o_ref[0] = _dot(s, w_ref[0].astype(BF16)) + b_ref[0]


def _ada(cond, w_ada, b_ada):
    depth, d, n = w_ada.shape
    rows = cond.shape[0]
    tn = n // 4
    return pl.pallas_call(
        _ada_kernel,
        out_shape=jax.ShapeDtypeStruct((depth, rows, n), F32),
        grid=(depth, n // tn),
        in_specs=[
            pl.BlockSpec((rows, d), lambda l, j: (0, 0)),
            pl.BlockSpec((1, d, tn), lambda l, j: (l, 0, j)),
            pl.BlockSpec((1, 1, tn), lambda l, j: (l, 0, j)),
        ],
        out_specs=pl.BlockSpec((1, rows, tn), lambda l, j: (l, 0, j)),
        compiler_params=_cparams("arbitrary", "arbitrary"),
        name="ada",
    )(cond, w_ada, b_ada.reshape(depth, 1, n))


def _rope128(y, cos, sin_signed):
    lane = lax.broadcasted_iota(jnp.int32, y.shape, 1)
    upper = (lane % 32) >= 16
    partner = jnp.where(upper, pltpu.roll(y, 16, 1), pltpu.roll(y, 112, 1))
    return y * cos + partner * sin_signed


def _inproj_kernel(x_ref, sh_ref, sc_ref, w_ref, cos_ref, sin_ref, qg_ref, kg_ref, bdq_ref, bdk_ref,
                   dftc_ref, q_ref, k_ref, v_ref, hq_ref, hi_ref, hg_ref, hf_ref, hit_ref, fab_ref, h_scr, qk_scr,
                   *, rope, offs):
    h_scr[...] = (x_ref[0] * (1.0 + sc_ref[0]) + sh_ref[0]).astype(BF16)

    def proj(name):
        lo, hi = offs[name]
        return _dot(h_scr[...], w_ref[:, lo:hi])

    q_lo, q_hi = offs["q"]
    k_lo, k_hi = offs["k"]

    def norm_rope(lo, hi, bd_ref, g_ref):
        base = q_lo if lo < q_hi else k_lo
        y = qk_scr[:, lo - q_lo:hi - q_lo]
        y = y * lax.rsqrt(_dot((y * y).astype(BF16), bd_ref[...]) + EPS) * g_ref[:, lo - base:hi - base]
        if rope:
            cos, sin = cos_ref[...], sin_ref[...]
            y = jnp.concatenate([_rope128(y[:, c:c + 128], cos, sin) for c in range(0, hi - lo, 128)], axis=1)
        return y

    q_scale = HEAD_DIM ** -0.5 * LOG2_E

    def q_part(lo, hi):
        q_ref[0, :, lo - q_lo:hi - q_lo] = (norm_rope(lo, hi, bdq_ref, qg_ref) * q_scale).astype(BF16)

    qk_scr[...] = _dot(h_scr[...], w_ref[:, q_lo:k_hi])
    v = proj("v")
    lane = lax.broadcasted_iota(jnp.int32, v.shape, 1)
    for hh in range(v.shape[1] // HEAD_DIM):
        vh = v if hh == 0 else pltpu.roll(v, v.shape[1] - hh * HEAD_DIM, 1)
        v_ref[0, hh] = jnp.where(lane < HEAD_DIM, vh, jnp.where(lane == HEAD_DIM, 1.0, 0.0)).astype(BF16)
    hq_ref[0] = proj("hq").astype(BF16)
    qw = bdq_ref.shape[0]
    q_part(q_lo, q_lo + qw)
    hi = proj("hi")
    hi_ref[0] = hi.astype(BF16)
    vt = hi.T.astype(BF16)
    for j in range(vt.shape[1] // HG_CHUNK):
        hit_ref[0, j] = vt[:, j * HG_CHUNK:(j + 1) * HG_CHUNK]
    for lo in range(q_lo + qw, q_hi, qw):
        q_part(lo, lo + qw)
    hg_ref[0] = proj("hg").astype(BF16)
    kt = norm_rope(k_lo, k_hi, bdk_ref, kg_ref).T.astype(BF16)
    for hh in range(kt.shape[0] // HEAD_DIM):
        k_ref[0, hh] = kt[hh * HEAD_DIM:(hh + 1) * HEAD_DIM, :]
    hf_ref[0] = proj("hf")
    ab = _dot(proj("ft").astype(BF16), dftc_ref[...])
    half = ab.shape[1] // 2
    fab_ref[0, 0] = ab[:, :half].astype(BF16)
    fab_ref[0, 1] = ab[:, half:].astype(BF16)


def _inproj(x, mod3, mod_row, l, w_in, cos, sin, qg, kg, bdq, bdk, dftc, *, offs, rope, tm):
    b, t, d = x.shape
    aw = offs["q"][1] - offs["q"][0]
    kw = offs["k"][1] - offs["k"][0]
    hw = offs["hq"][1] - offs["hq"][0]
    fw = offs["ft"][1] - offs["ft"][0]
    row = lambda bi: mod_row(bi)
    bt = lambda width: pl.BlockSpec((1, tm, width), lambda bi, i: (bi, i, 0))
    kvh = kw // HEAD_DIM
    out_shapes = (
        jax.ShapeDtypeStruct((b, t, aw), BF16), jax.ShapeDtypeStruct((b, kvh, HEAD_DIM, t), BF16),
        jax.ShapeDtypeStruct((b, kvh, t, 2 * HEAD_DIM), BF16), jax.ShapeDtypeStruct((b, t, hw), BF16),
        jax.ShapeDtypeStruct((b, t, hw), BF16), jax.ShapeDtypeStruct((b, t, hw), BF16),
        jax.ShapeDtypeStruct((b, t, 2 * hw), F32),
        jax.ShapeDtypeStruct((b, t // HG_CHUNK, hw, HG_CHUNK), BF16),
        jax.ShapeDtypeStruct((b, 2, t, fw), BF16),
    )
    out_specs = (
        bt(aw),
        pl.BlockSpec((1, kvh, HEAD_DIM, tm), lambda bi, i: (bi, 0, 0, i)),
        pl.BlockSpec((1, kvh, tm, 2 * HEAD_DIM), lambda bi, i: (bi, 0, i, 0)),
        bt(hw), bt(hw), bt(hw), bt(2 * hw),
        pl.BlockSpec((1, tm // HG_CHUNK, hw, HG_CHUNK), lambda bi, i: (bi, i, 0, 0)),
        pl.BlockSpec((1, 2, tm, fw), lambda bi, i: (bi, 0, i, 0)),
    )
    return pl.pallas_call(
        functools.partial(_inproj_kernel, rope=rope, offs=offs),
        out_shape=out_shapes,
        grid=(b, t // tm),
        in_specs=[
            pl.BlockSpec((1, tm, d), lambda bi, i: (bi, i, 0)),
            pl.BlockSpec((1, 1, d), lambda bi, i: (row(bi), 0, 0)),
            pl.BlockSpec((1, 1, d), lambda bi, i: (row(bi), 0, 1)),
            _full(w_in.shape),
            pl.BlockSpec((tm, 128), lambda bi, i: (i, 0)),
            pl.BlockSpec((tm, 128), lambda bi, i: (i, 0)),
            _full(qg.shape), _full(kg.shape), _full(bdq.shape), _full(bdk.shape), _full(dftc.shape),
        ],
        out_specs=out_specs,
        scratch_shapes=[pltpu.VMEM((tm, d), BF16), pltpu.VMEM((tm, aw + kw), F32)],
        compiler_params=_cparams("parallel", "parallel"),
        name="inproj",
    )(x, mod3, mod3, w_in, cos, sin, qg, kg, bdq, bdk, dftc)


def _attn_kernel(q_ref, kt_ref, v_ref, o_ref, s_scr, m_scr, p_scr, *, group):
    kt = kt_ref[0, 0]
    v = v_ref[0, 0]
    half = (v.shape[0] // 2) // 256 * 256

    def scores(h):
        s = _dot(q_ref[0, :, h * HEAD_DIM:(h + 1) * HEAD_DIM], kt)
        s_scr[h] = s
        m_scr[h] = jnp.max(s, axis=-1, keepdims=True)

    def probs(h):
        p_scr[h] = jnp.exp2(s_scr[h] - m_scr[h]).astype(BF16)

    def values(h):
        if half:
            pv = _dot(p_scr[h, :, :half], v[:half]) + _dot(p_scr[h, :, half:], v[half:])
        else:
            pv = _dot(p_scr[h], v)
        return pv[:, :HEAD_DIM] / pv[:, HEAD_DIM:HEAD_DIM + 1]

    outs = [None] * group
    for step in range(group + 2):
        if step < group:
            scores(step)
        if 0 <= step - 1 < group:
            probs(step - 1)
        if 0 <= step - 2 < group:
            outs[step - 2] = values(step - 2)
    o_ref[0] = jnp.concatenate(outs, axis=1).astype(BF16)


def _attention(q, kt_all, v_all, *, tq):
    b, t, aw = q.shape
    kvh, s = v_all.shape[1], v_all.shape[2]
    group = aw // HEAD_DIM // kvh
    gw = group * HEAD_DIM
    return pl.pallas_call(
        functools.partial(_attn_kernel, group=group),
        out_shape=jax.ShapeDtypeStruct((b, t, aw), BF16),
        grid=(b, kvh, t // tq),
        in_specs=[
            pl.BlockSpec((1, tq, gw), lambda bi, g, i: (bi, i, g)),
            pl.BlockSpec((1, 1, HEAD_DIM, s), lambda bi, g, i: (bi, g, 0, 0)),
            pl.BlockSpec((1, 1, s, 2 * HEAD_DIM), lambda bi, g, i: (bi, g, 0, 0)),
        ],
        out_specs=pl.BlockSpec((1, tq, gw), lambda bi, g, i: (bi, i, g)),
        scratch_shapes=[pltpu.VMEM((group, tq, s), F32), pltpu.VMEM((group, tq, 1), F32),
                        pltpu.VMEM((group, tq, s), BF16)],
        compiler_params=_cparams("parallel", "parallel", "parallel"),
        name="attention",
    )(q, kt_all, v_all)


def _hg_prepare(rev, hq, z, v, hit_ref, c0, lb, tri_ref, sel_ref, o_ref, row0, qb_ref, qs_ref, ks_ref, si_ref, dl_ref,
                d_ref):
    r, w = z.shape
    n = r // HG_CHUNK
    f = lb + (1.0 - lb) * jax.nn.sigmoid(z)
    logf = jnp.log2(jnp.maximum(f, F_MIN))
    kk = 1.0 - f
    qs = _silu(hq.astype(F32))
    hi = logf.astype(BF16)
    lo = (logf - hi.astype(F32)).astype(BF16)
    tri = tri_ref[0:r, 0:r]
    cum = _dot(tri, hi) + _dot(tri, lo)
    yield "local"

    c3 = cum.reshape(n, HG_CHUNK, w)
    q3 = qs.reshape(n, HG_CHUNK, w)
    k3 = kk.reshape(n, HG_CHUNK, w)

    def rows(a, lo_, hi_):
        parts = []
        if lo_ > 0:
            parts.append(jnp.zeros((n, lo_, w), F32))
        parts.append(a)
        if hi_ < HG_CHUNK:
            parts.append(jnp.zeros((n, HG_CHUNK - hi_, w), F32))
        return jnp.concatenate(parts, axis=1) if len(parts) > 1 else a

    e_last = c3[:, 0:1, :] if rev else c3[:, HG_CHUNK - 1:HG_CHUNK, :]
    qparts, kparts = [], []
    for g in range(1, HG_NB):
        j = HG_NB - g if rev else g - 1
        b0, b1 = j * HG_BLOCK, (j + 1) * HG_BLOCK
        e = c3[:, b0:b0 + 1, :] if rev else c3[:, b1 - 1:b1, :]
        p0, p1 = (0, b0) if rev else (b1, HG_CHUNK)
        qparts.append(rows(q3[:, p0:p1, :] * jnp.exp2(c3[:, p0:p1, :] - e), p0, p1))
        kparts.append(rows(k3[:, b0:b1, :] * jnp.exp2(e - c3[:, b0:b1, :]), b0, b1))
    qparts.append(q3 * jnp.exp2(c3))
    kparts.append(k3 * jnp.exp2(e_last - c3))
    dl = jnp.exp2(e_last).reshape(n, w)
    pairs = w // HG_PAIR
    ng = HG_NB - 1
    for pp in range(pairs):
        lanes = slice(pp * HG_PAIR, (pp + 1) * HG_PAIR)
        qb_ref[pp, 0:r, :] = jnp.concatenate([p[:, :, lanes] for p in qparts[:ng]], axis=-1).reshape(
            r, ng * HG_PAIR).astype(BF16)
        qs_ref[pp, 0:r, :] = qparts[ng][:, :, lanes].reshape(r, HG_PAIR).astype(BF16)
        ks_ref[pp, 0:r, :] = jnp.concatenate([p[:, :, lanes] for p in kparts], axis=-1).reshape(
            r, HG_NB * HG_PAIR).astype(BF16)
        dl_ref[pp, 0:n, :] = dl[:, lanes]
        yield "local"

    nb = r // HG_BLOCK
    hb = HG_BLOCK // 2
    c5 = cum.reshape(nb, 2, hb, w)
    q5 = qs.reshape(nb, 2, hb, w)
    d5 = (cum - jnp.log2(jnp.maximum(kk, 0.0))).reshape(nb, 2, hb, w)
    tau = lax.broadcasted_iota(jnp.int32, (nb, hb, w), 1)
    halves = []
    for half in range(2):
        sources = [sg for sg in range(HG_BLOCK) if not ((half > sg // hb) if rev else (half < sg // hb))]
        eparts = []
        for sg in sources:
            sh, sl = sg // hb, sg % hb
            arg = c5[:, half] - d5[:, sh, sl:sl + 1, :]
            if half == sh:
                arg = jnp.where((tau <= sl) if rev else (tau >= sl), arg, NEG)
            eparts.append((jnp.exp2(arg) * q5[:, half]).reshape(nb * hb, w).astype(BF16))
            if len(eparts) % 2 == 0:
                yield "local"
        lo_ = sources[0] * w
        halves.append(_dot(jnp.concatenate(eparts, axis=-1), sel_ref[lo_:lo_ + len(sources) * w, :]))
        yield "local"
    a = jnp.stack([h_.reshape(nb, hb, w) for h_ in halves], axis=1).reshape(r, w)
    ti = lax.broadcasted_iota(jnp.int32, (r, w), 0)
    ci = lax.broadcasted_iota(jnp.int32, (r, w), 1)
    same_block = ((ti % HG_CHUNK) // HG_BLOCK) == ((ci % HG_CHUNK) // HG_BLOCK)
    ad = jnp.where(same_block, a, 0.0).astype(BF16)

    def same_head(shape):
        return (lax.broadcasted_iota(jnp.int32, shape, 0) // HG_DIM) == (lax.broadcasted_iota(jnp.int32, shape, 1) // HG_DIM)

    pair_mask = same_head((HG_PAIR, HG_PAIR))
    head_mask = same_head((w, w))
    for c in range(n):
        rws = slice(c * HG_CHUNK, (c + 1) * HG_CHUNK)
        vt = hit_ref[0, c0 + c]
        for pp in range(pairs):
            x = _dot(vt[pp * HG_PAIR:(pp + 1) * HG_PAIR, :], ks_ref[pp, rws, :])
            d_ref[c, pp] = jnp.concatenate(
                [jnp.where(pair_mask, x[:, g * HG_PAIR:(g + 1) * HG_PAIR], 0.0).astype(BF16) for g in range(ng)],
                axis=1)
            si_ref[c, pp] = jnp.where(pair_mask, x[:, ng * HG_PAIR:], 0.0)
            yield "products"
    for c in range(n):
        rws = slice(c * HG_CHUNK, (c + 1) * HG_CHUNK)
        outs = [lax.dot_general(qb_ref[pp, rws, :], d_ref[c, pp], _NT, preferred_element_type=F32)
                for pp in range(pairs)]
        vc = v[rws]
        vbd = jnp.where(head_mask, jnp.concatenate([vc] * (w // HG_CHUNK), axis=0), jnp.zeros((), vc.dtype))
        o_ref[row0 + c * HG_CHUNK:row0 + (c + 1) * HG_CHUNK, :] = jnp.concatenate(outs, axis=1) + _dot(ad[rws], vbd)
        yield "products"


def _run_interleaved(first, second):
    live = [first, second]
    while live:
        for g in list(live):
            if next(g, None) is None:
                live.remove(g)


def _hg_step(qs_ref, si_ref, dl_ref, st_ref, o_ref, r0, g0, ci):
    outs = []
    for pp in range(st_ref.shape[0]):
        st = st_ref[pp]
        outs.append(lax.dot_general(qs_ref[pp, pl.ds(r0, HG_CHUNK), :], st.astype(BF16), _NT,
                                    preferred_element_type=F32))
        st_ref[pp] = st * dl_ref[pp, pl.ds(ci, 1), :] + si_ref[ci, pp]
    o_ref[pl.ds(g0, HG_CHUNK), :] += jnp.concatenate(outs, axis=1)


def _hgrn_kernel(hq_ref, hi_ref, hit_ref, hf_ref, hg_ref, hqc_ref, hic_ref, hitc_ref, hfc_ref, hgc_ref,
                 lb_ref, gn_ref, tri_ref, sel_ref, bd_ref, *rest, need_ctx):
    if need_ctx:
        rec_ref, recc_ref = rest[0], rest[1]
        scr = rest[2:]
    else:
        rec_ref, recc_ref = rest[0], None
        scr = rest[1:]
    of_ref, ob_ref, ofc_ref, obc_ref, stf_ref, stb_ref = scr[:6]
    fwd, bwd = scr[6:12], scr[12:18]
    w = hq_ref.shape[2]
    lb_f = lb_ref[0]
    lb_b = lb_ref[1]
    stf_ref[...] = jnp.zeros_like(stf_ref)
    stb_ref[...] = jnp.zeros_like(stb_ref)

    def run(hq, hi, hit, hf, o_f, o_b):
        t = hq.shape[1]
        tile = min(HG_TILE, t)
        nt = t // tile
        cpt = tile // HG_CHUNK
        for s in range(nt):
            rf = s * tile
            rb = (nt - 1 - s) * tile
            _run_interleaved(
                _hg_prepare(False, hq[0, rf:rf + tile, :], hf[0, rf:rf + tile, 0:w], hi[0, rf:rf + tile, :], hit,
                            rf // HG_CHUNK, lb_f, tri_ref.at[0], sel_ref, o_f, rf, *fwd),
                _hg_prepare(True, hq[0, rb:rb + tile, :], hf[0, rb:rb + tile, w:2 * w], hi[0, rb:rb + tile, :], hit,
                            rb // HG_CHUNK, lb_b, tri_ref.at[1], sel_ref, o_b, rb, *bwd))

            def body(i, carry):
                cf = i
                cb = cpt - 1 - i
                r0f = pl.multiple_of(cf * HG_CHUNK, HG_CHUNK)
                r0b = pl.multiple_of(cb * HG_CHUNK, HG_CHUNK)
                gf = pl.multiple_of(rf + cf * HG_CHUNK, HG_CHUNK)
                gb = pl.multiple_of(rb + cb * HG_CHUNK, HG_CHUNK)
                _hg_step(fwd[1], fwd[3], fwd[4], stf_ref, o_f, r0f, gf, cf)
                _hg_step(bwd[1], bwd[3], bwd[4], stb_ref, o_b, r0b, gb, cb)
                return carry

            lax.fori_loop(0, cpt, body, 0)

    def finish(o_f, o_b, hg, out):
        t = hg.shape[1]
        tile = min(HG_TILE, t)
        for s in range(t // tile):
            rows = slice(s * tile, (s + 1) * tile)
            o = o_f[rows, :] + o_b[rows, :]
            ms = _dot((o * o).astype(BF16), bd_ref[...])
            on = o * lax.rsqrt(ms + EPS) * gn_ref[...]
            out[0, rows, :] = (on * _silu(hg[0, rows, :].astype(F32))).astype(BF16)

    run(hqc_ref, hic_ref, hitc_ref, hfc_ref, ofc_ref, obc_ref)
    run(hq_ref, hi_ref, hit_ref, hf_ref, of_ref, ob_ref)
    finish(of_ref, ob_ref, hg_ref, rec_ref)
    if need_ctx:
        finish(ofc_ref, obc_ref, hgc_ref, recc_ref)


def _hgrn(lat, cx, lb, gn, tri, sel, bd, *, need_ctx):
    hq, hi, hit, hf, hg = lat
    hqc, hic, hitc, hfc, hgc = cx
    b, t, w = hq.shape
    tc = hqc.shape[1]
    tile = min(HG_TILE, t)

    def seq(a):
        nd = a.ndim
        return pl.BlockSpec((1,) + a.shape[1:], lambda bi: (bi,) + (0,) * (nd - 1))

    out_shape = [jax.ShapeDtypeStruct((b, t, w), BF16)]
    out_specs = [pl.BlockSpec((1, t, w), lambda bi: (bi, 0, 0))]
    if need_ctx:
        out_shape.append(jax.ShapeDtypeStruct((b, tc, w), BF16))
        out_specs.append(pl.BlockSpec((1, tc, w), lambda bi: (bi, 0, 0)))
    pairs = w // HG_PAIR
    cpt = tile // HG_CHUNK
    dir_scratch = [
        pltpu.VMEM((pairs, tile, (HG_NB - 1) * HG_PAIR), BF16),
        pltpu.VMEM((pairs, tile, HG_PAIR), BF16),
        pltpu.VMEM((pairs, tile, HG_NB * HG_PAIR), BF16),
        pltpu.VMEM((cpt, pairs, HG_PAIR, HG_PAIR), F32),
        pltpu.VMEM((pairs, cpt, HG_PAIR), F32),
        pltpu.VMEM((cpt, pairs, HG_PAIR, (HG_NB - 1) * HG_PAIR), BF16),
    ]
    res = pl.pallas_call(
        functools.partial(_hgrn_kernel, need_ctx=need_ctx),
        out_shape=tuple(out_shape),
        grid=(b,),
        in_specs=[seq(hq), seq(hi), seq(hit), seq(hf), seq(hg), seq(hqc), seq(hic), seq(hitc), seq(hfc), seq(hgc),
                  _full(lb.shape), _full(gn.shape), _full(tri.shape), _full(sel.shape), _full(bd.shape)],
        out_specs=tuple(out_specs),
        scratch_shapes=[pltpu.VMEM((t, w), F32), pltpu.VMEM((t, w), F32),
                        pltpu.VMEM((tc, w), F32), pltpu.VMEM((tc, w), F32),
                        pltpu.VMEM((pairs, HG_PAIR, HG_PAIR), F32), pltpu.VMEM((pairs, HG_PAIR, HG_PAIR), F32)]
        + dir_scratch + dir_scratch,
        compiler_params=_cparams("parallel"),
        name="hgrn",
    )(hq, hi, hit, hf, hg, hqc, hic, hitc, hfc, hgc, lb, gn, tri, sel, bd)
    return res if need_ctx else (res[0], None)


def _dft_kernel(cos_ref, msin_ref, ab_ref, w_ref, o_ref):
    z = _dot(cos_ref[...], ab_ref[0, 0]) + _dot(msin_ref[...], ab_ref[0, 1])
    o_ref[0] = _dot(z.astype(BF16), w_ref[...]).astype(BF16)


def _dft(cs, fab, wbd, *, tf):
    b, _, t, w = fab.shape
    return pl.pallas_call(
        _dft_kernel,
        out_shape=jax.ShapeDtypeStruct((b, t, w), BF16),
        grid=(t // tf, b),
        in_specs=[
            pl.BlockSpec((tf, t), lambda i, bi: (i, 0)),
            pl.BlockSpec((tf, t), lambda i, bi: (i, 0)),
            pl.BlockSpec((1, 2, t, w), lambda i, bi: (bi, 0, 0, 0)),
            _full(wbd.shape),
        ],
        out_specs=pl.BlockSpec((1, tf, w), lambda i, bi: (bi, i, 0)),
        compiler_params=_cparams("parallel", "parallel"),
        name="dft",
    )(cs[0], cs[1], fab, wbd)


def _outproj_kernel(a_ref, r_ref, f_ref, x_ref, g1_ref, w_ref, lg_ref, lb_ref, o_ref, mix_scr, *, alpha):
    aw = a_ref.shape[2]
    rw = r_ref.shape[2]
    tm = x_ref.shape[1]
    nsub = 2 if tm % (2 * BF16_ROWS) == 0 else 1
    sub = tm // nsub

    def mix(i):
        rows = slice(i * sub, (i + 1) * sub)
        mix_scr[rows, :] = (_dot(a_ref[0, rows, :], w_ref[0:aw, :]) + _dot(r_ref[0, rows, :], w_ref[aw:aw + rw, :])
                            + _dot(f_ref[0, rows, :], w_ref[aw + rw:, :]))

    def norm(i):
        rows = slice(i * sub, (i + 1) * sub)
        o_ref[0, rows, :] = _layer_norm(alpha * x_ref[0, rows, :] + g1_ref[0] * mix_scr[rows, :],
                                        lg_ref[...], lb_ref[...])

    for step in range(nsub + 1):
        if step < nsub:
            mix(step)
        if step >= 1:
            norm(step - 1)


def _outproj(attn, rec, four, x, mod3, mod_row, w_out, lg, lb, *, alpha, tm):
    b, t, d = x.shape
    bt = lambda a: pl.BlockSpec((1, tm, a.shape[2]), lambda bi, i: (bi, i, 0))
    return pl.pallas_call(
        functools.partial(_outproj_kernel, alpha=alpha),
        out_shape=jax.ShapeDtypeStruct((b, t, d), F32),
        grid=(b, t // tm),
        in_specs=[bt(attn), bt(rec), bt(four), bt(x),
                  pl.BlockSpec((1, 1, d), lambda bi, i: (mod_row(bi), 0, 2)),
                  _full(w_out.shape), _full(lg.shape), _full(lb.shape)],
        out_specs=pl.BlockSpec((1, tm, d), lambda bi, i: (bi, i, 0)),
        scratch_shapes=[pltpu.VMEM((tm, d), F32)],
        compiler_params=_cparams("parallel", "parallel"),
        name="outproj",
    )(attn, rec, four, x, mod3, w_out, lg, lb)


def _ffn_kernel(x_ref, xp_ref, xn_ref, sh_ref, sc_ref, g2_ref, wup_ref, cw_ref, cb_ref, wdn_ref, lg_ref, lb_ref,
                o_ref, h_scr, u_scr, act_scr, y_scr, *, alpha):
    i = pl.program_id(1)
    last = pl.num_programs(1) - 1
    tm = x_ref.shape[1]
    halo = BF16_ROWS
    sc = sc_ref[0]
    sh = sh_ref[0]

    def mod(v):
        return (v * (1.0 + sc) + sh).astype(BF16)

    h_scr[0:halo, :] = jnp.where(i > 0, mod(xp_ref[0]), jnp.zeros((), BF16))
    h_scr[halo:halo + tm, :] = mod(x_ref[0])
    h_scr[halo + tm:, :] = jnp.where(i < last, mod(xn_ref[0]), jnp.zeros((), BF16))
    d_ff = wdn_ref.shape[0]
    cf = u_scr.shape[3]

    def conv(ub, cols):
        cw = cw_ref[:, cols]
        u = ub[...]
        rows = u.shape[0]
        prev = pltpu.roll(u, 1, 0)[halo:halo + tm]
        nxt = pltpu.roll(u, rows - 1, 0)[halo:halo + tm]
        return prev * cw[0:1, :] + u[halo:halo + tm] * cw[1:2, :] + nxt * cw[2:3, :] + cb_ref[:, cols]

    for j in range(d_ff // cf):
        val_cols = slice(j * cf, (j + 1) * cf)
        gate_cols = slice(d_ff + j * cf, d_ff + (j + 1) * cf)
        uv, ug = u_scr.at[j % 2, 0], u_scr.at[j % 2, 1]
        uv[...] = _dot(h_scr[...], wup_ref[:, val_cols])
        ug[...] = _dot(h_scr[...], wup_ref[:, gate_cols])
        act_scr[:, val_cols] = (conv(uv, val_cols) * _silu(conv(ug, gate_cols))).astype(BF16)
    nsub = 2 if tm % (2 * BF16_ROWS) == 0 else 1
    sub = tm // nsub
    for step in range(nsub + 1):
        if step < nsub:
            rows = slice(step * sub, (step + 1) * sub)
            y_scr[rows, :] = _dot(act_scr[rows, :], wdn_ref[...])
        if step >= 1:
            rows = slice((step - 1) * sub, step * sub)
            o_ref[0, rows, :] = _layer_norm(alpha * x_ref[0, rows, :] + g2_ref[0] * y_scr[rows, :],
                                            lg_ref[...], lb_ref[...])


def _ffn(x1, mod3, mod_row, wup, cw, cb, wdn, lg, lb, *, alpha, tm, cf):
    b, t, d = x1.shape
    halo = BF16_ROWS
    per = tm // halo
    nblk = t // halo
    return pl.pallas_call(
        functools.partial(_ffn_kernel, alpha=alpha),
        out_shape=jax.ShapeDtypeStruct((b, t, d), F32),
        grid=(b, t // tm),
        in_specs=[
            pl.BlockSpec((1, tm, d), lambda bi, i: (bi, i, 0)),
            pl.BlockSpec((1, halo, d), lambda bi, i: (bi, jnp.maximum(i * per - 1, 0), 0)),
            pl.BlockSpec((1, halo, d), lambda bi, i: (bi, jnp.minimum((i + 1) * per, nblk - 1), 0)),
            pl.BlockSpec((1, 1, d), lambda bi, i: (mod_row(bi), 0, 3)),
            pl.BlockSpec((1, 1, d), lambda bi, i: (mod_row(bi), 0, 4)),
            pl.BlockSpec((1, 1, d), lambda bi, i: (mod_row(bi), 0, 5)),
            _resident(wup.shape), _full(cw.shape), _full(cb.shape), _resident(wdn.shape), _full(lg.shape),
            _full(lb.shape),
        ],
        out_specs=pl.BlockSpec((1, tm, d), lambda bi, i: (bi, i, 0)),
        scratch_shapes=[pltpu.VMEM((tm + 2 * halo, d), BF16), pltpu.VMEM((2, 2, tm + 2 * halo, cf), F32),
                        pltpu.VMEM((tm, wdn.shape[0]), BF16), pltpu.VMEM((tm, d), F32)],
        compiler_params=_cparams("parallel", "arbitrary"),
        name="ffn",
    )(x1, x1, x1, mod3, mod3, mod3, wup, cw, cb, wdn, lg, lb)


def _block_diag_mean(width, seg):
    i = np.arange(width)
    return jnp.asarray(((i[:, None] // seg) == (i[None, :] // seg)).astype(np.float32) / seg, dtype=BF16)


def _hg_tri(n):
    i = np.arange(n)
    same = (i[:, None] // HG_CHUNK) == (i[None, :] // HG_CHUNK)
    lower = same & (i[None, :] <= i[:, None])
    upper = same & (i[None, :] >= i[:, None])
    return jnp.asarray(np.stack([lower, upper]).astype(np.float32), dtype=BF16)


def _hg_select(w):
    r = np.arange(HG_BLOCK * w)
    c = np.arange(w)
    sg, head_r = r // w, (r % w) // HG_DIM
    head_c, s = c // HG_DIM, c % HG_DIM
    m = (head_r[:, None] == head_c[None, :]) & ((s[None, :] % HG_BLOCK) == sg[:, None])
    return jnp.asarray(m.astype(np.float32), dtype=BF16)


def _position_dft(t):
    lo = 64
    f = jnp.arange(t, dtype=jnp.int32)[:, None]
    step = 2.0 * np.pi / t
    a = ((f * (lo * jnp.arange(t // lo, dtype=jnp.int32))[None, :]) % t).astype(F32) * step
    bb = ((f * jnp.arange(lo, dtype=jnp.int32)[None, :]) % t).astype(F32) * step
    ca, sa = jnp.cos(a)[:, :, None], jnp.sin(a)[:, :, None]
    cb, sb = jnp.cos(bb)[:, None, :], jnp.sin(bb)[:, None, :]
    norm = 1.0 / np.sqrt(t)
    cos = ((ca * cb - sa * sb) * norm).astype(BF16).reshape(t, t)
    msin = ((sa * cb + ca * sb) * -norm).astype(BF16).reshape(t, t)
    return cos, msin


def _channel_dft(groups):
    c = np.arange(FT_DIM, dtype=np.int64)
    angc = 2.0 * np.pi * ((c[:, None] * c[None, :]) % FT_DIM).astype(np.float64) / FT_DIM
    eye = np.eye(groups)
    dftc = np.concatenate([np.kron(eye, np.cos(angc)), np.kron(eye, np.sin(angc))], axis=1) / np.sqrt(FT_DIM)
    return jnp.asarray(dftc, dtype=F32).astype(BF16)


def _rope_tables(t):
    rows = t // GRID_W
    row = jnp.repeat(jnp.arange(rows, dtype=jnp.int32), GRID_W)
    col = jnp.tile(jnp.arange(GRID_W, dtype=jnp.int32), rows)
    nf = HEAD_DIM // 4
    inv = ROPE_THETA ** (-jnp.arange(nf, dtype=F32) / nf)
    ar = row.astype(F32)[:, None] * inv
    ac = col.astype(F32)[:, None] * inv
    cos = jnp.concatenate([jnp.cos(ar)] * 2 + [jnp.cos(ac)] * 2, axis=1)
    sin = jnp.concatenate([-jnp.sin(ar), jnp.sin(ar), -jnp.sin(ac), jnp.sin(ac)], axis=1)
    return jnp.tile(cos, (1, 128 // HEAD_DIM)), jnp.tile(sin, (1, 128 // HEAD_DIM))


def kernel(x, c, ctx, c_ctx, w_ada, b_ada, w_in, q_norm, k_norm, hg_lb, hg_norm, ft_w, w_out, ln1_g, ln1_b, w_up,
           conv_w, conv_b, w_down, ln2_g, ln2_b):
    b, t, d = x.shape
    tc = ctx.shape[1]
    depth = w_ada.shape[0]
    d_ff = w_down.shape[1]
    hgw = hg_lb.shape[2]
    ftw = ft_w.shape[1] * ft_w.shape[2]
    kvw = ATTN_KV_HEADS * HEAD_DIM
    aw = w_in.shape[2] - 2 * kvw - 5 * hgw - ftw
    alpha = (2 * depth) ** 0.25

    sizes = [("q", aw), ("k", kvw), ("v", kvw), ("hq", hgw), ("hi", hgw), ("hf", 2 * hgw), ("hg", hgw), ("ft", ftw)]
    offs, lo = {}, 0
    for name, size in sizes:
        offs[name] = (lo, lo + size)
        lo += size

    mod_rows = -(-(b + 1) // 8) * 8
    cond = jnp.concatenate([c, c_ctx[None, :], jnp.zeros((mod_rows - b - 1, d), F32)], axis=0)
    mod = _ada(cond, w_ada, b_ada)
    mod3 = mod.reshape(depth * mod_rows, 1, 6 * d)

    lb_soft = jax.nn.softmax(hg_lb.astype(F32), axis=0)
    lower = (jnp.cumsum(lb_soft, axis=0) - lb_soft[:1]).reshape(depth, 2, 1, hgw)

    cos, sin = _rope_tables(t)
    bdq = _block_diag_mean(2 * 128, HEAD_DIM)
    bdk = _block_diag_mean(kvw, HEAD_DIM)
    bdh = _block_diag_mean(hgw, HG_DIM)
    tri = _hg_tri(min(HG_TILE, t))
    sel = _hg_select(hgw)
    dftc = _channel_dft(ftw // FT_DIM)
    cs_lat = _position_dft(t)
    cs_ctx = _position_dft(tc)

    cf = 256
    assert d_ff % cf == 0

    for l in range(depth):
        need_ctx = l < depth - 1
        lat_row = lambda bi, l=l: l * mod_rows + bi
        ctx_row = lambda bi, l=l: l * mod_rows + b

        w_in_l = w_in[l].astype(BF16)
        qg = jnp.tile(q_norm[l].astype(F32), aw // HEAD_DIM)[None, :]
        kg = jnp.tile(k_norm[l].astype(F32), kvw // HEAD_DIM)[None, :]
        proj_args = (w_in_l, cos, sin, qg, kg, bdq, bdk, dftc)
        q, k, v, hq, hi, hg, hf, hit, fab = _inproj(x, mod3, lat_row, l, *proj_args, offs=offs, rope=True, tm=512)
        qc, kc, vc, hqc, hic, hgc, hfc, hitc, fabc = _inproj(ctx, mod3, ctx_row, l, *proj_args, offs=offs,
                                                               rope=False, tm=tc)

        attn = _attention(q, jnp.concatenate([kc, k], axis=3), jnp.concatenate([vc, v], axis=2), tq=256)

        gn = jnp.tile(hg_norm[l].astype(F32), hgw // HG_DIM)[None, :]
        rec, rec_c = _hgrn((hq, hi, hit, hf, hg), (hqc, hic, hitc, hfc, hgc), lower[l], gn, tri, sel, bdh,
                           need_ctx=need_ctx)

        wbd = jax.scipy.linalg.block_diag(*[ft_w[l, g] for g in range(ft_w.shape[1])]).astype(BF16)
        four = _dft(cs_lat, fab, wbd, tf=512)

        w_out_l = w_out[l].astype(BF16)
        lg1, lb1 = ln1_g[l][None, :], ln1_b[l][None, :]
        lg2, lb2 = ln2_g[l][None, :], ln2_b[l][None, :]
        ffn_args = (w_up[l].astype(BF16), conv_w[l], conv_b[l][None, :], w_down[l].astype(BF16), lg2, lb2)

        x1 = _outproj(attn, rec, four, x, mod3, lat_row, w_out_l, lg1, lb1, alpha=alpha, tm=512)
        x_next = _ffn(x1, mod3, lat_row, *ffn_args, alpha=alpha, tm=512, cf=cf)

        if need_ctx:
            attn_c = _attention(qc, kc, vc, tq=tc)
            four_c = _dft(cs_ctx, fabc, wbd, tf=tc)
            ctx1 = _outproj(attn_c, rec_c, four_c, ctx, mod3, ctx_row, w_out_l, lg1, lb1, alpha=alpha, tm=tc)
            ctx = _ffn(ctx1, mod3, ctx_row, *ffn_args, alpha=alpha, tm=tc, cf=cf)
        x = x_next

    return x
```

```python
import functools

import numpy as np
import jax
import jax.numpy as jnp
from jax import lax
from jax.experimental import pallas as pl
from jax.experimental.pallas import tpu as pltpu

F32 = jnp.float32
BF16 = jnp.bfloat16

GRID_W = 64
HEAD_DIM = 64
ATTN_KV_HEADS = 2
HG_DIM = 64
FT_DIM = 64
ROPE_THETA = 10000.0
F_MIN = 1e-30
EPS = 1e-6
CONV_W = 3

HG_CHUNK = 64
HG_BLOCK = 16
HG_NB = HG_CHUNK // HG_BLOCK
HG_PAIR = 2 * HG_DIM
HG_TILE = 512
NEG = -1e30
LOG2_E = 1.4426950408889634

V7X_VMEM_BYTES = 64 * 1024 * 1024
VMEM_LIMIT = int(V7X_VMEM_BYTES * 0.85)
BF16_ROWS = 16

_NT = (((1,), (1,)), ((), ()))


def _cparams(*sem):
    return pltpu.CompilerParams(dimension_semantics=sem, vmem_limit_bytes=VMEM_LIMIT)


def _dot(a, b):
    return jnp.dot(a, b, preferred_element_type=F32)


def _silu(x):
    return x * jax.nn.sigmoid(x)


def _layer_norm(y, g, b):
    mu = jnp.mean(y, axis=-1, keepdims=True)
    d = y - mu
    var = jnp.mean(d * d, axis=-1, keepdims=True)
    return d * lax.rsqrt(var + EPS) * g + b


def _full(shape):
    n = len(shape)
    return pl.BlockSpec(shape, lambda *_: (0,) * n)


def _layer(w, l, buffers=2):
    n = w.ndim - 1
    return pl.BlockSpec((None,) + w.shape[1:], lambda *_: (l,) + (0,) * n, pipeline_mode=pl.Buffered(buffers))


def _ada_kernel(c_ref, w_ref, b_ref, o_ref):
    s = _silu(c_ref[...]).astype(BF16)
    o_ref[0] = _dot(s, w_ref[0].astype(BF16)) + b_ref[0]


def _ada(cond, w_ada, b_ada):
    depth, d, n = w_ada.shape
    rows = cond.shape[0]
    tn = n // 4
    return pl.pallas_call(
        _ada_kernel,
        out_shape=jax.ShapeDtypeStruct((depth, rows, n), F32),
        grid=(depth, n // tn),
        in_specs=[
            pl.BlockSpec((rows, d), lambda l, j: (0, 0)),
            pl.BlockSpec((1, d, tn), lambda l, j: (l, 0, j)),
            pl.BlockSpec((1, 1, tn), lambda l, j: (l, 0, j)),
        ],
        out_specs=pl.BlockSpec((1, rows, tn), lambda l, j: (l, 0, j)),
        compiler_params=_cparams("arbitrary", "arbitrary"),
        name="ada",
    )(cond, w_ada, b_ada.reshape(depth, 1, n))


def _rope128(y, cos, sin_signed):
    lane = lax.broadcasted_iota(jnp.int32, y.shape, 1)
    upper = (lane % 32) >= 16
    partner = jnp.where(upper, pltpu.roll(y, 16, 1), pltpu.roll(y, 112, 1))
    return y * cos + partner * sin_signed


def _inproj_kernel(x_ref, sh_ref, sc_ref, w_ref, cos_ref, sin_ref, qg_ref, kg_ref, bdq_ref, bdk_ref,
                   dftc_ref, q_ref, k_ref, v_ref, hq_ref, hi_ref, hg_ref, hf_ref, hit_ref, fab_ref, h_scr, qk_scr,
                   *, rope, offs):
    h_scr[...] = (x_ref[0] * (1.0 + sc_ref[0]) + sh_ref[0]).astype(BF16)

    def proj(name):
        lo, hi = offs[name]
        return _dot(h_scr[...], w_ref[:, lo:hi])

    q_lo, q_hi = offs["q"]
    k_lo, k_hi = offs["k"]

    def norm_rope(lo, hi, bd_ref, g_ref):
        base = q_lo if lo < q_hi else k_lo
        y = qk_scr[:, lo - q_lo:hi - q_lo]
        y = y * lax.rsqrt(_dot((y * y).astype(BF16), bd_ref[...]) + EPS) * g_ref[:, lo - base:hi - base]
        if rope:
            cos, sin = cos_ref[...], sin_ref[...]
            y = jnp.concatenate([_rope128(y[:, c:c + 128], cos, sin) for c in range(0, hi - lo, 128)], axis=1)
        return y

    q_scale = HEAD_DIM ** -0.5 * LOG2_E

    def q_part(lo, hi):
        q_ref[0, :, lo - q_lo:hi - q_lo] = (norm_rope(lo, hi, bdq_ref, qg_ref) * q_scale).astype(BF16)

    qk_scr[...] = _dot(h_scr[...], w_ref[:, q_lo:k_hi])
    v = proj("v")
    lane = lax.broadcasted_iota(jnp.int32, v.shape, 1)
    for hh in range(v.shape[1] // HEAD_DIM):
        vh = v if hh == 0 else pltpu.roll(v, v.shape[1] - hh * HEAD_DIM, 1)
        v_ref[0, hh] = jnp.where(lane < HEAD_DIM, vh, jnp.where(lane == HEAD_DIM, 1.0, 0.0)).astype(BF16)
    hq_ref[0] = proj("hq").astype(BF16)
    qw = bdq_ref.shape[0]
    q_part(q_lo, q_lo + qw)
    hi = proj("hi")
    hi_ref[0] = hi.astype(BF16)
    vt = hi.T.astype(BF16)
    for j in range(vt.shape[1] // HG_CHUNK):
        hit_ref[0, j] = vt[:, j * HG_CHUNK:(j + 1) * HG_CHUNK]
    for lo in range(q_lo + qw, q_hi, qw):
        q_part(lo, lo + qw)
    hg_ref[0] = proj("hg").astype(BF16)
    kt = norm_rope(k_lo, k_hi, bdk_ref, kg_ref).T.astype(BF16)
    for hh in range(kt.shape[0] // HEAD_DIM):
        k_ref[0, hh] = kt[hh * HEAD_DIM:(hh + 1) * HEAD_DIM, :]
    hf_ref[0] = proj("hf")
    ab = _dot(proj("ft").astype(BF16), dftc_ref[...])
    half = ab.shape[1] // 2
    fab_ref[0, 0] = ab[:, :half].astype(BF16)
    fab_ref[0, 1] = ab[:, half:].astype(BF16)


def _inproj(x, mod3, mod_row, l, w_in, cos, sin, qg, kg, bdq, bdk, dftc, *, offs, rope, tm):
    b, t, d = x.shape
    aw = offs["q"][1] - offs["q"][0]
    kw = offs["k"][1] - offs["k"][0]
    hw = offs["hq"][1] - offs["hq"][0]
    fw = offs["ft"][1] - offs["ft"][0]
    row = lambda bi: mod_row(bi)
    bt = lambda width: pl.BlockSpec((1, tm, width), lambda bi, i: (bi, i, 0))
    kvh = kw // HEAD_DIM
    out_shapes = (
        jax.ShapeDtypeStruct((b, t, aw), BF16), jax.ShapeDtypeStruct((b, kvh, HEAD_DIM, t), BF16),
        jax.ShapeDtypeStruct((b, kvh, t, 2 * HEAD_DIM), BF16), jax.ShapeDtypeStruct((b, t, hw), BF16),
        jax.ShapeDtypeStruct((b, t, hw), BF16), jax.ShapeDtypeStruct((b, t, hw), BF16),
        jax.ShapeDtypeStruct((b, t, 2 * hw), F32),
        jax.ShapeDtypeStruct((b, t // HG_CHUNK, hw, HG_CHUNK), BF16),
        jax.ShapeDtypeStruct((b, 2, t, fw), BF16),
    )
    out_specs = (
        bt(aw),
        pl.BlockSpec((1, kvh, HEAD_DIM, tm), lambda bi, i: (bi, 0, 0, i)),
        pl.BlockSpec((1, kvh, tm, 2 * HEAD_DIM), lambda bi, i: (bi, 0, i, 0)),
        bt(hw), bt(hw), bt(hw), bt(2 * hw),
        pl.BlockSpec((1, tm // HG_CHUNK, hw, HG_CHUNK), lambda bi, i: (bi, i, 0, 0)),
        pl.BlockSpec((1, 2, tm, fw), lambda bi, i: (bi, 0, i, 0)),
    )
    return pl.pallas_call(
        functools.partial(_inproj_kernel, rope=rope, offs=offs),
        out_shape=out_shapes,
        grid=(b, t // tm),
        in_specs=[
            pl.BlockSpec((1, tm, d), lambda bi, i: (bi, i, 0)),
            pl.BlockSpec((1, 1, d), lambda bi, i: (row(bi), 0, 0)),
            pl.BlockSpec((1, 1, d), lambda bi, i: (row(bi), 0, 1)),
            _layer(w_in, l),
            pl.BlockSpec((tm, 128), lambda bi, i: (i, 0)),
            pl.BlockSpec((tm, 128), lambda bi, i: (i, 0)),
            _full(qg.shape), _full(kg.shape), _full(bdq.shape), _full(bdk.shape), _full(dftc.shape),
        ],
        out_specs=out_specs,
        scratch_shapes=[pltpu.VMEM((tm, d), BF16), pltpu.VMEM((tm, aw + kw), F32)],
        compiler_params=_cparams("parallel", "parallel"),
        name="inproj",
    )(x, mod3, mod3, w_in, cos, sin, qg, kg, bdq, bdk, dftc)


def _attn_kernel(q_ref, kt_ref, v_ref, o_ref, s_scr, m_scr, p_scr, *, group):
    kt = kt_ref[0, 0]
    v = v_ref[0, 0]
    half = (v.shape[0] // 2) // 256 * 256

    def scores(h):
        s = _dot(q_ref[0, :, h * HEAD_DIM:(h + 1) * HEAD_DIM], kt)
        s_scr[h] = s
        m_scr[h] = jnp.max(s, axis=-1, keepdims=True)

    def probs(h):
        p_scr[h] = jnp.exp2(s_scr[h] - m_scr[h]).astype(BF16)

    def values(h):
        if half:
            pv = _dot(p_scr[h, :, :half], v[:half]) + _dot(p_scr[h, :, half:], v[half:])
        else:
            pv = _dot(p_scr[h], v)
        return pv[:, :HEAD_DIM] / pv[:, HEAD_DIM:HEAD_DIM + 1]

    outs = [None] * group
    for step in range(group + 2):
        if step < group:
            scores(step)
        if 0 <= step - 1 < group:
            probs(step - 1)
        if 0 <= step - 2 < group:
            outs[step - 2] = values(step - 2)
    o_ref[0] = jnp.concatenate(outs, axis=1).astype(BF16)


def _attention(q, kt_all, v_all, *, tq):
    b, t, aw = q.shape
    kvh, s = v_all.shape[1], v_all.shape[2]
    group = aw // HEAD_DIM // kvh
    gw = group * HEAD_DIM
    return pl.pallas_call(
        functools.partial(_attn_kernel, group=group),
        out_shape=jax.ShapeDtypeStruct((b, t, aw), BF16),
        grid=(b, kvh, t // tq),
        in_specs=[
            pl.BlockSpec((1, tq, gw), lambda bi, g, i: (bi, i, g)),
            pl.BlockSpec((1, 1, HEAD_DIM, s), lambda bi, g, i: (bi, g, 0, 0)),
            pl.BlockSpec((1, 1, s, 2 * HEAD_DIM), lambda bi, g, i: (bi, g, 0, 0)),
        ],
        out_specs=pl.BlockSpec((1, tq, gw), lambda bi, g, i: (bi, i, g)),
        scratch_shapes=[pltpu.VMEM((group, tq, s), F32), pltpu.VMEM((group, tq, 1), F32),
                        pltpu.VMEM((group, tq, s), BF16)],
        compiler_params=_cparams("parallel", "parallel", "parallel"),
        name="attention",
    )(q, kt_all, v_all)


def _hg_prepare(rev, hq, z, v, hit_ref, c0, lb, tri_ref, sel_ref, o_ref, row0, qb_ref, qs_ref, ks_ref, si_ref, dl_ref,
                d_ref):
    r, w = z.shape
    n = r // HG_CHUNK
    f = lb + (1.0 - lb) * jax.nn.sigmoid(z)
    logf = jnp.log2(jnp.maximum(f, F_MIN))
    kk = 1.0 - f
    qs = _silu(hq.astype(F32))
    hi = logf.astype(BF16)
    lo = (logf - hi.astype(F32)).astype(BF16)
    tri = tri_ref[0:r, 0:r]
    cum = _dot(tri, hi) + _dot(tri, lo)
    yield "local"

    c3 = cum.reshape(n, HG_CHUNK, w)
    q3 = qs.reshape(n, HG_CHUNK, w)
    k3 = kk.reshape(n, HG_CHUNK, w)

    def rows(a, lo_, hi_):
        parts = []
        if lo_ > 0:
            parts.append(jnp.zeros((n, lo_, w), F32))
        parts.append(a)
        if hi_ < HG_CHUNK:
            parts.append(jnp.zeros((n, HG_CHUNK - hi_, w), F32))
        return jnp.concatenate(parts, axis=1) if len(parts) > 1 else a

    e_last = c3[:, 0:1, :] if rev else c3[:, HG_CHUNK - 1:HG_CHUNK, :]
    qparts, kparts = [], []
    for g in range(1, HG_NB):
        j = HG_NB - g if rev else g - 1
        b0, b1 = j * HG_BLOCK, (j + 1) * HG_BLOCK
        e = c3[:, b0:b0 + 1, :] if rev else c3[:, b1 - 1:b1, :]
        p0, p1 = (0, b0) if rev else (b1, HG_CHUNK)
        qparts.append(rows(q3[:, p0:p1, :] * jnp.exp2(c3[:, p0:p1, :] - e), p0, p1))
        kparts.append(rows(k3[:, b0:b1, :] * jnp.exp2(e - c3[:, b0:b1, :]), b0, b1))
    qparts.append(q3 * jnp.exp2(c3))
    kparts.append(k3 * jnp.exp2(e_last - c3))
    dl = jnp.exp2(e_last).reshape(n, w)
    pairs = w // HG_PAIR
    ng = HG_NB - 1
    for pp in range(pairs):
        lanes = slice(pp * HG_PAIR, (pp + 1) * HG_PAIR)
        qb_ref[pp, 0:r, :] = jnp.concatenate([p[:, :, lanes] for p in qparts[:ng]], axis=-1).reshape(
            r, ng * HG_PAIR).astype(BF16)
        qs_ref[pp, 0:r, :] = qparts[ng][:, :, lanes].reshape(r, HG_PAIR).astype(BF16)
        ks_ref[pp, 0:r, :] = jnp.concatenate([p[:, :, lanes] for p in kparts], axis=-1).reshape(
            r, HG_NB * HG_PAIR).astype(BF16)
        dl_ref[pp, 0:n, :] = dl[:, lanes]
        yield "local"

    nb = r // HG_BLOCK
    hb = HG_BLOCK // 2
    c5 = cum.reshape(nb, 2, hb, w)
    q5 = qs.reshape(nb, 2, hb, w)
    d5 = (cum - jnp.log2(jnp.maximum(kk, 0.0))).reshape(nb, 2, hb, w)
    tau = lax.broadcasted_iota(jnp.int32, (nb, hb, w), 1)
    halves = []
    for half in range(2):
        sources = [sg for sg in range(HG_BLOCK) if not ((half > sg // hb) if rev else (half < sg // hb))]
        eparts = []
        for sg in sources:
            sh, sl = sg // hb, sg % hb
            arg = c5[:, half] - d5[:, sh, sl:sl + 1, :]
            if half == sh:
                arg = jnp.where((tau <= sl) if rev else (tau >= sl), arg, NEG)
            eparts.append((jnp.exp2(arg) * q5[:, half]).reshape(nb * hb, w).astype(BF16))
            if len(eparts) % 2 == 0:
                yield "local"
        lo_ = sources[0] * w
        halves.append(_dot(jnp.concatenate(eparts, axis=-1), sel_ref[lo_:lo_ + len(sources) * w, :]))
        yield "local"
    a = jnp.stack([h_.reshape(nb, hb, w) for h_ in halves], axis=1).reshape(r, w)
    ti = lax.broadcasted_iota(jnp.int32, (r, w), 0)
    ci = lax.broadcasted_iota(jnp.int32, (r, w), 1)
    same_block = ((ti % HG_CHUNK) // HG_BLOCK) == ((ci % HG_CHUNK) // HG_BLOCK)
    ad = jnp.where(same_block, a, 0.0).astype(BF16)

    def same_head(shape):
        return (lax.broadcasted_iota(jnp.int32, shape, 0) // HG_DIM) == (lax.broadcasted_iota(jnp.int32, shape, 1) // HG_DIM)

    pair_mask = same_head((HG_PAIR, HG_PAIR))
    head_mask = same_head((w, w))
    for c in range(n):
        rws = slice(c * HG_CHUNK, (c + 1) * HG_CHUNK)
        vt = hit_ref[0, c0 + c]
        for pp in range(pairs):
            x = _dot(vt[pp * HG_PAIR:(pp + 1) * HG_PAIR, :], ks_ref[pp, rws, :])
            d_ref[c, pp] = jnp.concatenate(
                [jnp.where(pair_mask, x[:, g * HG_PAIR:(g + 1) * HG_PAIR], 0.0).astype(BF16) for g in range(ng)],
                axis=1)
            si_ref[c, pp] = jnp.where(pair_mask, x[:, ng * HG_PAIR:], 0.0)
            yield "products"
    for c in range(n):
        rws = slice(c * HG_CHUNK, (c + 1) * HG_CHUNK)
        outs = [lax.dot_general(qb_ref[pp, rws, :], d_ref[c, pp], _NT, preferred_element_type=F32)
                for pp in range(pairs)]
        vc = v[rws]
        vbd = jnp.where(head_mask, jnp.concatenate([vc] * (w // HG_CHUNK), axis=0), jnp.zeros((), vc.dtype))
        o_ref[row0 + c * HG_CHUNK:row0 + (c + 1) * HG_CHUNK, :] = jnp.concatenate(outs, axis=1) + _dot(ad[rws], vbd)
        yield "products"


def _run_interleaved(first, second):
    live = [first, second]
    while live:
        for g in list(live):
            if next(g, None) is None:
                live.remove(g)


def _hg_step(qs_ref, si_ref, dl_ref, st_ref, o_ref, r0, g0, ci):
    outs = []
    for pp in range(st_ref.shape[0]):
        st = st_ref[pp]
        outs.append(lax.dot_general(qs_ref[pp, pl.ds(r0, HG_CHUNK), :], st.astype(BF16), _NT,
                                    preferred_element_type=F32))
        st_ref[pp] = st * dl_ref[pp, pl.ds(ci, 1), :] + si_ref[ci, pp]
    o_ref[pl.ds(g0, HG_CHUNK), :] += jnp.concatenate(outs, axis=1)


def _hgrn_kernel(hq_ref, hi_ref, hit_ref, hf_ref, hg_ref, hqc_ref, hic_ref, hitc_ref, hfc_ref, hgc_ref,
                 lb_ref, gn_ref, tri_ref, sel_ref, bd_ref, *rest, need_ctx):
    if need_ctx:
        rec_ref, recc_ref = rest[0], rest[1]
        scr = rest[2:]
    else:
        rec_ref, recc_ref = rest[0], None
        scr = rest[1:]
    of_ref, ob_ref, ofc_ref, obc_ref, stf_ref, stb_ref = scr[:6]
    fwd, bwd = scr[6:12], scr[12:18]
    w = hq_ref.shape[2]
    lb_f = lb_ref[0]
    lb_b = lb_ref[1]
    stf_ref[...] = jnp.zeros_like(stf_ref)
    stb_ref[...] = jnp.zeros_like(stb_ref)

    def run(hq, hi, hit, hf, o_f, o_b):
        t = hq.shape[1]
        tile = min(HG_TILE, t)
        nt = t // tile
        cpt = tile // HG_CHUNK
        for s in range(nt):
            rf = s * tile
            rb = (nt - 1 - s) * tile
            _run_interleaved(
                _hg_prepare(False, hq[0, rf:rf + tile, :], hf[0, rf:rf + tile, 0:w], hi[0, rf:rf + tile, :], hit,
                            rf // HG_CHUNK, lb_f, tri_ref.at[0], sel_ref, o_f, rf, *fwd),
                _hg_prepare(True, hq[0, rb:rb + tile, :], hf[0, rb:rb + tile, w:2 * w], hi[0, rb:rb + tile, :], hit,
                            rb // HG_CHUNK, lb_b, tri_ref.at[1], sel_ref, o_b, rb, *bwd))

            def body(i, carry):
                cf = i
                cb = cpt - 1 - i
                r0f = pl.multiple_of(cf * HG_CHUNK, HG_CHUNK)
                r0b = pl.multiple_of(cb * HG_CHUNK, HG_CHUNK)
                gf = pl.multiple_of(rf + cf * HG_CHUNK, HG_CHUNK)
                gb = pl.multiple_of(rb + cb * HG_CHUNK, HG_CHUNK)
                _hg_step(fwd[1], fwd[3], fwd[4], stf_ref, o_f, r0f, gf, cf)
                _hg_step(bwd[1], bwd[3], bwd[4], stb_ref, o_b, r0b, gb, cb)
                return carry

            lax.fori_loop(0, cpt, body, 0)

    def finish(o_f, o_b, hg, out):
        t = hg.shape[1]
        tile = min(HG_TILE, t)
        for s in range(t // tile):
            rows = slice(s * tile, (s + 1) * tile)
            o = o_f[rows, :] + o_b[rows, :]
            ms = _dot((o * o).astype(BF16), bd_ref[...])
            on = o * lax.rsqrt(ms + EPS) * gn_ref[...]
            out[0, rows, :] = (on * _silu(hg[0, rows, :].astype(F32))).astype(BF16)

    run(hqc_ref, hic_ref, hitc_ref, hfc_ref, ofc_ref, obc_ref)
    run(hq_ref, hi_ref, hit_ref, hf_ref, of_ref, ob_ref)
    finish(of_ref, ob_ref, hg_ref, rec_ref)
    if need_ctx:
        finish(ofc_ref, obc_ref, hgc_ref, recc_ref)


def _hgrn(lat, cx, lb, gn, tri, sel, bd, *, need_ctx):
    hq, hi, hit, hf, hg = lat
    hqc, hic, hitc, hfc, hgc = cx
    b, t, w = hq.shape
    tc = hqc.shape[1]
    tile = min(HG_TILE, t)

    def seq(a):
        nd = a.ndim
        return pl.BlockSpec((1,) + a.shape[1:], lambda bi: (bi,) + (0,) * (nd - 1))

    out_shape = [jax.ShapeDtypeStruct((b, t, w), BF16)]
    out_specs = [pl.BlockSpec((1, t, w), lambda bi: (bi, 0, 0))]
    if need_ctx:
        out_shape.append(jax.ShapeDtypeStruct((b, tc, w), BF16))
        out_specs.append(pl.BlockSpec((1, tc, w), lambda bi: (bi, 0, 0)))
    pairs = w // HG_PAIR
    cpt = tile // HG_CHUNK
    dir_scratch = [
        pltpu.VMEM((pairs, tile, (HG_NB - 1) * HG_PAIR), BF16),
        pltpu.VMEM((pairs, tile, HG_PAIR), BF16),
        pltpu.VMEM((pairs, tile, HG_NB * HG_PAIR), BF16),
        pltpu.VMEM((cpt, pairs, HG_PAIR, HG_PAIR), F32),
        pltpu.VMEM((pairs, cpt, HG_PAIR), F32),
        pltpu.VMEM((cpt, pairs, HG_PAIR, (HG_NB - 1) * HG_PAIR), BF16),
    ]
    res = pl.pallas_call(
        functools.partial(_hgrn_kernel, need_ctx=need_ctx),
        out_shape=tuple(out_shape),
        grid=(b,),
        in_specs=[seq(hq), seq(hi), seq(hit), seq(hf), seq(hg), seq(hqc), seq(hic), seq(hitc), seq(hfc), seq(hgc),
                  _full(lb.shape), _full(gn.shape), _full(tri.shape), _full(sel.shape), _full(bd.shape)],
        out_specs=tuple(out_specs),
        scratch_shapes=[pltpu.VMEM((t, w), F32), pltpu.VMEM((t, w), F32),
                        pltpu.VMEM((tc, w), F32), pltpu.VMEM((tc, w), F32),
                        pltpu.VMEM((pairs, HG_PAIR, HG_PAIR), F32), pltpu.VMEM((pairs, HG_PAIR, HG_PAIR), F32)]
        + dir_scratch + dir_scratch,
        compiler_params=_cparams("parallel"),
        name="hgrn",
    )(hq, hi, hit, hf, hg, hqc, hic, hitc, hfc, hgc, lb, gn, tri, sel, bd)
    return res if need_ctx else (res[0], None)


def _dft_kernel(cos_ref, msin_ref, ab_ref, w_ref, o_ref, z_scr):
    tf = cos_ref.shape[0]
    nsub = 2 if tf % (2 * BF16_ROWS) == 0 else 1
    sub = tf // nsub
    for step in range(nsub + 1):
        if step < nsub:
            rows = slice(step * sub, (step + 1) * sub)
            z_scr[rows, :] = (_dot(cos_ref[rows, :], ab_ref[0, 0]) + _dot(msin_ref[rows, :], ab_ref[0, 1])).astype(BF16)
        if step >= 1:
            rows = slice((step - 1) * sub, step * sub)
            o_ref[0, rows, :] = _dot(z_scr[rows, :], w_ref[...]).astype(BF16)


def _dft(cs, fab, wbd, *, tf):
    b, _, t, w = fab.shape
    return pl.pallas_call(
        _dft_kernel,
        out_shape=jax.ShapeDtypeStruct((b, t, w), BF16),
        grid=(t // tf, b),
        in_specs=[
            pl.BlockSpec((tf, t), lambda i, bi: (i, 0)),
            pl.BlockSpec((tf, t), lambda i, bi: (i, 0)),
            pl.BlockSpec((1, 2, t, w), lambda i, bi: (bi, 0, 0, 0)),
            _full(wbd.shape),
        ],
        out_specs=pl.BlockSpec((1, tf, w), lambda i, bi: (bi, i, 0)),
        scratch_shapes=[pltpu.VMEM((tf, w), BF16)],
        compiler_params=_cparams("parallel", "parallel"),
        name="dft",
    )(cs[0], cs[1], fab, wbd)


def _outproj_kernel(a_ref, r_ref, f_ref, x_ref, g1_ref, w_ref, lg_ref, lb_ref, o_ref, mix_scr, *, alpha):
    aw = a_ref.shape[2]
    rw = r_ref.shape[2]
    tm = x_ref.shape[1]
    nsub = 2 if tm % (2 * BF16_ROWS) == 0 else 1
    sub = tm // nsub

    def mix(i):
        rows = slice(i * sub, (i + 1) * sub)
        mix_scr[rows, :] = (_dot(a_ref[0, rows, :], w_ref[0:aw, :]) + _dot(r_ref[0, rows, :], w_ref[aw:aw + rw, :])
                            + _dot(f_ref[0, rows, :], w_ref[aw + rw:, :]))

    def norm(i):
        rows = slice(i * sub, (i + 1) * sub)
        o_ref[0, rows, :] = _layer_norm(alpha * x_ref[0, rows, :] + g1_ref[0] * mix_scr[rows, :],
                                        lg_ref[...], lb_ref[...])

    for step in range(nsub + 1):
        if step < nsub:
            mix(step)
        if step >= 1:
            norm(step - 1)


def _outproj(attn, rec, four, x, mod3, mod_row, l, w_out, lg, lb, *, alpha, tm):
    b, t, d = x.shape
    bt = lambda a: pl.BlockSpec((1, tm, a.shape[2]), lambda bi, i: (bi, i, 0))
    return pl.pallas_call(
        functools.partial(_outproj_kernel, alpha=alpha),
        out_shape=jax.ShapeDtypeStruct((b, t, d), F32),
        grid=(b, t // tm),
        in_specs=[bt(attn), bt(rec), bt(four), bt(x),
                  pl.BlockSpec((1, 1, d), lambda bi, i: (mod_row(bi), 0, 2)),
                  _layer(w_out, l), _full(lg.shape), _full(lb.shape)],
        out_specs=pl.BlockSpec((1, tm, d), lambda bi, i: (bi, i, 0)),
        scratch_shapes=[pltpu.VMEM((tm, d), F32)],
        compiler_params=_cparams("parallel", "parallel"),
        name="outproj",
    )(attn, rec, four, x, mod3, w_out, lg, lb)


def _ffn_kernel(x_ref, xp_ref, xn_ref, sh_ref, sc_ref, g2_ref, wup_ref, cw_ref, cb_ref, wdn_ref, lg_ref, lb_ref,
                o_ref, h_scr, u_scr, act_scr, y_scr, *, alpha):
    i = pl.program_id(1)
    last = pl.num_programs(1) - 1
    tm = x_ref.shape[1]
    halo = BF16_ROWS
    sc = sc_ref[0]
    sh = sh_ref[0]

    def mod(v):
        return (v * (1.0 + sc) + sh).astype(BF16)

    h_scr[0:halo, :] = jnp.where(i > 0, mod(xp_ref[0]), jnp.zeros((), BF16))
    h_scr[halo:halo + tm, :] = mod(x_ref[0])
    h_scr[halo + tm:, :] = jnp.where(i < last, mod(xn_ref[0]), jnp.zeros((), BF16))
    d_ff = wdn_ref.shape[0]
    cf = u_scr.shape[3]

    def conv(ub, cols):
        cw = cw_ref[:, cols]
        u = ub[...]
        rows = u.shape[0]
        prev = pltpu.roll(u, 1, 0)[halo:halo + tm]
        nxt = pltpu.roll(u, rows - 1, 0)[halo:halo + tm]
        return prev * cw[0:1, :] + u[halo:halo + tm] * cw[1:2, :] + nxt * cw[2:3, :] + cb_ref[:, cols]

    for j in range(d_ff // cf):
        val_cols = slice(j * cf, (j + 1) * cf)
        gate_cols = slice(d_ff + j * cf, d_ff + (j + 1) * cf)
        uv, ug = u_scr.at[j % 2, 0], u_scr.at[j % 2, 1]
        uv[...] = _dot(h_scr[...], wup_ref[:, val_cols])
        ug[...] = _dot(h_scr[...], wup_ref[:, gate_cols])
        act_scr[:, val_cols] = (conv(uv, val_cols) * _silu(conv(ug, gate_cols))).astype(BF16)
    nsub = 2 if tm % (2 * BF16_ROWS) == 0 else 1
    sub = tm // nsub
    for step in range(nsub + 1):
        if step < nsub:
            rows = slice(step * sub, (step + 1) * sub)
            y_scr[rows, :] = _dot(act_scr[rows, :], wdn_ref[...])
        if step >= 1:
            rows = slice((step - 1) * sub, step * sub)
            o_ref[0, rows, :] = _layer_norm(alpha * x_ref[0, rows, :] + g2_ref[0] * y_scr[rows, :],
                                            lg_ref[...], lb_ref[...])


def _ffn(x1, mod3, mod_row, l, wup, cw, cb, wdn, lg, lb, *, alpha, tm, cf):
    b, t, d = x1.shape
    halo = BF16_ROWS
    per = tm // halo
    nblk = t // halo
    return pl.pallas_call(
        functools.partial(_ffn_kernel, alpha=alpha),
        out_shape=jax.ShapeDtypeStruct((b, t, d), F32),
        grid=(b, t // tm),
        in_specs=[
            pl.BlockSpec((1, tm, d), lambda bi, i: (bi, i, 0)),
            pl.BlockSpec((1, halo, d), lambda bi, i: (bi, jnp.maximum(i * per - 1, 0), 0)),
            pl.BlockSpec((1, halo, d), lambda bi, i: (bi, jnp.minimum((i + 1) * per, nblk - 1), 0)),
            pl.BlockSpec((1, 1, d), lambda bi, i: (mod_row(bi), 0, 3)),
            pl.BlockSpec((1, 1, d), lambda bi, i: (mod_row(bi), 0, 4)),
            pl.BlockSpec((1, 1, d), lambda bi, i: (mod_row(bi), 0, 5)),
            _layer(wup, l, 1), _full(cw.shape), _full(cb.shape), _layer(wdn, l, 1), _full(lg.shape),
            _full(lb.shape),
        ],
        out_specs=pl.BlockSpec((1, tm, d), lambda bi, i: (bi, i, 0)),
        scratch_shapes=[pltpu.VMEM((tm + 2 * halo, d), BF16), pltpu.VMEM((2, 2, tm + 2 * halo, cf), F32),
                        pltpu.VMEM((tm, wdn.shape[1]), BF16), pltpu.VMEM((tm, d), F32)],
        compiler_params=_cparams("parallel", "arbitrary"),
        name="ffn",
    )(x1, x1, x1, mod3, mod3, mod3, wup, cw, cb, wdn, lg, lb)


def _block_diag_mean(width, seg):
    i = np.arange(width)
    return jnp.asarray(((i[:, None] // seg) == (i[None, :] // seg)).astype(np.float32) / seg, dtype=BF16)


def _hg_tri(n):
    i = np.arange(n)
    same = (i[:, None] // HG_CHUNK) == (i[None, :] // HG_CHUNK)
    lower = same & (i[None, :] <= i[:, None])
    upper = same & (i[None, :] >= i[:, None])
    return jnp.asarray(np.stack([lower, upper]).astype(np.float32), dtype=BF16)


def _hg_select(w):
    r = np.arange(HG_BLOCK * w)
    c = np.arange(w)
    sg, head_r = r // w, (r % w) // HG_DIM
    head_c, s = c // HG_DIM, c % HG_DIM
    m = (head_r[:, None] == head_c[None, :]) & ((s[None, :] % HG_BLOCK) == sg[:, None])
    return jnp.asarray(m.astype(np.float32), dtype=BF16)


def _position_dft(t):
    lo = 64
    f = jnp.arange(t, dtype=jnp.int32)[:, None]
    step = 2.0 * np.pi / t
    a = ((f * (lo * jnp.arange(t // lo, dtype=jnp.int32))[None, :]) % t).astype(F32) * step
    bb = ((f * jnp.arange(lo, dtype=jnp.int32)[None, :]) % t).astype(F32) * step
    ca, sa = jnp.cos(a)[:, :, None], jnp.sin(a)[:, :, None]
    cb, sb = jnp.cos(bb)[:, None, :], jnp.sin(bb)[:, None, :]
    norm = 1.0 / np.sqrt(t)
    cos = ((ca * cb - sa * sb) * norm).astype(BF16).reshape(t, t)
    msin = ((sa * cb + ca * sb) * -norm).astype(BF16).reshape(t, t)
    return cos, msin


def _channel_dft(groups):
    c = np.arange(FT_DIM, dtype=np.int64)
    angc = 2.0 * np.pi * ((c[:, None] * c[None, :]) % FT_DIM).astype(np.float64) / FT_DIM
    eye = np.eye(groups)
    dftc = np.concatenate([np.kron(eye, np.cos(angc)), np.kron(eye, np.sin(angc))], axis=1) / np.sqrt(FT_DIM)
    return jnp.asarray(dftc, dtype=F32).astype(BF16)


def _rope_tables(t):
    rows = t // GRID_W
    row = jnp.repeat(jnp.arange(rows, dtype=jnp.int32), GRID_W)
    col = jnp.tile(jnp.arange(GRID_W, dtype=jnp.int32), rows)
    nf = HEAD_DIM // 4
    inv = ROPE_THETA ** (-jnp.arange(nf, dtype=F32) / nf)
    ar = row.astype(F32)[:, None] * inv
    ac = col.astype(F32)[:, None] * inv
    cos = jnp.concatenate([jnp.cos(ar)] * 2 + [jnp.cos(ac)] * 2, axis=1)
    sin = jnp.concatenate([-jnp.sin(ar), jnp.sin(ar), -jnp.sin(ac), jnp.sin(ac)], axis=1)
    return jnp.tile(cos, (1, 128 // HEAD_DIM)), jnp.tile(sin, (1, 128 // HEAD_DIM))


def kernel(x, c, ctx, c_ctx, w_ada, b_ada, w_in, q_norm, k_norm, hg_lb, hg_norm, ft_w, w_out, ln1_g, ln1_b, w_up,
           conv_w, conv_b, w_down, ln2_g, ln2_b):
    b, t, d = x.shape
    tc = ctx.shape[1]
    depth = w_ada.shape[0]
    d_ff = w_down.shape[1]
    hgw = hg_lb.shape[2]
    ftw = ft_w.shape[1] * ft_w.shape[2]
    kvw = ATTN_KV_HEADS * HEAD_DIM
    aw = w_in.shape[2] - 2 * kvw - 5 * hgw - ftw
    alpha = (2 * depth) ** 0.25

    sizes = [("q", aw), ("k", kvw), ("v", kvw), ("hq", hgw), ("hi", hgw), ("hf", 2 * hgw), ("hg", hgw), ("ft", ftw)]
    offs, lo = {}, 0
    for name, size in sizes:
        offs[name] = (lo, lo + size)
        lo += size

    mod_rows = -(-(b + 1) // 8) * 8
    cond = jnp.concatenate([c, c_ctx[None, :], jnp.zeros((mod_rows - b - 1, d), F32)], axis=0)
    mod = _ada(cond, w_ada, b_ada)
    mod3 = mod.reshape(depth * mod_rows, 1, 6 * d)

    lb_soft = jax.nn.softmax(hg_lb.astype(F32), axis=0)
    lower = (jnp.cumsum(lb_soft, axis=0) - lb_soft[:1]).reshape(depth, 2, 1, hgw)

    cos, sin = _rope_tables(t)
    bdq = _block_diag_mean(2 * 128, HEAD_DIM)
    bdk = _block_diag_mean(kvw, HEAD_DIM)
    bdh = _block_diag_mean(hgw, HG_DIM)
    tri = _hg_tri(min(HG_TILE, t))
    sel = _hg_select(hgw)
    dftc = _channel_dft(ftw // FT_DIM)
    cs_lat = _position_dft(t)
    cs_ctx = _position_dft(tc)

    cf = 256
    assert d_ff % cf == 0
    w_in_b, w_out_b, w_up_b, w_down_b = (a.astype(BF16) for a in (w_in, w_out, w_up, w_down))

    for l in range(depth):
        need_ctx = l < depth - 1
        lat_row = lambda bi, l=l: l * mod_rows + bi
        ctx_row = lambda bi, l=l: l * mod_rows + b

        qg = jnp.tile(q_norm[l].astype(F32), aw // HEAD_DIM)[None, :]
        kg = jnp.tile(k_norm[l].astype(F32), kvw // HEAD_DIM)[None, :]
        proj_args = (w_in_b, cos, sin, qg, kg, bdq, bdk, dftc)
        q, k, v, hq, hi, hg, hf, hit, fab = _inproj(x, mod3, lat_row, l, *proj_args, offs=offs, rope=True, tm=512)
        qc, kc, vc, hqc, hic, hgc, hfc, hitc, fabc = _inproj(ctx, mod3, ctx_row, l, *proj_args, offs=offs,
                                                               rope=False, tm=tc)

        attn = _attention(q, jnp.concatenate([kc, k], axis=3), jnp.concatenate([vc, v], axis=2), tq=256)

        gn = jnp.tile(hg_norm[l].astype(F32), hgw // HG_DIM)[None, :]
        rec, rec_c = _hgrn((hq, hi, hit, hf, hg), (hqc, hic, hitc, hfc, hgc), lower[l], gn, tri, sel, bdh,
                           need_ctx=need_ctx)

        wbd = jax.scipy.linalg.block_diag(*[ft_w[l, g] for g in range(ft_w.shape[1])]).astype(BF16)
        four = _dft(cs_lat, fab, wbd, tf=512)

        lg1, lb1 = ln1_g[l][None, :], ln1_b[l][None, :]
        lg2, lb2 = ln2_g[l][None, :], ln2_b[l][None, :]
        ffn_args = (l, w_up_b, conv_w[l], conv_b[l][None, :], w_down_b, lg2, lb2)

        x1 = _outproj(attn, rec, four, x, mod3, lat_row, l, w_out_b, lg1, lb1, alpha=alpha, tm=512)
        x_next = _ffn(x1, mod3, lat_row, *ffn_args, alpha=alpha, tm=512, cf=cf)

        if need_ctx:
            attn_c = _attention(qc, kc, vc, tq=tc)
            four_c = _dft(cs_ctx, fabc, wbd, tf=tc)
            ctx1 = _outproj(attn_c, rec_c, four_c, ctx, mod3, ctx_row, l, w_out_b, lg1, lb1, alpha=alpha, tm=tc)
            ctx = _ffn(ctx1, mod3, ctx_row, *ffn_args, alpha=alpha, tm=tc, cf=cf)
        x = x_next

    return x
```

```python
import functools

import numpy as np
import jax
import jax.numpy as jnp
from jax import lax
from jax.experimental import pallas as pl
from jax.experimental.pallas import tpu as pltpu

F32 = jnp.float32
BF16 = jnp.bfloat16

GRID_W = 64
HEAD_DIM = 64
ATTN_KV_HEADS = 2
HG_DIM = 64
FT_DIM = 64
ROPE_THETA = 10000.0
F_MIN = 1e-30
EPS = 1e-6
CONV_W = 3

HG_CHUNK = 64
HG_BLOCK = 16
HG_NB = HG_CHUNK // HG_BLOCK
HG_PAIR = 2 * HG_DIM
HG_TILE = 512
NEG = -1e30
LOG2_E = 1.4426950408889634

V7X_VMEM_BYTES = 64 * 1024 * 1024
VMEM_LIMIT = int(V7X_VMEM_BYTES * 0.85)
BF16_ROWS = 16

_NT = (((1,), (1,)), ((), ()))


def _cparams(*sem):
    return pltpu.CompilerParams(dimension_semantics=sem, vmem_limit_bytes=VMEM_LIMIT)


def _dot(a, b):
    return jnp.dot(a, b, preferred_element_type=F32)


def _silu(x):
    return x * jax.nn.sigmoid(x)


def _layer_norm(y, g, b):
    mu = jnp.mean(y, axis=-1, keepdims=True)
    d = y - mu
    var = jnp.mean(d * d, axis=-1, keepdims=True)
    return d * lax.rsqrt(var + EPS) * g + b


def _full(shape):
    n = len(shape)
    return pl.BlockSpec(shape, lambda *_: (0,) * n)


def _layer(w, l, buffers=2):
    n = w.ndim - 1
    return pl.BlockSpec((None,) + w.shape[1:], lambda *_: (l,) + (0,) * n, pipeline_mode=pl.Buffered(buffers))


def _ada_kernel(c_ref, w_ref, b_ref, o_ref):
    s = _silu(c_ref[...]).astype(BF16)
    o_ref[0] = _dot(s, w_ref[0].astype(BF16)) + b_ref[0]


def _ada(cond, w_ada, b_ada):
    depth, d, n = w_ada.shape
    rows = cond.shape[0]
    tn = n // 4
    return pl.pallas_call(
        _ada_kernel,
        out_shape=jax.ShapeDtypeStruct((depth, rows, n), F32),
        grid=(depth, n // tn),
        in_specs=[
            pl.BlockSpec((rows, d), lambda l, j: (0, 0)),
            pl.BlockSpec((1, d, tn), lambda l, j: (l, 0, j)),
            pl.BlockSpec((1, 1, tn), lambda l, j: (l, 0, j)),
        ],
        out_specs=pl.BlockSpec((1, rows, tn), lambda l, j: (l, 0, j)),
        compiler_params=_cparams("arbitrary", "arbitrary"),
        name="ada",
    )(cond, w_ada, b_ada.reshape(depth, 1, n))


def _rope128(y, cos, sin_signed):
    lane = lax.broadcasted_iota(jnp.int32, y.shape, 1)
    upper = (lane % 32) >= 16
    partner = jnp.where(upper, pltpu.roll(y, 16, 1), pltpu.roll(y, 112, 1))
    return y * cos + partner * sin_signed


def _inproj_kernel(x_ref, sh_ref, sc_ref, w_ref, cos_ref, sin_ref, qg_ref, kg_ref, bdq_ref, bdk_ref,
                   dftc_ref, q_ref, k_ref, v_ref, hq_ref, hi_ref, hg_ref, hf_ref, hit_ref, fab_ref, h_scr, qk_scr,
                   *, rope, offs):
    h_scr[...] = (x_ref[0] * (1.0 + sc_ref[0]) + sh_ref[0]).astype(BF16)

    def proj(name):
        lo, hi = offs[name]
        return _dot(h_scr[...], w_ref[:, lo:hi])

    q_lo, q_hi = offs["q"]
    k_lo, k_hi = offs["k"]

    def norm_rope(lo, hi, bd_ref, g_ref):
        base = q_lo if lo < q_hi else k_lo
        y = qk_scr[:, lo - q_lo:hi - q_lo]
        y = y * lax.rsqrt(_dot((y * y).astype(BF16), bd_ref[...]) + EPS) * g_ref[:, lo - base:hi - base]
        if rope:
            cos, sin = cos_ref[...], sin_ref[...]
            y = jnp.concatenate([_rope128(y[:, c:c + 128], cos, sin) for c in range(0, hi - lo, 128)], axis=1)
        return y

    q_scale = HEAD_DIM ** -0.5 * LOG2_E

    def q_part(lo, hi):
        q_ref[0, :, lo - q_lo:hi - q_lo] = (norm_rope(lo, hi, bdq_ref, qg_ref) * q_scale).astype(BF16)

    qk_scr[...] = _dot(h_scr[...], w_ref[:, q_lo:k_hi])
    v = proj("v")
    lane = lax.broadcasted_iota(jnp.int32, v.shape, 1)
    for hh in range(v.shape[1] // HEAD_DIM):
        vh = v if hh == 0 else pltpu.roll(v, v.shape[1] - hh * HEAD_DIM, 1)
        v_ref[0, hh] = jnp.where(lane < HEAD_DIM, vh, jnp.where(lane == HEAD_DIM, 1.0, 0.0)).astype(BF16)
    hq_ref[0] = proj("hq").astype(BF16)
    qw = bdq_ref.shape[0]
    q_part(q_lo, q_lo + qw)
    hi = proj("hi")
    hi_ref[0] = hi.astype(BF16)
    vt = hi.T.astype(BF16)
    for j in range(vt.shape[1] // HG_CHUNK):
        hit_ref[0, j] = vt[:, j * HG_CHUNK:(j + 1) * HG_CHUNK]
    for lo in range(q_lo + qw, q_hi, qw):
        q_part(lo, lo + qw)
    hg_ref[0] = proj("hg").astype(BF16)
    kt = norm_rope(k_lo, k_hi, bdk_ref, kg_ref).T.astype(BF16)
    for hh in range(kt.shape[0] // HEAD_DIM):
        k_ref[0, hh] = kt[hh * HEAD_DIM:(hh + 1) * HEAD_DIM, :]
    hf_ref[0] = proj("hf")
    ab = _dot(proj("ft").astype(BF16), dftc_ref[...])
    half = ab.shape[1] // 2
    fab_ref[0, 0] = ab[:, :half].astype(BF16)
    fab_ref[0, 1] = ab[:, half:].astype(BF16)


def _inproj(x, mod3, mod_row, l, w_in, cos, sin, qg, kg, bdq, bdk, dftc, *, offs, rope, tm):
    b, t, d = x.shape
    aw = offs["q"][1] - offs["q"][0]
    kw = offs["k"][1] - offs["k"][0]
    hw = offs["hq"][1] - offs["hq"][0]
    fw = offs["ft"][1] - offs["ft"][0]
    row = lambda bi: mod_row(bi)
    bt = lambda width: pl.BlockSpec((1, tm, width), lambda bi, i: (bi, i, 0))
    kvh = kw // HEAD_DIM
    out_shapes = (
        jax.ShapeDtypeStruct((b, t, aw), BF16), jax.ShapeDtypeStruct((b, kvh, HEAD_DIM, t), BF16),
        jax.ShapeDtypeStruct((b, kvh, t, 2 * HEAD_DIM), BF16), jax.ShapeDtypeStruct((b, t, hw), BF16),
        jax.ShapeDtypeStruct((b, t, hw), BF16), jax.ShapeDtypeStruct((b, t, hw), BF16),
        jax.ShapeDtypeStruct((b, t, 2 * hw), F32),
        jax.ShapeDtypeStruct((b, t // HG_CHUNK, hw, HG_CHUNK), BF16),
        jax.ShapeDtypeStruct((b, 2, t, fw), BF16),
    )
    out_specs = (
        bt(aw),
        pl.BlockSpec((1, kvh, HEAD_DIM, tm), lambda bi, i: (bi, 0, 0, i)),
        pl.BlockSpec((1, kvh, tm, 2 * HEAD_DIM), lambda bi, i: (bi, 0, i, 0)),
        bt(hw), bt(hw), bt(hw), bt(2 * hw),
        pl.BlockSpec((1, tm // HG_CHUNK, hw, HG_CHUNK), lambda bi, i: (bi, i, 0, 0)),
        pl.BlockSpec((1, 2, tm, fw), lambda bi, i: (bi, 0, i, 0)),
    )
    return pl.pallas_call(
        functools.partial(_inproj_kernel, rope=rope, offs=offs),
        out_shape=out_shapes,
        grid=(b, t // tm),
        in_specs=[
            pl.BlockSpec((1, tm, d), lambda bi, i: (bi, i, 0)),
            pl.BlockSpec((1, 1, d), lambda bi, i: (row(bi), 0, 0)),
            pl.BlockSpec((1, 1, d), lambda bi, i: (row(bi), 0, 1)),
            _layer(w_in, l),
            pl.BlockSpec((tm, 128), lambda bi, i: (i, 0)),
            pl.BlockSpec((tm, 128), lambda bi, i: (i, 0)),
            _full(qg.shape), _full(kg.shape), _full(bdq.shape), _full(bdk.shape), _full(dftc.shape),
        ],
        out_specs=out_specs,
        scratch_shapes=[pltpu.VMEM((tm, d), BF16), pltpu.VMEM((tm, aw + kw), F32)],
        compiler_params=_cparams("parallel", "parallel"),
        name="inproj",
    )(x, mod3, mod3, w_in, cos, sin, qg, kg, bdq, bdk, dftc)


def _attn_kernel(q_ref, kt_ref, v_ref, o_ref, s_scr, m_scr, p_scr, *, group):
    kt = kt_ref[0, 0]
    v = v_ref[0, 0]
    half = (v.shape[0] // 2) // 256 * 256

    def scores(h):
        s = _dot(q_ref[0, :, h * HEAD_DIM:(h + 1) * HEAD_DIM], kt)
        s_scr[h] = s
        m_scr[h] = jnp.max(s, axis=-1, keepdims=True)

    def probs(h):
        p_scr[h] = jnp.exp2(s_scr[h] - m_scr[h]).astype(BF16)

    def values(h):
        if half:
            pv = _dot(p_scr[h, :, :half], v[:half]) + _dot(p_scr[h, :, half:], v[half:])
        else:
            pv = _dot(p_scr[h], v)
        return pv[:, :HEAD_DIM] / pv[:, HEAD_DIM:HEAD_DIM + 1]

    outs = [None] * group
    for step in range(group + 2):
        if step < group:
            scores(step)
        if 0 <= step - 1 < group:
            probs(step - 1)
        if 0 <= step - 2 < group:
            outs[step - 2] = values(step - 2)
    o_ref[0] = jnp.concatenate(outs, axis=1).astype(BF16)


def _attention(q, kt_all, v_all, *, tq):
    b, t, aw = q.shape
    kvh, s = v_all.shape[1], v_all.shape[2]
    group = aw // HEAD_DIM // kvh
    gw = group * HEAD_DIM
    return pl.pallas_call(
        functools.partial(_attn_kernel, group=group),
        out_shape=jax.ShapeDtypeStruct((b, t, aw), BF16),
        grid=(b, kvh, t // tq),
        in_specs=[
            pl.BlockSpec((1, tq, gw), lambda bi, g, i: (bi, i, g)),
            pl.BlockSpec((1, 1, HEAD_DIM, s), lambda bi, g, i: (bi, g, 0, 0)),
            pl.BlockSpec((1, 1, s, 2 * HEAD_DIM), lambda bi, g, i: (bi, g, 0, 0)),
        ],
        out_specs=pl.BlockSpec((1, tq, gw), lambda bi, g, i: (bi, i, g)),
        scratch_shapes=[pltpu.VMEM((group, tq, s), F32), pltpu.VMEM((group, tq, 1), F32),
                        pltpu.VMEM((group, tq, s), BF16)],
        compiler_params=_cparams("parallel", "parallel", "parallel"),
        name="attention",
    )(q, kt_all, v_all)


def _hg_prepare(rev, hq, z, v, hit_ref, c0, lb, tri_ref, sel_ref, o_ref, row0, qb_ref, qs_ref, ks_ref, si_ref, dl_ref,
                d_ref):
    r, w = z.shape
    n = r // HG_CHUNK
    f = lb + (1.0 - lb) * jax.nn.sigmoid(z)
    logf = jnp.log2(jnp.maximum(f, F_MIN))
    kk = 1.0 - f
    qs = _silu(hq.astype(F32))
    hi = logf.astype(BF16)
    lo = (logf - hi.astype(F32)).astype(BF16)
    tri = tri_ref[0:r, 0:r]
    cum = _dot(tri, hi) + _dot(tri, lo)
    yield "local"

    c3 = cum.reshape(n, HG_CHUNK, w)
    q3 = qs.reshape(n, HG_CHUNK, w)
    k3 = kk.reshape(n, HG_CHUNK, w)

    def rows(a, lo_, hi_):
        parts = []
        if lo_ > 0:
            parts.append(jnp.zeros((n, lo_, w), F32))
        parts.append(a)
        if hi_ < HG_CHUNK:
            parts.append(jnp.zeros((n, HG_CHUNK - hi_, w), F32))
        return jnp.concatenate(parts, axis=1) if len(parts) > 1 else a

    e_last = c3[:, 0:1, :] if rev else c3[:, HG_CHUNK - 1:HG_CHUNK, :]
    qparts, kparts = [], []
    for g in range(1, HG_NB):
        j = HG_NB - g if rev else g - 1
        b0, b1 = j * HG_BLOCK, (j + 1) * HG_BLOCK
        e = c3[:, b0:b0 + 1, :] if rev else c3[:, b1 - 1:b1, :]
        p0, p1 = (0, b0) if rev else (b1, HG_CHUNK)
        qparts.append(rows(q3[:, p0:p1, :] * jnp.exp2(c3[:, p0:p1, :] - e), p0, p1))
        kparts.append(rows(k3[:, b0:b1, :] * jnp.exp2(e - c3[:, b0:b1, :]), b0, b1))
    qparts.append(q3 * jnp.exp2(c3))
    kparts.append(k3 * jnp.exp2(e_last - c3))
    dl = jnp.exp2(e_last).reshape(n, w)
    pairs = w // HG_PAIR
    ng = HG_NB - 1
    for pp in range(pairs):
        lanes = slice(pp * HG_PAIR, (pp + 1) * HG_PAIR)
        qb_ref[pp, 0:r, :] = jnp.concatenate([p[:, :, lanes] for p in qparts[:ng]], axis=-1).reshape(
            r, ng * HG_PAIR).astype(BF16)
        qs_ref[pp, 0:r, :] = qparts[ng][:, :, lanes].reshape(r, HG_PAIR).astype(BF16)
        ks_ref[pp, 0:r, :] = jnp.concatenate([p[:, :, lanes] for p in kparts], axis=-1).reshape(
            r, HG_NB * HG_PAIR).astype(BF16)
        dl_ref[pp, 0:n, :] = dl[:, lanes]
        yield "local"

    nb = r // HG_BLOCK
    hb = HG_BLOCK // 2
    c5 = cum.reshape(nb, 2, hb, w)
    q5 = qs.reshape(nb, 2, hb, w)
    d5 = (cum - jnp.log2(jnp.maximum(kk, 0.0))).reshape(nb, 2, hb, w)
    tau = lax.broadcasted_iota(jnp.int32, (nb, hb, w), 1)
    halves = []
    for half in range(2):
        sources = [sg for sg in range(HG_BLOCK) if not ((half > sg // hb) if rev else (half < sg // hb))]
        eparts = []
        for sg in sources:
            sh, sl = sg // hb, sg % hb
            arg = c5[:, half] - d5[:, sh, sl:sl + 1, :]
            if half == sh:
                arg = jnp.where((tau <= sl) if rev else (tau >= sl), arg, NEG)
            eparts.append((jnp.exp2(arg) * q5[:, half]).reshape(nb * hb, w).astype(BF16))
            if len(eparts) % 2 == 0:
                yield "local"
        lo_ = sources[0] * w
        halves.append(_dot(jnp.concatenate(eparts, axis=-1), sel_ref[lo_:lo_ + len(sources) * w, :]))
        yield "local"
    a = jnp.stack([h_.reshape(nb, hb, w) for h_ in halves], axis=1).reshape(r, w)
    ti = lax.broadcasted_iota(jnp.int32, (r, w), 0)
    ci = lax.broadcasted_iota(jnp.int32, (r, w), 1)
    same_block = ((ti % HG_CHUNK) // HG_BLOCK) == ((ci % HG_CHUNK) // HG_BLOCK)
    ad = jnp.where(same_block, a, 0.0).astype(BF16)

    def same_head(shape):
        return (lax.broadcasted_iota(jnp.int32, shape, 0) // HG_DIM) == (lax.broadcasted_iota(jnp.int32, shape, 1) // HG_DIM)

    pair_mask = same_head((HG_PAIR, HG_PAIR))
    head_mask = same_head((w, w))
    for c in range(n):
        rws = slice(c * HG_CHUNK, (c + 1) * HG_CHUNK)
        vt = hit_ref[0, c0 + c]
        for pp in range(pairs):
            x = _dot(vt[pp * HG_PAIR:(pp + 1) * HG_PAIR, :], ks_ref[pp, rws, :])
            d_ref[c, pp] = jnp.concatenate(
                [jnp.where(pair_mask, x[:, g * HG_PAIR:(g + 1) * HG_PAIR], 0.0).astype(BF16) for g in range(ng)],
                axis=1)
            si_ref[c, pp] = jnp.where(pair_mask, x[:, ng * HG_PAIR:], 0.0)
            yield "products"
    for c in range(n):
        rws = slice(c * HG_CHUNK, (c + 1) * HG_CHUNK)
        outs = [lax.dot_general(qb_ref[pp, rws, :], d_ref[c, pp], _NT, preferred_element_type=F32)
                for pp in range(pairs)]
        vc = v[rws]
        vbd = jnp.where(head_mask, jnp.concatenate([vc] * (w // HG_CHUNK), axis=0), jnp.zeros((), vc.dtype))
        o_ref[row0 + c * HG_CHUNK:row0 + (c + 1) * HG_CHUNK, :] = jnp.concatenate(outs, axis=1) + _dot(ad[rws], vbd)
        yield "products"


def _run_interleaved(first, second):
    live = [first, second]
    while live:
        for g in list(live):
            if next(g, None) is None:
                live.remove(g)


def _hg_step(qs_ref, si_ref, dl_ref, st_ref, o_ref, r0, g0, ci):
    outs = []
    for pp in range(st_ref.shape[0]):
        st = st_ref[pp]
        outs.append(lax.dot_general(qs_ref[pp, pl.ds(r0, HG_CHUNK), :], st.astype(BF16), _NT,
                                    preferred_element_type=F32))
        st_ref[pp] = st * dl_ref[pp, pl.ds(ci, 1), :] + si_ref[ci, pp]
    o_ref[pl.ds(g0, HG_CHUNK), :] += jnp.concatenate(outs, axis=1)


def _hgrn_kernel(hq_ref, hi_ref, hit_ref, hf_ref, hg_ref, hqc_ref, hic_ref, hitc_ref, hfc_ref, hgc_ref,
                 lb_ref, gn_ref, tri_ref, sel_ref, bd_ref, *rest, need_ctx):
    if need_ctx:
        rec_ref, recc_ref = rest[0], rest[1]
        scr = rest[2:]
    else:
        rec_ref, recc_ref = rest[0], None
        scr = rest[1:]
    of_ref, ob_ref, ofc_ref, obc_ref, stf_ref, stb_ref = scr[:6]
    fwd, bwd = scr[6:12], scr[12:18]
    w = hq_ref.shape[2]
    lb_f = lb_ref[0]
    lb_b = lb_ref[1]
    stf_ref[...] = jnp.zeros_like(stf_ref)
    stb_ref[...] = jnp.zeros_like(stb_ref)

    def run(hq, hi, hit, hf, o_f, o_b):
        t = hq.shape[1]
        tile = min(HG_TILE, t)
        nt = t // tile
        cpt = tile // HG_CHUNK
        for s in range(nt):
            rf = s * tile
            rb = (nt - 1 - s) * tile
            _run_interleaved(
                _hg_prepare(False, hq[0, rf:rf + tile, :], hf[0, rf:rf + tile, 0:w], hi[0, rf:rf + tile, :], hit,
                            rf // HG_CHUNK, lb_f, tri_ref.at[0], sel_ref, o_f, rf, *fwd),
                _hg_prepare(True, hq[0, rb:rb + tile, :], hf[0, rb:rb + tile, w:2 * w], hi[0, rb:rb + tile, :], hit,
                            rb // HG_CHUNK, lb_b, tri_ref.at[1], sel_ref, o_b, rb, *bwd))

            def body(i, carry):
                cf = i
                cb = cpt - 1 - i
                r0f = pl.multiple_of(cf * HG_CHUNK, HG_CHUNK)
                r0b = pl.multiple_of(cb * HG_CHUNK, HG_CHUNK)
                gf = pl.multiple_of(rf + cf * HG_CHUNK, HG_CHUNK)
                gb = pl.multiple_of(rb + cb * HG_CHUNK, HG_CHUNK)
                _hg_step(fwd[1], fwd[3], fwd[4], stf_ref, o_f, r0f, gf, cf)
                _hg_step(bwd[1], bwd[3], bwd[4], stb_ref, o_b, r0b, gb, cb)
                return carry

            lax.fori_loop(0, cpt, body, 0)

    def finish(o_f, o_b, hg, out):
        t = hg.shape[1]
        tile = min(HG_TILE, t)
        for s in range(t // tile):
            rows = slice(s * tile, (s + 1) * tile)
            o = o_f[rows, :] + o_b[rows, :]
            ms = _dot((o * o).astype(BF16), bd_ref[...])
            on = o * lax.rsqrt(ms + EPS) * gn_ref[...]
            out[0, rows, :] = (on * _silu(hg[0, rows, :].astype(F32))).astype(BF16)

    run(hqc_ref, hic_ref, hitc_ref, hfc_ref, ofc_ref, obc_ref)
    run(hq_ref, hi_ref, hit_ref, hf_ref, of_ref, ob_ref)
    finish(of_ref, ob_ref, hg_ref, rec_ref)
    if need_ctx:
        finish(ofc_ref, obc_ref, hgc_ref, recc_ref)


def _hgrn(lat, cx, lb, gn, tri, sel, bd, *, need_ctx):
    hq, hi, hit, hf, hg = lat
    hqc, hic, hitc, hfc, hgc = cx
    b, t, w = hq.shape
    tc = hqc.shape[1]
    tile = min(HG_TILE, t)

    def seq(a):
        nd = a.ndim
        return pl.BlockSpec((1,) + a.shape[1:], lambda bi: (bi,) + (0,) * (nd - 1))

    out_shape = [jax.ShapeDtypeStruct((b, t, w), BF16)]
    out_specs = [pl.BlockSpec((1, t, w), lambda bi: (bi, 0, 0))]
    if need_ctx:
        out_shape.append(jax.ShapeDtypeStruct((b, tc, w), BF16))
        out_specs.append(pl.BlockSpec((1, tc, w), lambda bi: (bi, 0, 0)))
    pairs = w // HG_PAIR
    cpt = tile // HG_CHUNK
    dir_scratch = [
        pltpu.VMEM((pairs, tile, (HG_NB - 1) * HG_PAIR), BF16),
        pltpu.VMEM((pairs, tile, HG_PAIR), BF16),
        pltpu.VMEM((pairs, tile, HG_NB * HG_PAIR), BF16),
        pltpu.VMEM((cpt, pairs, HG_PAIR, HG_PAIR), F32),
        pltpu.VMEM((pairs, cpt, HG_PAIR), F32),
        pltpu.VMEM((cpt, pairs, HG_PAIR, (HG_NB - 1) * HG_PAIR), BF16),
    ]
    res = pl.pallas_call(
        functools.partial(_hgrn_kernel, need_ctx=need_ctx),
        out_shape=tuple(out_shape),
        grid=(b,),
        in_specs=[seq(hq), seq(hi), seq(hit), seq(hf), seq(hg), seq(hqc), seq(hic), seq(hitc), seq(hfc), seq(hgc),
                  _full(lb.shape), _full(gn.shape), _full(tri.shape), _full(sel.shape), _full(bd.shape)],
        out_specs=tuple(out_specs),
        scratch_shapes=[pltpu.VMEM((t, w), F32), pltpu.VMEM((t, w), F32),
                        pltpu.VMEM((tc, w), F32), pltpu.VMEM((tc, w), F32),
                        pltpu.VMEM((pairs, HG_PAIR, HG_PAIR), F32), pltpu.VMEM((pairs, HG_PAIR, HG_PAIR), F32)]
        + dir_scratch + dir_scratch,
        compiler_params=_cparams("parallel"),
        name="hgrn",
    )(hq, hi, hit, hf, hg, hqc, hic, hitc, hfc, hgc, lb, gn, tri, sel, bd)
    return res if need_ctx else (res[0], None)


def _dft_kernel(cos_ref, msin_ref, ab_ref, w_ref, o_ref, z_scr):
    tf = cos_ref.shape[0]
    nsub = 2 if tf % (2 * BF16_ROWS) == 0 else 1
    sub = tf // nsub
    for step in range(nsub + 1):
        if step < nsub:
            rows = slice(step * sub, (step + 1) * sub)
            z_scr[rows, :] = (_dot(cos_ref[rows, :], ab_ref[0, 0]) + _dot(msin_ref[rows, :], ab_ref[0, 1])).astype(BF16)
        if step >= 1:
            rows = slice((step - 1) * sub, step * sub)
            o_ref[0, rows, :] = _dot(z_scr[rows, :], w_ref[...]).astype(BF16)


def _dft(cs, fab, wbd, *, tf):
    b, _, t, w = fab.shape
    return pl.pallas_call(
        _dft_kernel,
        out_shape=jax.ShapeDtypeStruct((b, t, w), BF16),
        grid=(t // tf, b),
        in_specs=[
            pl.BlockSpec((tf, t), lambda i, bi: (i, 0)),
            pl.BlockSpec((tf, t), lambda i, bi: (i, 0)),
            pl.BlockSpec((1, 2, t, w), lambda i, bi: (bi, 0, 0, 0)),
            _full(wbd.shape),
        ],
        out_specs=pl.BlockSpec((1, tf, w), lambda i, bi: (bi, i, 0)),
        scratch_shapes=[pltpu.VMEM((tf, w), BF16)],
        compiler_params=_cparams("parallel", "parallel"),
        name="dft",
    )(cs[0], cs[1], fab, wbd)


def _outproj_kernel(a_ref, r_ref, f_ref, x_ref, g1_ref, w_ref, lg_ref, lb_ref, o_ref, mix_scr, *, alpha):
    aw = a_ref.shape[2]
    rw = r_ref.shape[2]
    tm = x_ref.shape[1]
    nsub = 2 if tm % (2 * BF16_ROWS) == 0 else 1
    sub = tm // nsub

    def mix(i):
        rows = slice(i * sub, (i + 1) * sub)
        mix_scr[rows, :] = (_dot(a_ref[0, rows, :], w_ref[0:aw, :]) + _dot(r_ref[0, rows, :], w_ref[aw:aw + rw, :])
                            + _dot(f_ref[0, rows, :], w_ref[aw + rw:, :]))

    def norm(i):
        rows = slice(i * sub, (i + 1) * sub)
        o_ref[0, rows, :] = _layer_norm(alpha * x_ref[0, rows, :] + g1_ref[0] * mix_scr[rows, :],
                                        lg_ref[...], lb_ref[...])

    for step in range(nsub + 1):
        if step < nsub:
            mix(step)
        if step >= 1:
            norm(step - 1)


def _outproj(attn, rec, four, x, mod3, mod_row, l, w_out, lg, lb, *, alpha, tm):
    b, t, d = x.shape
    bt = lambda a: pl.BlockSpec((1, tm, a.shape[2]), lambda bi, i: (bi, i, 0))
    return pl.pallas_call(
        functools.partial(_outproj_kernel, alpha=alpha),
        out_shape=jax.ShapeDtypeStruct((b, t, d), F32),
        grid=(b, t // tm),
        in_specs=[bt(attn), bt(rec), bt(four), bt(x),
                  pl.BlockSpec((1, 1, d), lambda bi, i: (mod_row(bi), 0, 2)),
                  _layer(w_out, l), _full(lg.shape), _full(lb.shape)],
        out_specs=pl.BlockSpec((1, tm, d), lambda bi, i: (bi, i, 0)),
        scratch_shapes=[pltpu.VMEM((tm, d), F32)],
        compiler_params=_cparams("parallel", "parallel"),
        name="outproj",
    )(attn, rec, four, x, mod3, w_out, lg, lb)


def _ffn_kernel(x_ref, xp_ref, xn_ref, sh_ref, sc_ref, g2_ref, wup_ref, cw_ref, cb_ref, wdn_ref, lg_ref, lb_ref,
                o_ref, h_scr, u_scr, act_scr, y_scr, *, alpha):
    i = pl.program_id(1)
    last = pl.num_programs(1) - 1
    tm = x_ref.shape[1]
    halo = BF16_ROWS
    sc = sc_ref[0]
    sh = sh_ref[0]

    def mod(v):
        return (v * (1.0 + sc) + sh).astype(BF16)

    h_scr[0:halo, :] = jnp.where(i > 0, mod(xp_ref[0]), jnp.zeros((), BF16))
    h_scr[halo:halo + tm, :] = mod(x_ref[0])
    h_scr[halo + tm:, :] = jnp.where(i < last, mod(xn_ref[0]), jnp.zeros((), BF16))
    d_ff = wdn_ref.shape[0]
    cf = u_scr.shape[3]

    def conv(ub, cols):
        cw = cw_ref[:, cols]
        u = ub[...]
        rows = u.shape[0]
        prev = pltpu.roll(u, 1, 0)[halo:halo + tm]
        nxt = pltpu.roll(u, rows - 1, 0)[halo:halo + tm]
        return prev * cw[0:1, :] + u[halo:halo + tm] * cw[1:2, :] + nxt * cw[2:3, :] + cb_ref[:, cols]

    for j in range(d_ff // cf):
        val_cols = slice(j * cf, (j + 1) * cf)
        gate_cols = slice(d_ff + j * cf, d_ff + (j + 1) * cf)
        uv, ug = u_scr.at[j % 2, 0], u_scr.at[j % 2, 1]
        uv[...] = _dot(h_scr[...], wup_ref[:, val_cols])
        ug[...] = _dot(h_scr[...], wup_ref[:, gate_cols])
        act_scr[:, val_cols] = (conv(uv, val_cols) * _silu(conv(ug, gate_cols))).astype(BF16)
    nsub = 2 if tm % (2 * BF16_ROWS) == 0 else 1
    sub = tm // nsub
    for step in range(nsub + 1):
        if step < nsub:
            rows = slice(step * sub, (step + 1) * sub)
            y_scr[rows, :] = _dot(act_scr[rows, :], wdn_ref[...])
        if step >= 1:
            rows = slice((step - 1) * sub, step * sub)
            o_ref[0, rows, :] = _layer_norm(alpha * x_ref[0, rows, :] + g2_ref[0] * y_scr[rows, :],
                                            lg_ref[...], lb_ref[...])


def _ffn(x1, mod3, mod_row, l, wup, cw, cb, wdn, lg, lb, *, alpha, tm, cf):
    b, t, d = x1.shape
    halo = BF16_ROWS
    per = tm // halo
    nblk = t // halo
    return pl.pallas_call(
        functools.partial(_ffn_kernel, alpha=alpha),
        out_shape=jax.ShapeDtypeStruct((b, t, d), F32),
        grid=(b, t // tm),
        in_specs=[
            pl.BlockSpec((1, tm, d), lambda bi, i: (bi, i, 0)),
            pl.BlockSpec((1, halo, d), lambda bi, i: (bi, jnp.maximum(i * per - 1, 0), 0)),
            pl.BlockSpec((1, halo, d), lambda bi, i: (bi, jnp.minimum((i + 1) * per, nblk - 1), 0)),
            pl.BlockSpec((1, 1, d), lambda bi, i: (mod_row(bi), 0, 3)),
            pl.BlockSpec((1, 1, d), lambda bi, i: (mod_row(bi), 0, 4)),
            pl.BlockSpec((1, 1, d), lambda bi, i: (mod_row(bi), 0, 5)),
            _layer(wup, l, 1), _full(cw.shape), _full(cb.shape), _layer(wdn, l, 1), _full(lg.shape),
            _full(lb.shape),
        ],
        out_specs=pl.BlockSpec((1, tm, d), lambda bi, i: (bi, i, 0)),
        scratch_shapes=[pltpu.VMEM((tm + 2 * halo, d), BF16), pltpu.VMEM((2, 2, tm + 2 * halo, cf), F32),
                        pltpu.VMEM((tm, wdn.shape[1]), BF16), pltpu.VMEM((tm, d), F32)],
        compiler_params=_cparams("parallel", "arbitrary"),
        name="ffn",
    )(x1, x1, x1, mod3, mod3, mod3, wup, cw, cb, wdn, lg, lb)


def _block_diag_mean(width, seg):
    i = np.arange(width)
    return jnp.asarray(((i[:, None] // seg) == (i[None, :] // seg)).astype(np.float32) / seg, dtype=BF16)


def _hg_tri(n):
    i = np.arange(n)
    same = (i[:, None] // HG_CHUNK) == (i[None, :] // HG_CHUNK)
    lower = same & (i[None, :] <= i[:, None])
    upper = same & (i[None, :] >= i[:, None])
    return jnp.asarray(np.stack([lower, upper]).astype(np.float32), dtype=BF16)


def _hg_select(w):
    r = np.arange(HG_BLOCK * w)
    c = np.arange(w)
    sg, head_r = r // w, (r % w) // HG_DIM
    head_c, s = c // HG_DIM, c % HG_DIM
    m = (head_r[:, None] == head_c[None, :]) & ((s[None, :] % HG_BLOCK) == sg[:, None])
    return jnp.asarray(m.astype(np.float32), dtype=BF16)


def _position_dft(t):
    lo = 64
    f = jnp.arange(t, dtype=jnp.int32)[:, None]
    step = 2.0 * np.pi / t
    a = ((f * (lo * jnp.arange(t // lo, dtype=jnp.int32))[None, :]) % t).astype(F32) * step
    bb = ((f * jnp.arange(lo, dtype=jnp.int32)[None, :]) % t).astype(F32) * step
    ca, sa = jnp.cos(a)[:, :, None], jnp.sin(a)[:, :, None]
    cb, sb = jnp.cos(bb)[:, None, :], jnp.sin(bb)[:, None, :]
    norm = 1.0 / np.sqrt(t)
    cos = ((ca * cb - sa * sb) * norm).astype(BF16).reshape(t, t)
    msin = ((sa * cb + ca * sb) * -norm).astype(BF16).reshape(t, t)
    return cos, msin


def _channel_dft(groups):
    c = np.arange(FT_DIM, dtype=np.int64)
    angc = 2.0 * np.pi * ((c[:, None] * c[None, :]) % FT_DIM).astype(np.float64) / FT_DIM
    eye = np.eye(groups)
    dftc = np.concatenate([np.kron(eye, np.cos(angc)), np.kron(eye, np.sin(angc))], axis=1) / np.sqrt(FT_DIM)
    return jnp.asarray(dftc, dtype=F32).astype(BF16)


def _rope_tables(t):
    rows = t // GRID_W
    row = jnp.repeat(jnp.arange(rows, dtype=jnp.int32), GRID_W)
    col = jnp.tile(jnp.arange(GRID_W, dtype=jnp.int32), rows)
    nf = HEAD_DIM // 4
    inv = ROPE_THETA ** (-jnp.arange(nf, dtype=F32) / nf)
    ar = row.astype(F32)[:, None] * inv
    ac = col.astype(F32)[:, None] * inv
    cos = jnp.concatenate([jnp.cos(ar)] * 2 + [jnp.cos(ac)] * 2, axis=1)
    sin = jnp.concatenate([-jnp.sin(ar), jnp.sin(ar), -jnp.sin(ac), jnp.sin(ac)], axis=1)
    return jnp.tile(cos, (1, 128 // HEAD_DIM)), jnp.tile(sin, (1, 128 // HEAD_DIM))


def kernel(x, c, ctx, c_ctx, w_ada, b_ada, w_in, q_norm, k_norm, hg_lb, hg_norm, ft_w, w_out, ln1_g, ln1_b, w_up,
           conv_w, conv_b, w_down, ln2_g, ln2_b):
    b, t, d = x.shape
    tc = ctx.shape[1]
    depth = w_ada.shape[0]
    d_ff = w_down.shape[1]
    hgw = hg_lb.shape[2]
    ftw = ft_w.shape[1] * ft_w.shape[2]
    kvw = ATTN_KV_HEADS * HEAD_DIM
    aw = w_in.shape[2] - 2 * kvw - 5 * hgw - ftw
    alpha = (2 * depth) ** 0.25

    sizes = [("q", aw), ("k", kvw), ("v", kvw), ("hq", hgw), ("hi", hgw), ("hf", 2 * hgw), ("hg", hgw), ("ft", ftw)]
    offs, lo = {}, 0
    for name, size in sizes:
        offs[name] = (lo, lo + size)
        lo += size

    mod_rows = -(-(b + 1) // 8) * 8
    cond = jnp.concatenate([c, c_ctx[None, :], jnp.zeros((mod_rows - b - 1, d), F32)], axis=0)
    mod = _ada(cond, w_ada, b_ada)
    mod3 = mod.reshape(depth * mod_rows, 1, 6 * d)

    lb_soft = jax.nn.softmax(hg_lb.astype(F32), axis=0)
    lower = (jnp.cumsum(lb_soft, axis=0) - lb_soft[:1]).reshape(depth, 2, 1, hgw)

    cos, sin = _rope_tables(t)
    bdq = _block_diag_mean(2 * 128, HEAD_DIM)
    bdk = _block_diag_mean(kvw, HEAD_DIM)
    bdh = _block_diag_mean(hgw, HG_DIM)
    tri = _hg_tri(min(HG_TILE, t))
    sel = _hg_select(hgw)
    dftc = _channel_dft(ftw // FT_DIM)
    cs_lat = _position_dft(t)
    cs_ctx = _position_dft(tc)

    cf = 256
    assert d_ff % cf == 0
    w_in_b, w_out_b, w_up_b, w_down_b = (a.astype(BF16) for a in (w_in, w_out, w_up, w_down))

    for l in range(depth):
        need_ctx = l < depth - 1
        lat_row = lambda bi, l=l: l * mod_rows + bi
        ctx_row = lambda bi, l=l: l * mod_rows + b

        qg = jnp.tile(q_norm[l].astype(F32), aw // HEAD_DIM)[None, :]
        kg = jnp.tile(k_norm[l].astype(F32), kvw // HEAD_DIM)[None, :]
        proj_args = (w_in_b, cos, sin, qg, kg, bdq, bdk, dftc)
        q, k, v, hq, hi, hg, hf, hit, fab = _inproj(x, mod3, lat_row, l, *proj_args, offs=offs, rope=True, tm=512)
        qc, kc, vc, hqc, hic, hgc, hfc, hitc, fabc = _inproj(ctx, mod3, ctx_row, l, *proj_args, offs=offs,
                                                               rope=False, tm=tc)

        attn = _attention(q, jnp.concatenate([kc, k], axis=3), jnp.concatenate([vc, v], axis=2), tq=512)

        gn = jnp.tile(hg_norm[l].astype(F32), hgw // HG_DIM)[None, :]
        rec, rec_c = _hgrn((hq, hi, hit, hf, hg), (hqc, hic, hitc, hfc, hgc), lower[l], gn, tri, sel, bdh,
                           need_ctx=need_ctx)

        wbd = jax.scipy.linalg.block_diag(*[ft_w[l, g] for g in range(ft_w.shape[1])]).astype(BF16)
        four = _dft(cs_lat, fab, wbd, tf=512)

        lg1, lb1 = ln1_g[l][None, :], ln1_b[l][None, :]
        lg2, lb2 = ln2_g[l][None, :], ln2_b[l][None, :]
        ffn_args = (l, w_up_b, conv_w[l], conv_b[l][None, :], w_down_b, lg2, lb2)

        x1 = _outproj(attn, rec, four, x, mod3, lat_row, l, w_out_b, lg1, lb1, alpha=alpha, tm=512)
        x_next = _ffn(x1, mod3, lat_row, *ffn_args, alpha=alpha, tm=512, cf=cf)

        if need_ctx:
            attn_c = _attention(qc, kc, vc, tq=tc)
            four_c = _dft(cs_ctx, fabc, wbd, tf=tc)
            ctx1 = _outproj(attn_c, rec_c, four_c, ctx, mod3, ctx_row, l, w_out_b, lg1, lb1, alpha=alpha, tm=tc)
            ctx = _ffn(ctx1, mod3, ctx_row, *ffn_args, alpha=alpha, tm=tc, cf=cf)
        x = x_next

    return x
```

```python
import functools

import numpy as np
import jax
import jax.numpy as jnp
from jax import lax
from jax.experimental import pallas as pl
from jax.experimental.pallas import tpu as pltpu

F32 = jnp.float32
BF16 = jnp.bfloat16

GRID_W = 64
HEAD_DIM = 64
ATTN_KV_HEADS = 2
HG_DIM = 64
FT_DIM = 64
ROPE_THETA = 10000.0
F_MIN = 1e-30
EPS = 1e-6
CONV_W = 3

HG_CHUNK = 64
HG_BLOCK = 16
HG_NB = HG_CHUNK // HG_BLOCK
HG_PAIR = 2 * HG_DIM
HG_TILE = 512
NEG = -1e30
LOG2_E = 1.4426950408889634

V7X_VMEM_BYTES = 64 * 1024 * 1024
VMEM_LIMIT = int(V7X_VMEM_BYTES * 0.85)
BF16_ROWS = 16

_NT = (((1,), (1,)), ((), ()))


def _cparams(*sem):
    return pltpu.CompilerParams(dimension_semantics=sem, vmem_limit_bytes=VMEM_LIMIT)


def _dot(a, b):
    return jnp.dot(a, b, preferred_element_type=F32)


def _silu(x):
    return x * jax.nn.sigmoid(x)


def _layer_norm(y, g, b):
    mu = jnp.mean(y, axis=-1, keepdims=True)
    d = y - mu
    var = jnp.mean(d * d, axis=-1, keepdims=True)
    return d * lax.rsqrt(var + EPS) * g + b


def _full(shape):
    n = len(shape)
    return pl.BlockSpec(shape, lambda *_: (0,) * n)


def _layer(w, l, buffers=2):
    n = w.ndim - 1
    return pl.BlockSpec((None,) + w.shape[1:], lambda *_: (l,) + (0,) * n, pipeline_mode=pl.Buffered(buffers))


def _ada_kernel(c_ref, w_ref, b_ref, o_ref):
    s = _silu(c_ref[...]).astype(BF16)
    o_ref[0] = _dot(s, w_ref[0].astype(BF16)) + b_ref[0]


def _ada(cond, w_ada, b_ada):
    depth, d, n = w_ada.shape
    rows = cond.shape[0]
    tn = n // 4
    return pl.pallas_call(
        _ada_kernel,
        out_shape=jax.ShapeDtypeStruct((depth, rows, n), F32),
        grid=(depth, n // tn),
        in_specs=[
            pl.BlockSpec((rows, d), lambda l, j: (0, 0)),
            pl.BlockSpec((1, d, tn), lambda l, j: (l, 0, j)),
            pl.BlockSpec((1, 1, tn), lambda l, j: (l, 0, j)),
        ],
        out_specs=pl.BlockSpec((1, rows, tn), lambda l, j: (l, 0, j)),
        compiler_params=_cparams("arbitrary", "arbitrary"),
        name="ada",
    )(cond, w_ada, b_ada.reshape(depth, 1, n))


def _rope128(y, cos, sin_signed):
    lane = lax.broadcasted_iota(jnp.int32, y.shape, 1)
    upper = (lane % 32) >= 16
    partner = jnp.where(upper, pltpu.roll(y, 16, 1), pltpu.roll(y, 112, 1))
    return y * cos + partner * sin_signed


def _inproj_kernel(x_ref, sh_ref, sc_ref, w_ref, cos_ref, sin_ref, qg_ref, kg_ref, bdq_ref, bdk_ref,
                   dftc_ref, q_ref, k_ref, v_ref, hq_ref, hi_ref, hg_ref, hf_ref, hit_ref, fab_ref, h_scr, qk_scr,
                   *, rope, offs):
    h_scr[...] = (x_ref[0] * (1.0 + sc_ref[0]) + sh_ref[0]).astype(BF16)

    def proj(name):
        lo, hi = offs[name]
        return _dot(h_scr[...], w_ref[:, lo:hi])

    q_lo, q_hi = offs["q"]
    k_lo, k_hi = offs["k"]

    def norm_rope(lo, hi, bd_ref, g_ref):
        base = q_lo if lo < q_hi else k_lo
        y = qk_scr[:, lo - q_lo:hi - q_lo]
        y = y * lax.rsqrt(_dot((y * y).astype(BF16), bd_ref[...]) + EPS) * g_ref[:, lo - base:hi - base]
        if rope:
            cos, sin = cos_ref[...], sin_ref[...]
            y = jnp.concatenate([_rope128(y[:, c:c + 128], cos, sin) for c in range(0, hi - lo, 128)], axis=1)
        return y

    q_scale = HEAD_DIM ** -0.5 * LOG2_E

    def q_part(lo, hi):
        q_ref[0, :, lo - q_lo:hi - q_lo] = (norm_rope(lo, hi, bdq_ref, qg_ref) * q_scale).astype(BF16)

    qk_scr[...] = _dot(h_scr[...], w_ref[:, q_lo:k_hi])
    v = proj("v")
    lane = lax.broadcasted_iota(jnp.int32, v.shape, 1)
    for hh in range(v.shape[1] // HEAD_DIM):
        vh = v if hh == 0 else pltpu.roll(v, v.shape[1] - hh * HEAD_DIM, 1)
        v_ref[0, hh] = jnp.where(lane < HEAD_DIM, vh, jnp.where(lane == HEAD_DIM, 1.0, 0.0)).astype(BF16)
    hq_ref[0] = proj("hq").astype(BF16)
    qw = bdq_ref.shape[0]
    q_part(q_lo, q_lo + qw)
    hi = proj("hi")
    hi_ref[0] = hi.astype(BF16)
    vt = hi.T.astype(BF16)
    for j in range(vt.shape[1] // HG_CHUNK):
        hit_ref[0, j] = vt[:, j * HG_CHUNK:(j + 1) * HG_CHUNK]
    for lo in range(q_lo + qw, q_hi, qw):
        q_part(lo, lo + qw)
    hg_ref[0] = proj("hg").astype(BF16)
    kt = norm_rope(k_lo, k_hi, bdk_ref, kg_ref).T.astype(BF16)
    for hh in range(kt.shape[0] // HEAD_DIM):
        k_ref[0, hh] = kt[hh * HEAD_DIM:(hh + 1) * HEAD_DIM, :]
    hf_ref[0] = proj("hf")
    ab = _dot(proj("ft").astype(BF16), dftc_ref[...])
    half = ab.shape[1] // 2
    fab_ref[0, 0] = ab[:, :half].astype(BF16)
    fab_ref[0, 1] = ab[:, half:].astype(BF16)


def _inproj(x, mod3, mod_row, l, w_in, cos, sin, qg, kg, bdq, bdk, dftc, *, offs, rope, tm):
    b, t, d = x.shape
    aw = offs["q"][1] - offs["q"][0]
    kw = offs["k"][1] - offs["k"][0]
    hw = offs["hq"][1] - offs["hq"][0]
    fw = offs["ft"][1] - offs["ft"][0]
    row = lambda bi: mod_row(bi)
    bt = lambda width: pl.BlockSpec((1, tm, width), lambda bi, i: (bi, i, 0))
    kvh = kw // HEAD_DIM
    out_shapes = (
        jax.ShapeDtypeStruct((b, t, aw), BF16), jax.ShapeDtypeStruct((b, kvh, HEAD_DIM, t), BF16),
        jax.ShapeDtypeStruct((b, kvh, t, 2 * HEAD_DIM), BF16), jax.ShapeDtypeStruct((b, t, hw), BF16),
        jax.ShapeDtypeStruct((b, t, hw), BF16), jax.ShapeDtypeStruct((b, t, hw), BF16),
        jax.ShapeDtypeStruct((b, t, 2 * hw), F32),
        jax.ShapeDtypeStruct((b, t // HG_CHUNK, hw, HG_CHUNK), BF16),
        jax.ShapeDtypeStruct((b, 2, t, fw), BF16),
    )
    out_specs = (
        bt(aw),
        pl.BlockSpec((1, kvh, HEAD_DIM, tm), lambda bi, i: (bi, 0, 0, i)),
        pl.BlockSpec((1, kvh, tm, 2 * HEAD_DIM), lambda bi, i: (bi, 0, i, 0)),
        bt(hw), bt(hw), bt(hw), bt(2 * hw),
        pl.BlockSpec((1, tm // HG_CHUNK, hw, HG_CHUNK), lambda bi, i: (bi, i, 0, 0)),
        pl.BlockSpec((1, 2, tm, fw), lambda bi, i: (bi, 0, i, 0)),
    )
    return pl.pallas_call(
        functools.partial(_inproj_kernel, rope=rope, offs=offs),
        out_shape=out_shapes,
        grid=(b, t // tm),
        in_specs=[
            pl.BlockSpec((1, tm, d), lambda bi, i: (bi, i, 0)),
            pl.BlockSpec((1, 1, d), lambda bi, i: (row(bi), 0, 0)),
            pl.BlockSpec((1, 1, d), lambda bi, i: (row(bi), 0, 1)),
            _layer(w_in, l),
            pl.BlockSpec((tm, 128), lambda bi, i: (i, 0)),
            pl.BlockSpec((tm, 128), lambda bi, i: (i, 0)),
            _full(qg.shape), _full(kg.shape), _full(bdq.shape), _full(bdk.shape), _full(dftc.shape),
        ],
        out_specs=out_specs,
        scratch_shapes=[pltpu.VMEM((tm, d), BF16), pltpu.VMEM((tm, aw + kw), F32)],
        compiler_params=_cparams("parallel", "parallel"),
        name="inproj",
    )(x, mod3, mod3, w_in, cos, sin, qg, kg, bdq, bdk, dftc)


def _attn_kernel(q_ref, kt_ref, v_ref, o_ref, s_scr, m_scr, p_scr, *, group):
    kt = kt_ref[0, 0]
    v = v_ref[0, 0]
    half = (v.shape[0] // 2) // 256 * 256
    tq = q_ref.shape[1]
    nrg = 2 if tq % 512 == 0 else 1
    rg = tq // nrg
    units = [(h, r) for h in range(group) for r in range(nrg)]

    def scores(h, r):
        rows = slice(r * rg, (r + 1) * rg)
        s = _dot(q_ref[0, rows, h * HEAD_DIM:(h + 1) * HEAD_DIM], kt)
        s_scr[h, rows, :] = s
        m_scr[h, rows, :] = jnp.max(s, axis=-1, keepdims=True)

    def probs(h, r):
        rows = slice(r * rg, (r + 1) * rg)
        p_scr[h, rows, :] = jnp.exp2(s_scr[h, rows, :] - m_scr[h, rows, :]).astype(BF16)

    def values(h, r):
        rows = slice(r * rg, (r + 1) * rg)
        if half:
            pv = _dot(p_scr[h, rows, :half], v[:half]) + _dot(p_scr[h, rows, half:], v[half:])
        else:
            pv = _dot(p_scr[h, rows, :], v)
        return pv[:, :HEAD_DIM] / pv[:, HEAD_DIM:HEAD_DIM + 1]

    outs = {}
    for step in range(len(units) + 2):
        if step < len(units):
            scores(*units[step])
        if 0 <= step - 1 < len(units):
            probs(*units[step - 1])
        if 0 <= step - 2 < len(units):
            outs[units[step - 2]] = values(*units[step - 2])
    o_ref[0] = jnp.concatenate(
        [jnp.concatenate([outs[(h, r)] for r in range(nrg)], axis=0) for h in range(group)], axis=1).astype(BF16)


def _attention(q, kt_all, v_all, *, tq):
    b, t, aw = q.shape
    kvh, s = v_all.shape[1], v_all.shape[2]
    group = aw // HEAD_DIM // kvh
    gw = group * HEAD_DIM
    return pl.pallas_call(
        functools.partial(_attn_kernel, group=group),
        out_shape=jax.ShapeDtypeStruct((b, t, aw), BF16),
        grid=(b, kvh, t // tq),
        in_specs=[
            pl.BlockSpec((1, tq, gw), lambda bi, g, i: (bi, i, g)),
            pl.BlockSpec((1, 1, HEAD_DIM, s), lambda bi, g, i: (bi, g, 0, 0)),
            pl.BlockSpec((1, 1, s, 2 * HEAD_DIM), lambda bi, g, i: (bi, g, 0, 0)),
        ],
        out_specs=pl.BlockSpec((1, tq, gw), lambda bi, g, i: (bi, i, g)),
        scratch_shapes=[pltpu.VMEM((group, tq, s), F32), pltpu.VMEM((group, tq, 1), F32),
                        pltpu.VMEM((group, tq, s), BF16)],
        compiler_params=_cparams("parallel", "parallel", "parallel"),
        name="attention",
    )(q, kt_all, v_all)


def _hg_prepare(rev, hq, z, v, hit_ref, c0, lb, tri_ref, sel_ref, o_ref, row0, qb_ref, qs_ref, ks_ref, si_ref, dl_ref,
                d_ref):
    r, w = z.shape
    n = r // HG_CHUNK
    f = lb + (1.0 - lb) * jax.nn.sigmoid(z)
    logf = jnp.log2(jnp.maximum(f, F_MIN))
    kk = 1.0 - f
    qs = _silu(hq.astype(F32))
    hi = logf.astype(BF16)
    lo = (logf - hi.astype(F32)).astype(BF16)
    tri = tri_ref[0:r, 0:r]
    cum = _dot(tri, hi) + _dot(tri, lo)
    yield "local"

    c3 = cum.reshape(n, HG_CHUNK, w)
    q3 = qs.reshape(n, HG_CHUNK, w)
    k3 = kk.reshape(n, HG_CHUNK, w)

    def rows(a, lo_, hi_):
        parts = []
        if lo_ > 0:
            parts.append(jnp.zeros((n, lo_, w), F32))
        parts.append(a)
        if hi_ < HG_CHUNK:
            parts.append(jnp.zeros((n, HG_CHUNK - hi_, w), F32))
        return jnp.concatenate(parts, axis=1) if len(parts) > 1 else a

    e_last = c3[:, 0:1, :] if rev else c3[:, HG_CHUNK - 1:HG_CHUNK, :]
    qparts, kparts = [], []
    for g in range(1, HG_NB):
        j = HG_NB - g if rev else g - 1
        b0, b1 = j * HG_BLOCK, (j + 1) * HG_BLOCK
        e = c3[:, b0:b0 + 1, :] if rev else c3[:, b1 - 1:b1, :]
        p0, p1 = (0, b0) if rev else (b1, HG_CHUNK)
        qparts.append(rows(q3[:, p0:p1, :] * jnp.exp2(c3[:, p0:p1, :] - e), p0, p1))
        kparts.append(rows(k3[:, b0:b1, :] * jnp.exp2(e - c3[:, b0:b1, :]), b0, b1))
    qparts.append(q3 * jnp.exp2(c3))
    kparts.append(k3 * jnp.exp2(e_last - c3))
    dl = jnp.exp2(e_last).reshape(n, w)
    pairs = w // HG_PAIR
    ng = HG_NB - 1
    for pp in range(pairs):
        lanes = slice(pp * HG_PAIR, (pp + 1) * HG_PAIR)
        qb_ref[pp, 0:r, :] = jnp.concatenate([p[:, :, lanes] for p in qparts[:ng]], axis=-1).reshape(
            r, ng * HG_PAIR).astype(BF16)
        qs_ref[pp, 0:r, :] = qparts[ng][:, :, lanes].reshape(r, HG_PAIR).astype(BF16)
        ks_ref[pp, 0:r, :] = jnp.concatenate([p[:, :, lanes] for p in kparts], axis=-1).reshape(
            r, HG_NB * HG_PAIR).astype(BF16)
        dl_ref[pp, 0:n, :] = dl[:, lanes]
        yield "local"

    nb = r // HG_BLOCK
    hb = HG_BLOCK // 2
    c5 = cum.reshape(nb, 2, hb, w)
    q5 = qs.reshape(nb, 2, hb, w)
    d5 = (cum - jnp.log2(jnp.maximum(kk, 0.0))).reshape(nb, 2, hb, w)
    tau = lax.broadcasted_iota(jnp.int32, (nb, hb, w), 1)
    halves = []
    for half in range(2):
        sources = [sg for sg in range(HG_BLOCK) if not ((half > sg // hb) if rev else (half < sg // hb))]
        eparts = []
        for sg in sources:
            sh, sl = sg // hb, sg % hb
            arg = c5[:, half] - d5[:, sh, sl:sl + 1, :]
            if half == sh:
                arg = jnp.where((tau <= sl) if rev else (tau >= sl), arg, NEG)
            eparts.append((jnp.exp2(arg) * q5[:, half]).reshape(nb * hb, w).astype(BF16))
            if len(eparts) % 2 == 0:
                yield "local"
        lo_ = sources[0] * w
        halves.append(_dot(jnp.concatenate(eparts, axis=-1), sel_ref[lo_:lo_ + len(sources) * w, :]))
        yield "local"
    a = jnp.stack([h_.reshape(nb, hb, w) for h_ in halves], axis=1).reshape(r, w)
    ti = lax.broadcasted_iota(jnp.int32, (r, w), 0)
    ci = lax.broadcasted_iota(jnp.int32, (r, w), 1)
    same_block = ((ti % HG_CHUNK) // HG_BLOCK) == ((ci % HG_CHUNK) // HG_BLOCK)
    ad = jnp.where(same_block, a, 0.0).astype(BF16)

    def same_head(shape):
        return (lax.broadcasted_iota(jnp.int32, shape, 0) // HG_DIM) == (lax.broadcasted_iota(jnp.int32, shape, 1) // HG_DIM)

    pair_mask = same_head((HG_PAIR, HG_PAIR))
    head_mask = same_head((w, w))
    for c in range(n):
        rws = slice(c * HG_CHUNK, (c + 1) * HG_CHUNK)
        vt = hit_ref[0, c0 + c]
        for pp in range(pairs):
            x = _dot(vt[pp * HG_PAIR:(pp + 1) * HG_PAIR, :], ks_ref[pp, rws, :])
            d_ref[c, pp] = jnp.concatenate(
                [jnp.where(pair_mask, x[:, g * HG_PAIR:(g + 1) * HG_PAIR], 0.0).astype(BF16) for g in range(ng)],
                axis=1)
            si_ref[c, pp] = jnp.where(pair_mask, x[:, ng * HG_PAIR:], 0.0)
            yield "products"
    for c in range(n):
        rws = slice(c * HG_CHUNK, (c + 1) * HG_CHUNK)
        outs = [lax.dot_general(qb_ref[pp, rws, :], d_ref[c, pp], _NT, preferred_element_type=F32)
                for pp in range(pairs)]
        vc = v[rws]
        vbd = jnp.where(head_mask, jnp.concatenate([vc] * (w // HG_CHUNK), axis=0), jnp.zeros((), vc.dtype))
        o_ref[row0 + c * HG_CHUNK:row0 + (c + 1) * HG_CHUNK, :] = jnp.concatenate(outs, axis=1) + _dot(ad[rws], vbd)
        yield "products"


def _run_interleaved(first, second):
    live = [first, second]
    while live:
        for g in list(live):
            if next(g, None) is None:
                live.remove(g)


def _hg_step(qs_ref, si_ref, dl_ref, st_ref, o_ref, r0, g0, ci):
    outs = []
    for pp in range(st_ref.shape[0]):
        st = st_ref[pp]
        outs.append(lax.dot_general(qs_ref[pp, pl.ds(r0, HG_CHUNK), :], st.astype(BF16), _NT,
                                    preferred_element_type=F32))
        st_ref[pp] = st * dl_ref[pp, pl.ds(ci, 1), :] + si_ref[ci, pp]
    o_ref[pl.ds(g0, HG_CHUNK), :] += jnp.concatenate(outs, axis=1)


def _hgrn_kernel(hq_ref, hi_ref, hit_ref, hf_ref, hg_ref, hqc_ref, hic_ref, hitc_ref, hfc_ref, hgc_ref,
                 lb_ref, gn_ref, tri_ref, sel_ref, bd_ref, *rest, need_ctx):
    if need_ctx:
        rec_ref, recc_ref = rest[0], rest[1]
        scr = rest[2:]
    else:
        rec_ref, recc_ref = rest[0], None
        scr = rest[1:]
    of_ref, ob_ref, ofc_ref, obc_ref, stf_ref, stb_ref = scr[:6]
    fwd, bwd = scr[6:12], scr[12:18]
    w = hq_ref.shape[2]
    lb_f = lb_ref[0]
    lb_b = lb_ref[1]
    stf_ref[...] = jnp.zeros_like(stf_ref)
    stb_ref[...] = jnp.zeros_like(stb_ref)

    def run(hq, hi, hit, hf, o_f, o_b):
        t = hq.shape[1]
        tile = min(HG_TILE, t)
        nt = t // tile
        cpt = tile // HG_CHUNK
        for s in range(nt):
            rf = s * tile
            rb = (nt - 1 - s) * tile
            _run_interleaved(
                _hg_prepare(False, hq[0, rf:rf + tile, :], hf[0, rf:rf + tile, 0:w], hi[0, rf:rf + tile, :], hit,
                            rf // HG_CHUNK, lb_f, tri_ref.at[0], sel_ref, o_f, rf, *fwd),
                _hg_prepare(True, hq[0, rb:rb + tile, :], hf[0, rb:rb + tile, w:2 * w], hi[0, rb:rb + tile, :], hit,
                            rb // HG_CHUNK, lb_b, tri_ref.at[1], sel_ref, o_b, rb, *bwd))

            def body(i, carry):
                cf = i
                cb = cpt - 1 - i
                r0f = pl.multiple_of(cf * HG_CHUNK, HG_CHUNK)
                r0b = pl.multiple_of(cb * HG_CHUNK, HG_CHUNK)
                gf = pl.multiple_of(rf + cf * HG_CHUNK, HG_CHUNK)
                gb = pl.multiple_of(rb + cb * HG_CHUNK, HG_CHUNK)
                _hg_step(fwd[1], fwd[3], fwd[4], stf_ref, o_f, r0f, gf, cf)
                _hg_step(bwd[1], bwd[3], bwd[4], stb_ref, o_b, r0b, gb, cb)
                return carry

            lax.fori_loop(0, cpt, body, 0)

    def finish(o_f, o_b, hg, out):
        t = hg.shape[1]
        tile = min(HG_TILE, t)
        for s in range(t // tile):
            rows = slice(s * tile, (s + 1) * tile)
            o = o_f[rows, :] + o_b[rows, :]
            ms = _dot((o * o).astype(BF16), bd_ref[...])
            on = o * lax.rsqrt(ms + EPS) * gn_ref[...]
            out[0, rows, :] = (on * _silu(hg[0, rows, :].astype(F32))).astype(BF16)

    run(hqc_ref, hic_ref, hitc_ref, hfc_ref, ofc_ref, obc_ref)
    run(hq_ref, hi_ref, hit_ref, hf_ref, of_ref, ob_ref)
    finish(of_ref, ob_ref, hg_ref, rec_ref)
    if need_ctx:
        finish(ofc_ref, obc_ref, hgc_ref, recc_ref)


def _hgrn(lat, cx, lb, gn, tri, sel, bd, *, need_ctx):
    hq, hi, hit, hf, hg = lat
    hqc, hic, hitc, hfc, hgc = cx
    b, t, w = hq.shape
    tc = hqc.shape[1]
    tile = min(HG_TILE, t)

    def seq(a):
        nd = a.ndim
        return pl.BlockSpec((1,) + a.shape[1:], lambda bi: (bi,) + (0,) * (nd - 1))

    out_shape = [jax.ShapeDtypeStruct((b, t, w), BF16)]
    out_specs = [pl.BlockSpec((1, t, w), lambda bi: (bi, 0, 0))]
    if need_ctx:
        out_shape.append(jax.ShapeDtypeStruct((b, tc, w), BF16))
        out_specs.append(pl.BlockSpec((1, tc, w), lambda bi: (bi, 0, 0)))
    pairs = w // HG_PAIR
    cpt = tile // HG_CHUNK
    dir_scratch = [
        pltpu.VMEM((pairs, tile, (HG_NB - 1) * HG_PAIR), BF16),
        pltpu.VMEM((pairs, tile, HG_PAIR), BF16),
        pltpu.VMEM((pairs, tile, HG_NB * HG_PAIR), BF16),
        pltpu.VMEM((cpt, pairs, HG_PAIR, HG_PAIR), F32),
        pltpu.VMEM((pairs, cpt, HG_PAIR), F32),
        pltpu.VMEM((cpt, pairs, HG_PAIR, (HG_NB - 1) * HG_PAIR), BF16),
    ]
    res = pl.pallas_call(
        functools.partial(_hgrn_kernel, need_ctx=need_ctx),
        out_shape=tuple(out_shape),
        grid=(b,),
        in_specs=[seq(hq), seq(hi), seq(hit), seq(hf), seq(hg), seq(hqc), seq(hic), seq(hitc), seq(hfc), seq(hgc),
                  _full(lb.shape), _full(gn.shape), _full(tri.shape), _full(sel.shape), _full(bd.shape)],
        out_specs=tuple(out_specs),
        scratch_shapes=[pltpu.VMEM((t, w), F32), pltpu.VMEM((t, w), F32),
                        pltpu.VMEM((tc, w), F32), pltpu.VMEM((tc, w), F32),
                        pltpu.VMEM((pairs, HG_PAIR, HG_PAIR), F32), pltpu.VMEM((pairs, HG_PAIR, HG_PAIR), F32)]
        + dir_scratch + dir_scratch,
        compiler_params=_cparams("parallel"),
        name="hgrn",
    )(hq, hi, hit, hf, hg, hqc, hic, hitc, hfc, hgc, lb, gn, tri, sel, bd)
    return res if need_ctx else (res[0], None)


def _dft_kernel(cos_ref, msin_ref, ab_ref, w_ref, o_ref, z_scr):
    tf = cos_ref.shape[0]
    nsub = 2 if tf % (2 * BF16_ROWS) == 0 else 1
    sub = tf // nsub
    for step in range(nsub + 1):
        if step < nsub:
            rows = slice(step * sub, (step + 1) * sub)
            z_scr[rows, :] = (_dot(cos_ref[rows, :], ab_ref[0, 0]) + _dot(msin_ref[rows, :], ab_ref[0, 1])).astype(BF16)
        if step >= 1:
            rows = slice((step - 1) * sub, step * sub)
            o_ref[0, rows, :] = _dot(z_scr[rows, :], w_ref[...]).astype(BF16)


def _dft(cs, fab, wbd, *, tf):
    b, _, t, w = fab.shape
    return pl.pallas_call(
        _dft_kernel,
        out_shape=jax.ShapeDtypeStruct((b, t, w), BF16),
        grid=(t // tf, b),
        in_specs=[
            pl.BlockSpec((tf, t), lambda i, bi: (i, 0)),
            pl.BlockSpec((tf, t), lambda i, bi: (i, 0)),
            pl.BlockSpec((1, 2, t, w), lambda i, bi: (bi, 0, 0, 0)),
            _full(wbd.shape),
        ],
        out_specs=pl.BlockSpec((1, tf, w), lambda i, bi: (bi, i, 0)),
        scratch_shapes=[pltpu.VMEM((tf, w), BF16)],
        compiler_params=_cparams("parallel", "parallel"),
        name="dft",
    )(cs[0], cs[1], fab, wbd)


def _outproj_kernel(a_ref, r_ref, f_ref, x_ref, g1_ref, w_ref, lg_ref, lb_ref, o_ref, mix_scr, *, alpha):
    aw = a_ref.shape[2]
    rw = r_ref.shape[2]
    tm = x_ref.shape[1]
    nsub = 2 if tm % (2 * BF16_ROWS) == 0 else 1
    sub = tm // nsub

    def mix(i):
        rows = slice(i * sub, (i + 1) * sub)
        mix_scr[rows, :] = (_dot(a_ref[0, rows, :], w_ref[0:aw, :]) + _dot(r_ref[0, rows, :], w_ref[aw:aw + rw, :])
                            + _dot(f_ref[0, rows, :], w_ref[aw + rw:, :]))

    def norm(i):
        rows = slice(i * sub, (i + 1) * sub)
        o_ref[0, rows, :] = _layer_norm(alpha * x_ref[0, rows, :] + g1_ref[0] * mix_scr[rows, :],
                                        lg_ref[...], lb_ref[...])

    for step in range(nsub + 1):
        if step < nsub:
            mix(step)
        if step >= 1:
            norm(step - 1)


def _outproj(attn, rec, four, x, mod3, mod_row, l, w_out, lg, lb, *, alpha, tm):
    b, t, d = x.shape
    bt = lambda a: pl.BlockSpec((1, tm, a.shape[2]), lambda bi, i: (bi, i, 0))
    return pl.pallas_call(
        functools.partial(_outproj_kernel, alpha=alpha),
        out_shape=jax.ShapeDtypeStruct((b, t, d), F32),
        grid=(b, t // tm),
        in_specs=[bt(attn), bt(rec), bt(four), bt(x),
                  pl.BlockSpec((1, 1, d), lambda bi, i: (mod_row(bi), 0, 2)),
                  _layer(w_out, l), _full(lg.shape), _full(lb.shape)],
        out_specs=pl.BlockSpec((1, tm, d), lambda bi, i: (bi, i, 0)),
        scratch_shapes=[pltpu.VMEM((tm, d), F32)],
        compiler_params=_cparams("parallel", "parallel"),
        name="outproj",
    )(attn, rec, four, x, mod3, w_out, lg, lb)


def _ffn_kernel(x_ref, xp_ref, xn_ref, sh_ref, sc_ref, g2_ref, wup_ref, cw_ref, cb_ref, wdn_ref, lg_ref, lb_ref,
                o_ref, h_scr, u_scr, act_scr, y_scr, *, alpha):
    i = pl.program_id(1)
    last = pl.num_programs(1) - 1
    tm = x_ref.shape[1]
    halo = BF16_ROWS
    sc = sc_ref[0]
    sh = sh_ref[0]

    def mod(v):
        return (v * (1.0 + sc) + sh).astype(BF16)

    h_scr[0:halo, :] = jnp.where(i > 0, mod(xp_ref[0]), jnp.zeros((), BF16))
    h_scr[halo:halo + tm, :] = mod(x_ref[0])
    h_scr[halo + tm:, :] = jnp.where(i < last, mod(xn_ref[0]), jnp.zeros((), BF16))
    d_ff = wdn_ref.shape[0]
    cf = u_scr.shape[3]

    def conv(ub, cols):
        cw = cw_ref[:, cols]
        u = ub[...]
        rows = u.shape[0]
        prev = pltpu.roll(u, 1, 0)[halo:halo + tm]
        nxt = pltpu.roll(u, rows - 1, 0)[halo:halo + tm]
        return prev * cw[0:1, :] + u[halo:halo + tm] * cw[1:2, :] + nxt * cw[2:3, :] + cb_ref[:, cols]

    for j in range(d_ff // cf):
        val_cols = slice(j * cf, (j + 1) * cf)
        gate_cols = slice(d_ff + j * cf, d_ff + (j + 1) * cf)
        uv, ug = u_scr.at[j % 2, 0], u_scr.at[j % 2, 1]
        uv[...] = _dot(h_scr[...], wup_ref[:, val_cols])
        ug[...] = _dot(h_scr[...], wup_ref[:, gate_cols])
        act_scr[:, val_cols] = (conv(uv, val_cols) * _silu(conv(ug, gate_cols))).astype(BF16)
    nsub = 2 if tm % (2 * BF16_ROWS) == 0 else 1
    sub = tm // nsub
    for step in range(nsub + 1):
        if step < nsub:
            rows = slice(step * sub, (step + 1) * sub)
            y_scr[rows, :] = _dot(act_scr[rows, :], wdn_ref[...])
        if step >= 1:
            rows = slice((step - 1) * sub, step * sub)
            o_ref[0, rows, :] = _layer_norm(alpha * x_ref[0, rows, :] + g2_ref[0] * y_scr[rows, :],
                                            lg_ref[...], lb_ref[...])


def _ffn(x1, mod3, mod_row, l, wup, cw, cb, wdn, lg, lb, *, alpha, tm, cf):
    b, t, d = x1.shape
    halo = BF16_ROWS
    per = tm // halo
    nblk = t // halo
    return pl.pallas_call(
        functools.partial(_ffn_kernel, alpha=alpha),
        out_shape=jax.ShapeDtypeStruct((b, t, d), F32),
        grid=(b, t // tm),
        in_specs=[
            pl.BlockSpec((1, tm, d), lambda bi, i: (bi, i, 0)),
            pl.BlockSpec((1, halo, d), lambda bi, i: (bi, jnp.maximum(i * per - 1, 0), 0)),
            pl.BlockSpec((1, halo, d), lambda bi, i: (bi, jnp.minimum((i + 1) * per, nblk - 1), 0)),
            pl.BlockSpec((1, 1, d), lambda bi, i: (mod_row(bi), 0, 3)),
            pl.BlockSpec((1, 1, d), lambda bi, i: (mod_row(bi), 0, 4)),
            pl.BlockSpec((1, 1, d), lambda bi, i: (mod_row(bi), 0, 5)),
            _layer(wup, l, 1), _full(cw.shape), _full(cb.shape), _layer(wdn, l, 1), _full(lg.shape),
            _full(lb.shape),
        ],
        out_specs=pl.BlockSpec((1, tm, d), lambda bi, i: (bi, i, 0)),
        scratch_shapes=[pltpu.VMEM((tm + 2 * halo, d), BF16), pltpu.VMEM((2, 2, tm + 2 * halo, cf), F32),
                        pltpu.VMEM((tm, wdn.shape[1]), BF16), pltpu.VMEM((tm, d), F32)],
        compiler_params=_cparams("parallel", "arbitrary"),
        name="ffn",
    )(x1, x1, x1, mod3, mod3, mod3, wup, cw, cb, wdn, lg, lb)


def _block_diag_mean(width, seg):
    i = np.arange(width)
    return jnp.asarray(((i[:, None] // seg) == (i[None, :] // seg)).astype(np.float32) / seg, dtype=BF16)


def _hg_tri(n):
    i = np.arange(n)
    same = (i[:, None] // HG_CHUNK) == (i[None, :] // HG_CHUNK)
    lower = same & (i[None, :] <= i[:, None])
    upper = same & (i[None, :] >= i[:, None])
    return jnp.asarray(np.stack([lower, upper]).astype(np.float32), dtype=BF16)


def _hg_select(w):
    r = np.arange(HG_BLOCK * w)
    c = np.arange(w)
    sg, head_r = r // w, (r % w) // HG_DIM
    head_c, s = c // HG_DIM, c % HG_DIM
    m = (head_r[:, None] == head_c[None, :]) & ((s[None, :] % HG_BLOCK) == sg[:, None])
    return jnp.asarray(m.astype(np.float32), dtype=BF16)


def _position_dft(t):
    lo = 64
    f = jnp.arange(t, dtype=jnp.int32)[:, None]
    step = 2.0 * np.pi / t
    a = ((f * (lo * jnp.arange(t // lo, dtype=jnp.int32))[None, :]) % t).astype(F32) * step
    bb = ((f * jnp.arange(lo, dtype=jnp.int32)[None, :]) % t).astype(F32) * step
    ca, sa = jnp.cos(a)[:, :, None], jnp.sin(a)[:, :, None]
    cb, sb = jnp.cos(bb)[:, None, :], jnp.sin(bb)[:, None, :]
    norm = 1.0 / np.sqrt(t)
    cos = ((ca * cb - sa * sb) * norm).astype(BF16).reshape(t, t)
    msin = ((sa * cb + ca * sb) * -norm).astype(BF16).reshape(t, t)
    return cos, msin


def _channel_dft(groups):
    c = np.arange(FT_DIM, dtype=np.int64)
    angc = 2.0 * np.pi * ((c[:, None] * c[None, :]) % FT_DIM).astype(np.float64) / FT_DIM
    eye = np.eye(groups)
    dftc = np.concatenate([np.kron(eye, np.cos(angc)), np.kron(eye, np.sin(angc))], axis=1) / np.sqrt(FT_DIM)
    return jnp.asarray(dftc, dtype=F32).astype(BF16)


def _rope_tables(t):
    rows = t // GRID_W
    row = jnp.repeat(jnp.arange(rows, dtype=jnp.int32), GRID_W)
    col = jnp.tile(jnp.arange(GRID_W, dtype=jnp.int32), rows)
    nf = HEAD_DIM // 4
    inv = ROPE_THETA ** (-jnp.arange(nf, dtype=F32) / nf)
    ar = row.astype(F32)[:, None] * inv
    ac = col.astype(F32)[:, None] * inv
    cos = jnp.concatenate([jnp.cos(ar)] * 2 + [jnp.cos(ac)] * 2, axis=1)
    sin = jnp.concatenate([-jnp.sin(ar), jnp.sin(ar), -jnp.sin(ac), jnp.sin(ac)], axis=1)
    return jnp.tile(cos, (1, 128 // HEAD_DIM)), jnp.tile(sin, (1, 128 // HEAD_DIM))


def kernel(x, c, ctx, c_ctx, w_ada, b_ada, w_in, q_norm, k_norm, hg_lb, hg_norm, ft_w, w_out, ln1_g, ln1_b, w_up,
           conv_w, conv_b, w_down, ln2_g, ln2_b):
    b, t, d = x.shape
    tc = ctx.shape[1]
    depth = w_ada.shape[0]
    d_ff = w_down.shape[1]
    hgw = hg_lb.shape[2]
    ftw = ft_w.shape[1] * ft_w.shape[2]
    kvw = ATTN_KV_HEADS * HEAD_DIM
    aw = w_in.shape[2] - 2 * kvw - 5 * hgw - ftw
    alpha = (2 * depth) ** 0.25

    sizes = [("q", aw), ("k", kvw), ("v", kvw), ("hq", hgw), ("hi", hgw), ("hf", 2 * hgw), ("hg", hgw), ("ft", ftw)]
    offs, lo = {}, 0
    for name, size in sizes:
        offs[name] = (lo, lo + size)
        lo += size

    mod_rows = -(-(b + 1) // 8) * 8
    cond = jnp.concatenate([c, c_ctx[None, :], jnp.zeros((mod_rows - b - 1, d), F32)], axis=0)
    mod = _ada(cond, w_ada, b_ada)
    mod3 = mod.reshape(depth * mod_rows, 1, 6 * d)

    lb_soft = jax.nn.softmax(hg_lb.astype(F32), axis=0)
    lower = (jnp.cumsum(lb_soft, axis=0) - lb_soft[:1]).reshape(depth, 2, 1, hgw)

    cos, sin = _rope_tables(t)
    bdq = _block_diag_mean(2 * 128, HEAD_DIM)
    bdk = _block_diag_mean(kvw, HEAD_DIM)
    bdh = _block_diag_mean(hgw, HG_DIM)
    tri = _hg_tri(min(HG_TILE, t))
    sel = _hg_select(hgw)
    dftc = _channel_dft(ftw // FT_DIM)
    cs_lat = _position_dft(t)
    cs_ctx = _position_dft(tc)

    cf = 256
    assert d_ff % cf == 0
    w_in_b, w_out_b, w_up_b, w_down_b = (a.astype(BF16) for a in (w_in, w_out, w_up, w_down))

    for l in range(depth):
        need_ctx = l < depth - 1
        lat_row = lambda bi, l=l: l * mod_rows + bi
        ctx_row = lambda bi, l=l: l * mod_rows + b

        qg = jnp.tile(q_norm[l].astype(F32), aw // HEAD_DIM)[None, :]
        kg = jnp.tile(k_norm[l].astype(F32), kvw // HEAD_DIM)[None, :]
        proj_args = (w_in_b, cos, sin, qg, kg, bdq, bdk, dftc)
        q, k, v, hq, hi, hg, hf, hit, fab = _inproj(x, mod3, lat_row, l, *proj_args, offs=offs, rope=True, tm=512)
        qc, kc, vc, hqc, hic, hgc, hfc, hitc, fabc = _inproj(ctx, mod3, ctx_row, l, *proj_args, offs=offs,
                                                               rope=False, tm=tc)

        attn = _attention(q, jnp.concatenate([kc, k], axis=3), jnp.concatenate([vc, v], axis=2), tq=512)

        gn = jnp.tile(hg_norm[l].astype(F32), hgw // HG_DIM)[None, :]
        rec, rec_c = _hgrn((hq, hi, hit, hf, hg), (hqc, hic, hitc, hfc, hgc), lower[l], gn, tri, sel, bdh,
                           need_ctx=need_ctx)

        wbd = jax.scipy.linalg.block_diag(*[ft_w[l, g] for g in range(ft_w.shape[1])]).astype(BF16)
        four = _dft(cs_lat, fab, wbd, tf=512)

        lg1, lb1 = ln1_g[l][None, :], ln1_b[l][None, :]
        lg2, lb2 = ln2_g[l][None, :], ln2_b[l][None, :]
        ffn_args = (l, w_up_b, conv_w[l], conv_b[l][None, :], w_down_b, lg2, lb2)

        x1 = _outproj(attn, rec, four, x, mod3, lat_row, l, w_out_b, lg1, lb1, alpha=alpha, tm=512)
        x_next = _ffn(x1, mod3, lat_row, *ffn_args, alpha=alpha, tm=512, cf=cf)

        if need_ctx:
            attn_c = _attention(qc, kc, vc, tq=tc)
            four_c = _dft(cs_ctx, fabc, wbd, tf=tc)
            ctx1 = _outproj(attn_c, rec_c, four_c, ctx, mod3, ctx_row, l, w_out_b, lg1, lb1, alpha=alpha, tm=tc)
            ctx = _ffn(ctx1, mod3, ctx_row, *ffn_args, alpha=alpha, tm=tc, cf=cf)
        x = x_next

    return x
```

```python
import functools

import numpy as np
import jax
import jax.numpy as jnp
from jax import lax
from jax.experimental import pallas as pl
from jax.experimental.pallas import tpu as pltpu

F32 = jnp.float32
BF16 = jnp.bfloat16

GRID_W = 64
HEAD_DIM = 64
ATTN_KV_HEADS = 2
HG_DIM = 64
FT_DIM = 64
ROPE_THETA = 10000.0
F_MIN = 1e-30
EPS = 1e-6
CONV_W = 3

HG_CHUNK = 64
HG_BLOCK = 16
HG_NB = HG_CHUNK // HG_BLOCK
HG_PAIR = 2 * HG_DIM
HG_TILE = 512
NEG = -1e30
LOG2_E = 1.4426950408889634

ROW_TILE = 512
FFN_COLS = 256

V7X_VMEM_BYTES = 64 * 1024 * 1024
VMEM_SHARE = 0.85
VMEM_LIMIT = int(V7X_VMEM_BYTES * VMEM_SHARE)
BF16_ROWS = 16

_NT = (((1,), (1,)), ((), ()))


def _cparams(*sem):
    return pltpu.CompilerParams(dimension_semantics=sem, vmem_limit_bytes=VMEM_LIMIT)


def _dot(a, b):
    return jnp.dot(a, b, preferred_element_type=F32)


def _silu(x):
    return x * jax.nn.sigmoid(x)


def _layer_norm(y, g, b):
    mu = jnp.mean(y, axis=-1, keepdims=True)
    d = y - mu
    var = jnp.mean(d * d, axis=-1, keepdims=True)
    return d * lax.rsqrt(var + EPS) * g + b


def _full(shape):
    n = len(shape)
    return pl.BlockSpec(shape, lambda *_: (0,) * n)


def _layer(w, l, buffers=2):
    n = w.ndim - 1
    return pl.BlockSpec((None,) + w.shape[1:], lambda *_: (l,) + (0,) * n, pipeline_mode=pl.Buffered(buffers))


def _ada_kernel(c_ref, w_ref, b_ref, o_ref):
    s = _silu(c_ref[...]).astype(BF16)
    o_ref[0] = _dot(s, w_ref[0].astype(BF16)) + b_ref[0]


def _ada(cond, w_ada, b_ada):
    depth, d, n = w_ada.shape
    rows = cond.shape[0]
    tn = n // 4
    return pl.pallas_call(
        _ada_kernel,
        out_shape=jax.ShapeDtypeStruct((depth, rows, n), F32),
        grid=(depth, n // tn),
        in_specs=[
            pl.BlockSpec((rows, d), lambda l, j: (0, 0)),
            pl.BlockSpec((1, d, tn), lambda l, j: (l, 0, j)),
            pl.BlockSpec((1, 1, tn), lambda l, j: (l, 0, j)),
        ],
        out_specs=pl.BlockSpec((1, rows, tn), lambda l, j: (l, 0, j)),
        compiler_params=_cparams("arbitrary", "arbitrary"),
        name="ada",
    )(cond, w_ada, b_ada.reshape(depth, 1, n))


def _rope128(y, cos, sin_signed):
    lane = lax.broadcasted_iota(jnp.int32, y.shape, 1)
    upper = (lane % 32) >= 16
    partner = jnp.where(upper, pltpu.roll(y, 16, 1), pltpu.roll(y, 112, 1))
    return y * cos + partner * sin_signed


def _inproj_kernel(x_ref, sh_ref, sc_ref, w_ref, cos_ref, sin_ref, qg_ref, kg_ref, bdq_ref, bdk_ref,
                   dftc_ref, q_ref, k_ref, v_ref, hq_ref, hi_ref, hg_ref, hf_ref, hit_ref, fab_ref, h_scr, qk_scr,
                   *, rope, offs):
    h_scr[...] = (x_ref[0] * (1.0 + sc_ref[0]) + sh_ref[0]).astype(BF16)

    def proj(name):
        lo, hi = offs[name]
        return _dot(h_scr[...], w_ref[:, lo:hi])

    q_lo, q_hi = offs["q"]
    k_lo, k_hi = offs["k"]

    def norm_rope(lo, hi, bd_ref, g_ref):
        base = q_lo if lo < q_hi else k_lo
        y = qk_scr[:, lo - q_lo:hi - q_lo]
        y = y * lax.rsqrt(_dot((y * y).astype(BF16), bd_ref[...]) + EPS) * g_ref[:, lo - base:hi - base]
        if rope:
            cos, sin = cos_ref[...], sin_ref[...]
            y = jnp.concatenate([_rope128(y[:, c:c + 128], cos, sin) for c in range(0, hi - lo, 128)], axis=1)
        return y

    q_scale = HEAD_DIM ** -0.5 * LOG2_E

    def q_part(lo, hi):
        q_ref[0, :, lo - q_lo:hi - q_lo] = (norm_rope(lo, hi, bdq_ref, qg_ref) * q_scale).astype(BF16)

    qk_scr[...] = _dot(h_scr[...], w_ref[:, q_lo:k_hi])
    v = proj("v")
    lane = lax.broadcasted_iota(jnp.int32, v.shape, 1)
    for hh in range(v.shape[1] // HEAD_DIM):
        vh = v if hh == 0 else pltpu.roll(v, v.shape[1] - hh * HEAD_DIM, 1)
        v_ref[0, hh] = jnp.where(lane < HEAD_DIM, vh, jnp.where(lane == HEAD_DIM, 1.0, 0.0)).astype(BF16)
    hq_ref[0] = proj("hq").astype(BF16)
    qw = bdq_ref.shape[0]
    q_part(q_lo, q_lo + qw)
    hi = proj("hi")
    hi_ref[0] = hi.astype(BF16)
    vt = hi.T.astype(BF16)
    for j in range(vt.shape[1] // HG_CHUNK):
        hit_ref[0, j] = vt[:, j * HG_CHUNK:(j + 1) * HG_CHUNK]
    for lo in range(q_lo + qw, q_hi, qw):
        q_part(lo, lo + qw)
    hg_ref[0] = proj("hg").astype(BF16)
    kt = norm_rope(k_lo, k_hi, bdk_ref, kg_ref).T.astype(BF16)
    for hh in range(kt.shape[0] // HEAD_DIM):
        k_ref[0, hh] = kt[hh * HEAD_DIM:(hh + 1) * HEAD_DIM, :]
    ab = _dot(proj("ft").astype(BF16), dftc_ref[...])
    half = ab.shape[1] // 2
    fab_ref[0, 0] = ab[:, :half].astype(BF16)
    fab_ref[0, 1] = ab[:, half:].astype(BF16)
    hf_ref[0] = proj("hf")


def _inproj(x, mod3, mod_row, l, w_in, cos, sin, qg, kg, bdq, bdk, dftc, *, offs, rope, tm):
    b, t, d = x.shape
    aw = offs["q"][1] - offs["q"][0]
    kw = offs["k"][1] - offs["k"][0]
    hw = offs["hq"][1] - offs["hq"][0]
    fw = offs["ft"][1] - offs["ft"][0]
    row = lambda bi: mod_row(bi)
    bt = lambda width: pl.BlockSpec((1, tm, width), lambda bi, i: (bi, i, 0))
    kvh = kw // HEAD_DIM
    out_shapes = (
        jax.ShapeDtypeStruct((b, t, aw), BF16), jax.ShapeDtypeStruct((b, kvh, HEAD_DIM, t), BF16),
        jax.ShapeDtypeStruct((b, kvh, t, 2 * HEAD_DIM), BF16), jax.ShapeDtypeStruct((b, t, hw), BF16),
        jax.ShapeDtypeStruct((b, t, hw), BF16), jax.ShapeDtypeStruct((b, t, hw), BF16),
        jax.ShapeDtypeStruct((b, t, 2 * hw), F32),
        jax.ShapeDtypeStruct((b, t // HG_CHUNK, hw, HG_CHUNK), BF16),
        jax.ShapeDtypeStruct((b, 2, t, fw), BF16),
    )
    out_specs = (
        bt(aw),
        pl.BlockSpec((1, kvh, HEAD_DIM, tm), lambda bi, i: (bi, 0, 0, i)),
        pl.BlockSpec((1, kvh, tm, 2 * HEAD_DIM), lambda bi, i: (bi, 0, i, 0)),
        bt(hw), bt(hw), bt(hw), bt(2 * hw),
        pl.BlockSpec((1, tm // HG_CHUNK, hw, HG_CHUNK), lambda bi, i: (bi, i, 0, 0)),
        pl.BlockSpec((1, 2, tm, fw), lambda bi, i: (bi, 0, i, 0)),
    )
    return pl.pallas_call(
        functools.partial(_inproj_kernel, rope=rope, offs=offs),
        out_shape=out_shapes,
        grid=(b, t // tm),
        in_specs=[
            pl.BlockSpec((1, tm, d), lambda bi, i: (bi, i, 0)),
            pl.BlockSpec((1, 1, d), lambda bi, i: (row(bi), 0, 0)),
            pl.BlockSpec((1, 1, d), lambda bi, i: (row(bi), 0, 1)),
            _layer(w_in, l),
            pl.BlockSpec((tm, 128), lambda bi, i: (i, 0)),
            pl.BlockSpec((tm, 128), lambda bi, i: (i, 0)),
            _full(qg.shape), _full(kg.shape), _full(bdq.shape), _full(bdk.shape), _full(dftc.shape),
        ],
        out_specs=out_specs,
        scratch_shapes=[pltpu.VMEM((tm, d), BF16), pltpu.VMEM((tm, aw + kw), F32)],
        compiler_params=_cparams("parallel", "parallel"),
        name="inproj",
    )(x, mod3, mod3, w_in, cos, sin, qg, kg, bdq, bdk, dftc)


def _attn_kernel(q_ref, kt_ref, v_ref, o_ref, s_scr, m_scr, p_scr, *, group):
    kt = kt_ref[0, 0]
    v = v_ref[0, 0]
    half = (v.shape[0] // 2) // 256 * 256
    tq = q_ref.shape[1]
    nrg = 2 if tq % 512 == 0 else 1
    rg = tq // nrg
    units = [(h, r) for h in range(group) for r in range(nrg)]

    def scores(h, r):
        rows = slice(r * rg, (r + 1) * rg)
        s = _dot(q_ref[0, rows, h * HEAD_DIM:(h + 1) * HEAD_DIM], kt)
        s_scr[h, rows, :] = s
        m_scr[h, rows, :] = jnp.max(s, axis=-1, keepdims=True)

    def probs(h, r):
        rows = slice(r * rg, (r + 1) * rg)
        p_scr[h, rows, :] = jnp.exp2(s_scr[h, rows, :] - m_scr[h, rows, :]).astype(BF16)

    def values(h, r):
        rows = slice(r * rg, (r + 1) * rg)
        if half:
            pv = _dot(p_scr[h, rows, :half], v[:half]) + _dot(p_scr[h, rows, half:], v[half:])
        else:
            pv = _dot(p_scr[h, rows, :], v)
        return pv[:, :HEAD_DIM] / pv[:, HEAD_DIM:HEAD_DIM + 1]

    outs = {}
    for step in range(len(units) + 2):
        if step < len(units):
            scores(*units[step])
        if 0 <= step - 1 < len(units):
            probs(*units[step - 1])
        if 0 <= step - 2 < len(units):
            outs[units[step - 2]] = values(*units[step - 2])
    o_ref[0] = jnp.concatenate(
        [jnp.concatenate([outs[(h, r)] for r in range(nrg)], axis=0) for h in range(group)], axis=1).astype(BF16)


def _attention(q, kt_all, v_all, *, tq):
    b, t, aw = q.shape
    kvh, s = v_all.shape[1], v_all.shape[2]
    group = aw // HEAD_DIM // kvh
    gw = group * HEAD_DIM
    return pl.pallas_call(
        functools.partial(_attn_kernel, group=group),
        out_shape=jax.ShapeDtypeStruct((b, t, aw), BF16),
        grid=(b, kvh, t // tq),
        in_specs=[
            pl.BlockSpec((1, tq, gw), lambda bi, g, i: (bi, i, g)),
            pl.BlockSpec((1, 1, HEAD_DIM, s), lambda bi, g, i: (bi, g, 0, 0)),
            pl.BlockSpec((1, 1, s, 2 * HEAD_DIM), lambda bi, g, i: (bi, g, 0, 0)),
        ],
        out_specs=pl.BlockSpec((1, tq, gw), lambda bi, g, i: (bi, i, g)),
        scratch_shapes=[pltpu.VMEM((group, tq, s), F32), pltpu.VMEM((group, tq, 1), F32),
                        pltpu.VMEM((group, tq, s), BF16)],
        compiler_params=_cparams("parallel", "parallel", "parallel"),
        name="attention",
    )(q, kt_all, v_all)


def _hg_prepare(rev, hq, z, v, hit_ref, c0, lb, tri_ref, sel_ref, o_ref, row0, qb_ref, qs_ref, ks_ref, si_ref, dl_ref,
                d_ref):
    r, w = z.shape
    n = r // HG_CHUNK
    f = lb + (1.0 - lb) * jax.nn.sigmoid(z)
    logf = jnp.log2(jnp.maximum(f, F_MIN))
    kk = 1.0 - f
    qs = _silu(hq.astype(F32))
    hi = logf.astype(BF16)
    lo = (logf - hi.astype(F32)).astype(BF16)
    tri = tri_ref[0:r, 0:r]
    cum = _dot(tri, hi) + _dot(tri, lo)
    yield "local"

    c3 = cum.reshape(n, HG_CHUNK, w)
    q3 = qs.reshape(n, HG_CHUNK, w)
    k3 = kk.reshape(n, HG_CHUNK, w)

    def rows(a, lo_, hi_):
        parts = []
        if lo_ > 0:
            parts.append(jnp.zeros((n, lo_, w), F32))
        parts.append(a)
        if hi_ < HG_CHUNK:
            parts.append(jnp.zeros((n, HG_CHUNK - hi_, w), F32))
        return jnp.concatenate(parts, axis=1) if len(parts) > 1 else a

    e_last = c3[:, 0:1, :] if rev else c3[:, HG_CHUNK - 1:HG_CHUNK, :]
    qparts, kparts = [], []
    for g in range(1, HG_NB):
        j = HG_NB - g if rev else g - 1
        b0, b1 = j * HG_BLOCK, (j + 1) * HG_BLOCK
        e = c3[:, b0:b0 + 1, :] if rev else c3[:, b1 - 1:b1, :]
        p0, p1 = (0, b0) if rev else (b1, HG_CHUNK)
        qparts.append(rows(q3[:, p0:p1, :] * jnp.exp2(c3[:, p0:p1, :] - e), p0, p1))
        kparts.append(rows(k3[:, b0:b1, :] * jnp.exp2(e - c3[:, b0:b1, :]), b0, b1))
    qparts.append(q3 * jnp.exp2(c3))
    kparts.append(k3 * jnp.exp2(e_last - c3))
    dl = jnp.exp2(e_last).reshape(n, w)
    pairs = w // HG_PAIR
    ng = HG_NB - 1
    for pp in range(pairs):
        lanes = slice(pp * HG_PAIR, (pp + 1) * HG_PAIR)
        qb_ref[pp, 0:r, :] = jnp.concatenate([p[:, :, lanes] for p in qparts[:ng]], axis=-1).reshape(
            r, ng * HG_PAIR).astype(BF16)
        qs_ref[pp, 0:r, :] = qparts[ng][:, :, lanes].reshape(r, HG_PAIR).astype(BF16)
        ks_ref[pp, 0:r, :] = jnp.concatenate([p[:, :, lanes] for p in kparts], axis=-1).reshape(
            r, HG_NB * HG_PAIR).astype(BF16)
        dl_ref[pp, 0:n, :] = dl[:, lanes]
        yield "local"

    nb = r // HG_BLOCK
    hb = HG_BLOCK // 2
    c5 = cum.reshape(nb, 2, hb, w)
    q5 = qs.reshape(nb, 2, hb, w)
    d5 = (cum - jnp.log2(jnp.maximum(kk, 0.0))).reshape(nb, 2, hb, w)
    tau = lax.broadcasted_iota(jnp.int32, (nb, hb, w), 1)
    halves = []
    for half in range(2):
        sources = [sg for sg in range(HG_BLOCK) if not ((half > sg // hb) if rev else (half < sg // hb))]
        eparts = []
        for sg in sources:
            sh, sl = sg // hb, sg % hb
            arg = c5[:, half] - d5[:, sh, sl:sl + 1, :]
            if half == sh:
                arg = jnp.where((tau <= sl) if rev else (tau >= sl), arg, NEG)
            eparts.append((jnp.exp2(arg) * q5[:, half]).reshape(nb * hb, w).astype(BF16))
            if len(eparts) % 2 == 0:
                yield "local"
        lo_ = sources[0] * w
        halves.append(_dot(jnp.concatenate(eparts, axis=-1), sel_ref[lo_:lo_ + len(sources) * w, :]))
        yield "local"
    a = jnp.stack([h_.reshape(nb, hb, w) for h_ in halves], axis=1).reshape(r, w)
    ti = lax.broadcasted_iota(jnp.int32, (r, w), 0)
    ci = lax.broadcasted_iota(jnp.int32, (r, w), 1)
    same_block = ((ti % HG_CHUNK) // HG_BLOCK) == ((ci % HG_CHUNK) // HG_BLOCK)
    ad = jnp.where(same_block, a, 0.0).astype(BF16)

    def same_head(shape):
        return (lax.broadcasted_iota(jnp.int32, shape, 0) // HG_DIM) == (lax.broadcasted_iota(jnp.int32, shape, 1) // HG_DIM)

    pair_mask = same_head((HG_PAIR, HG_PAIR))
    head_mask = same_head((w, w))
    for c in range(n):
        rws = slice(c * HG_CHUNK, (c + 1) * HG_CHUNK)
        vt = hit_ref[0, c0 + c]
        for pp in range(pairs):
            x = _dot(vt[pp * HG_PAIR:(pp + 1) * HG_PAIR, :], ks_ref[pp, rws, :])
            d_ref[c, pp] = jnp.concatenate(
                [jnp.where(pair_mask, x[:, g * HG_PAIR:(g + 1) * HG_PAIR], 0.0).astype(BF16) for g in range(ng)],
                axis=1)
            si_ref[c, pp] = jnp.where(pair_mask, x[:, ng * HG_PAIR:], 0.0)
            yield "products"
    for c in range(n):
        rws = slice(c * HG_CHUNK, (c + 1) * HG_CHUNK)
        outs = [lax.dot_general(qb_ref[pp, rws, :], d_ref[c, pp], _NT, preferred_element_type=F32)
                for pp in range(pairs)]
        vc = v[rws]
        vbd = jnp.where(head_mask, jnp.concatenate([vc] * (w // HG_CHUNK), axis=0), jnp.zeros((), vc.dtype))
        o_ref[row0 + c * HG_CHUNK:row0 + (c + 1) * HG_CHUNK, :] = jnp.concatenate(outs, axis=1) + _dot(ad[rws], vbd)
        yield "products"


def _run_interleaved(first, second):
    live = [first, second]
    while live:
        for g in list(live):
            if next(g, None) is None:
                live.remove(g)


def _hg_step(qs_ref, si_ref, dl_ref, st_ref, o_ref, r0, g0, ci):
    outs = []
    for pp in range(st_ref.shape[0]):
        st = st_ref[pp]
        outs.append(lax.dot_general(qs_ref[pp, pl.ds(r0, HG_CHUNK), :], st.astype(BF16), _NT,
                                    preferred_element_type=F32))
        st_ref[pp] = st * dl_ref[pp, pl.ds(ci, 1), :] + si_ref[ci, pp]
    o_ref[pl.ds(g0, HG_CHUNK), :] += jnp.concatenate(outs, axis=1)


def _hgrn_kernel(hq_ref, hi_ref, hit_ref, hf_ref, hg_ref, hqc_ref, hic_ref, hitc_ref, hfc_ref, hgc_ref,
                 lb_ref, gn_ref, tri_ref, sel_ref, bd_ref, *rest, need_ctx):
    if need_ctx:
        rec_ref, recc_ref = rest[0], rest[1]
        scr = rest[2:]
    else:
        rec_ref, recc_ref = rest[0], None
        scr = rest[1:]
    of_ref, ob_ref, ofc_ref, obc_ref, stf_ref, stb_ref = scr[:6]
    fwd, bwd = scr[6:12], scr[12:18]
    w = hq_ref.shape[2]
    lb_f = lb_ref[0]
    lb_b = lb_ref[1]
    stf_ref[...] = jnp.zeros_like(stf_ref)
    stb_ref[...] = jnp.zeros_like(stb_ref)

    def run(hq, hi, hit, hf, o_f, o_b):
        t = hq.shape[1]
        tile = min(HG_TILE, t)
        nt = t // tile
        cpt = tile // HG_CHUNK
        for s in range(nt):
            rf = s * tile
            rb = (nt - 1 - s) * tile
            _run_interleaved(
                _hg_prepare(False, hq[0, rf:rf + tile, :], hf[0, rf:rf + tile, 0:w], hi[0, rf:rf + tile, :], hit,
                            rf // HG_CHUNK, lb_f, tri_ref.at[0], sel_ref, o_f, rf, *fwd),
                _hg_prepare(True, hq[0, rb:rb + tile, :], hf[0, rb:rb + tile, w:2 * w], hi[0, rb:rb + tile, :], hit,
                            rb // HG_CHUNK, lb_b, tri_ref.at[1], sel_ref, o_b, rb, *bwd))

            for i in range(cpt):
                cf, cb = i, cpt - 1 - i
                _hg_step(fwd[1], fwd[3], fwd[4], stf_ref, o_f, cf * HG_CHUNK, rf + cf * HG_CHUNK, cf)
                _hg_step(bwd[1], bwd[3], bwd[4], stb_ref, o_b, cb * HG_CHUNK, rb + cb * HG_CHUNK, cb)

    def finish(o_f, o_b, hg, out):
        t = hg.shape[1]
        tile = min(HG_TILE, t)
        for s in range(t // tile):
            rows = slice(s * tile, (s + 1) * tile)
            o = o_f[rows, :] + o_b[rows, :]
            ms = _dot((o * o).astype(BF16), bd_ref[...])
            on = o * lax.rsqrt(ms + EPS) * gn_ref[...]
            out[0, rows, :] = (on * _silu(hg[0, rows, :].astype(F32))).astype(BF16)

    run(hqc_ref, hic_ref, hitc_ref, hfc_ref, ofc_ref, obc_ref)
    run(hq_ref, hi_ref, hit_ref, hf_ref, of_ref, ob_ref)
    finish(of_ref, ob_ref, hg_ref, rec_ref)
    if need_ctx:
        finish(ofc_ref, obc_ref, hgc_ref, recc_ref)


def _hgrn(lat, cx, lb, gn, tri, sel, bd, *, need_ctx):
    hq, hi, hit, hf, hg = lat
    hqc, hic, hitc, hfc, hgc = cx
    b, t, w = hq.shape
    tc = hqc.shape[1]
    tile = min(HG_TILE, t)

    def seq(a):
        nd = a.ndim
        return pl.BlockSpec((1,) + a.shape[1:], lambda bi: (bi,) + (0,) * (nd - 1))

    out_shape = [jax.ShapeDtypeStruct((b, t, w), BF16)]
    out_specs = [pl.BlockSpec((1, t, w), lambda bi: (bi, 0, 0))]
    if need_ctx:
        out_shape.append(jax.ShapeDtypeStruct((b, tc, w), BF16))
        out_specs.append(pl.BlockSpec((1, tc, w), lambda bi: (bi, 0, 0)))
    pairs = w // HG_PAIR
    cpt = tile // HG_CHUNK
    dir_scratch = [
        pltpu.VMEM((pairs, tile, (HG_NB - 1) * HG_PAIR), BF16),
        pltpu.VMEM((pairs, tile, HG_PAIR), BF16),
        pltpu.VMEM((pairs, tile, HG_NB * HG_PAIR), BF16),
        pltpu.VMEM((cpt, pairs, HG_PAIR, HG_PAIR), F32),
        pltpu.VMEM((pairs, cpt, HG_PAIR), F32),
        pltpu.VMEM((cpt, pairs, HG_PAIR, (HG_NB - 1) * HG_PAIR), BF16),
    ]
    res = pl.pallas_call(
        functools.partial(_hgrn_kernel, need_ctx=need_ctx),
        out_shape=tuple(out_shape),
        grid=(b,),
        in_specs=[seq(hq), seq(hi), seq(hit), seq(hf), seq(hg), seq(hqc), seq(hic), seq(hitc), seq(hfc), seq(hgc),
                  _full(lb.shape), _full(gn.shape), _full(tri.shape), _full(sel.shape), _full(bd.shape)],
        out_specs=tuple(out_specs),
        scratch_shapes=[pltpu.VMEM((t, w), F32), pltpu.VMEM((t, w), F32),
                        pltpu.VMEM((tc, w), F32), pltpu.VMEM((tc, w), F32),
                        pltpu.VMEM((pairs, HG_PAIR, HG_PAIR), F32), pltpu.VMEM((pairs, HG_PAIR, HG_PAIR), F32)]
        + dir_scratch + dir_scratch,
        compiler_params=_cparams("parallel"),
        name="hgrn",
    )(hq, hi, hit, hf, hg, hqc, hic, hitc, hfc, hgc, lb, gn, tri, sel, bd)
    return res if need_ctx else (res[0], None)


def _dft_kernel(cos_ref, msin_ref, ab_ref, w_ref, o_ref, z_scr):
    tf = cos_ref.shape[0]
    nsub = 2 if tf % (2 * BF16_ROWS) == 0 else 1
    sub = tf // nsub
    for step in range(nsub + 1):
        if step < nsub:
            rows = slice(step * sub, (step + 1) * sub)
            z_scr[rows, :] = (_dot(cos_ref[rows, :], ab_ref[0, 0]) + _dot(msin_ref[rows, :], ab_ref[0, 1])).astype(BF16)
        if step >= 1:
            rows = slice((step - 1) * sub, step * sub)
            o_ref[0, rows, :] = _dot(z_scr[rows, :], w_ref[...]).astype(BF16)


def _dft(cs, fab, wbd, *, tf):
    b, _, t, w = fab.shape
    return pl.pallas_call(
        _dft_kernel,
        out_shape=jax.ShapeDtypeStruct((b, t, w), BF16),
        grid=(t // tf, b),
        in_specs=[
            pl.BlockSpec((tf, t), lambda i, bi: (i, 0)),
            pl.BlockSpec((tf, t), lambda i, bi: (i, 0)),
            pl.BlockSpec((1, 2, t, w), lambda i, bi: (bi, 0, 0, 0)),
            _full(wbd.shape),
        ],
        out_specs=pl.BlockSpec((1, tf, w), lambda i, bi: (bi, i, 0)),
        scratch_shapes=[pltpu.VMEM((tf, w), BF16)],
        compiler_params=_cparams("parallel", "parallel"),
        name="dft",
    )(cs[0], cs[1], fab, wbd)


def _outproj_kernel(a_ref, r_ref, f_ref, x_ref, g1_ref, w_ref, lg_ref, lb_ref, o_ref, mix_scr, *, alpha):
    aw = a_ref.shape[2]
    rw = r_ref.shape[2]
    tm = x_ref.shape[1]
    nsub = 2 if tm % (2 * BF16_ROWS) == 0 else 1
    sub = tm // nsub

    def mix(i):
        rows = slice(i * sub, (i + 1) * sub)
        mix_scr[rows, :] = (_dot(a_ref[0, rows, :], w_ref[0:aw, :]) + _dot(r_ref[0, rows, :], w_ref[aw:aw + rw, :])
                            + _dot(f_ref[0, rows, :], w_ref[aw + rw:, :]))

    def norm(i):
        rows = slice(i * sub, (i + 1) * sub)
        o_ref[0, rows, :] = _layer_norm(alpha * x_ref[0, rows, :] + g1_ref[0] * mix_scr[rows, :],
                                        lg_ref[...], lb_ref[...])

    for step in range(nsub + 1):
        if step < nsub:
            mix(step)
        if step >= 1:
            norm(step - 1)


def _outproj(attn, rec, four, x, mod3, mod_row, l, w_out, lg, lb, *, alpha, tm):
    b, t, d = x.shape
    bt = lambda a: pl.BlockSpec((1, tm, a.shape[2]), lambda bi, i: (bi, i, 0))
    return pl.pallas_call(
        functools.partial(_outproj_kernel, alpha=alpha),
        out_shape=jax.ShapeDtypeStruct((b, t, d), F32),
        grid=(b, t // tm),
        in_specs=[bt(attn), bt(rec), bt(four), bt(x),
                  pl.BlockSpec((1, 1, d), lambda bi, i: (mod_row(bi), 0, 2)),
                  _layer(w_out, l), _full(lg.shape), _full(lb.shape)],
        out_specs=pl.BlockSpec((1, tm, d), lambda bi, i: (bi, i, 0)),
        scratch_shapes=[pltpu.VMEM((tm, d), F32)],
        compiler_params=_cparams("parallel", "parallel"),
        name="outproj",
    )(attn, rec, four, x, mod3, w_out, lg, lb)


def _ffn_kernel(x_ref, xp_ref, xn_ref, sh_ref, sc_ref, g2_ref, wup_ref, cw_ref, cb_ref, wdn_ref, lg_ref, lb_ref,
                o_ref, h_scr, u_scr, act_scr, y_scr, *, alpha):
    i = pl.program_id(1)
    last = pl.num_programs(1) - 1
    tm = x_ref.shape[1]
    halo = BF16_ROWS
    sc = sc_ref[0]
    sh = sh_ref[0]

    def mod(v):
        return (v * (1.0 + sc) + sh).astype(BF16)

    h_scr[0:halo, :] = jnp.where(i > 0, mod(xp_ref[0]), jnp.zeros((), BF16))
    h_scr[halo:halo + tm, :] = mod(x_ref[0])
    h_scr[halo + tm:, :] = jnp.where(i < last, mod(xn_ref[0]), jnp.zeros((), BF16))
    d_ff = wdn_ref.shape[0]
    cf = u_scr.shape[3]

    def conv(ub, cols):
        cw = cw_ref[:, cols]
        u = ub[...]
        rows = u.shape[0]
        prev = pltpu.roll(u, 1, 0)[halo:halo + tm]
        nxt = pltpu.roll(u, rows - 1, 0)[halo:halo + tm]
        return prev * cw[0:1, :] + u[halo:halo + tm] * cw[1:2, :] + nxt * cw[2:3, :] + cb_ref[:, cols]

    for j in range(d_ff // cf):
        val_cols = slice(j * cf, (j + 1) * cf)
        gate_cols = slice(d_ff + j * cf, d_ff + (j + 1) * cf)
        uv, ug = u_scr.at[j % 2, 0], u_scr.at[j % 2, 1]
        uv[...] = _dot(h_scr[...], wup_ref[:, val_cols])
        ug[...] = _dot(h_scr[...], wup_ref[:, gate_cols])
        act_scr[:, val_cols] = (conv(uv, val_cols) * _silu(conv(ug, gate_cols))).astype(BF16)
    nsub = 2 if tm % (2 * BF16_ROWS) == 0 else 1
    sub = tm // nsub
    for step in range(nsub + 1):
        if step < nsub:
            rows = slice(step * sub, (step + 1) * sub)
            y_scr[rows, :] = _dot(act_scr[rows, :], wdn_ref[...])
        if step >= 1:
            rows = slice((step - 1) * sub, step * sub)
            o_ref[0, rows, :] = _layer_norm(alpha * x_ref[0, rows, :] + g2_ref[0] * y_scr[rows, :],
                                            lg_ref[...], lb_ref[...])


def _ffn(x1, mod3, mod_row, l, wup, cw, cb, wdn, lg, lb, *, alpha, tm, cf):
    b, t, d = x1.shape
    halo = BF16_ROWS
    per = tm // halo
    nblk = t // halo
    return pl.pallas_call(
        functools.partial(_ffn_kernel, alpha=alpha),
        out_shape=jax.ShapeDtypeStruct((b, t, d), F32),
        grid=(b, t // tm),
        in_specs=[
            pl.BlockSpec((1, tm, d), lambda bi, i: (bi, i, 0)),
            pl.BlockSpec((1, halo, d), lambda bi, i: (bi, jnp.maximum(i * per - 1, 0), 0)),
            pl.BlockSpec((1, halo, d), lambda bi, i: (bi, jnp.minimum((i + 1) * per, nblk - 1), 0)),
            pl.BlockSpec((1, 1, d), lambda bi, i: (mod_row(bi), 0, 3)),
            pl.BlockSpec((1, 1, d), lambda bi, i: (mod_row(bi), 0, 4)),
            pl.BlockSpec((1, 1, d), lambda bi, i: (mod_row(bi), 0, 5)),
            _layer(wup, l, 1), _full(cw.shape), _full(cb.shape), _layer(wdn, l, 1), _full(lg.shape),
            _full(lb.shape),
        ],
        out_specs=pl.BlockSpec((1, tm, d), lambda bi, i: (bi, i, 0)),
        scratch_shapes=[pltpu.VMEM((tm + 2 * halo, d), BF16), pltpu.VMEM((2, 2, tm + 2 * halo, cf), F32),
                        pltpu.VMEM((tm, wdn.shape[1]), BF16), pltpu.VMEM((tm, d), F32)],
        compiler_params=_cparams("parallel", "arbitrary"),
        name="ffn",
    )(x1, x1, x1, mod3, mod3, mod3, wup, cw, cb, wdn, lg, lb)


def _block_diag_mean(width, seg):
    i = np.arange(width)
    return jnp.asarray(((i[:, None] // seg) == (i[None, :] // seg)).astype(np.float32) / seg, dtype=BF16)


def _hg_tri(n):
    i = np.arange(n)
    same = (i[:, None] // HG_CHUNK) == (i[None, :] // HG_CHUNK)
    lower = same & (i[None, :] <= i[:, None])
    upper = same & (i[None, :] >= i[:, None])
    return jnp.asarray(np.stack([lower, upper]).astype(np.float32), dtype=BF16)


def _hg_select(w):
    r = np.arange(HG_BLOCK * w)
    c = np.arange(w)
    sg, head_r = r // w, (r % w) // HG_DIM
    head_c, s = c // HG_DIM, c % HG_DIM
    m = (head_r[:, None] == head_c[None, :]) & ((s[None, :] % HG_BLOCK) == sg[:, None])
    return jnp.asarray(m.astype(np.float32), dtype=BF16)


def _position_dft(t):
    lo = 64
    f = jnp.arange(t, dtype=jnp.int32)[:, None]
    step = 2.0 * np.pi / t
    a = ((f * (lo * jnp.arange(t // lo, dtype=jnp.int32))[None, :]) % t).astype(F32) * step
    bb = ((f * jnp.arange(lo, dtype=jnp.int32)[None, :]) % t).astype(F32) * step
    ca, sa = jnp.cos(a)[:, :, None], jnp.sin(a)[:, :, None]
    cb, sb = jnp.cos(bb)[:, None, :], jnp.sin(bb)[:, None, :]
    norm = 1.0 / np.sqrt(t)
    cos = ((ca * cb - sa * sb) * norm).astype(BF16).reshape(t, t)
    msin = ((sa * cb + ca * sb) * -norm).astype(BF16).reshape(t, t)
    return cos, msin


def _channel_dft(groups):
    c = np.arange(FT_DIM, dtype=np.int64)
    angc = 2.0 * np.pi * ((c[:, None] * c[None, :]) % FT_DIM).astype(np.float64) / FT_DIM
    eye = np.eye(groups)
    dftc = np.concatenate([np.kron(eye, np.cos(angc)), np.kron(eye, np.sin(angc))], axis=1) / np.sqrt(FT_DIM)
    return jnp.asarray(dftc, dtype=F32).astype(BF16)


def _rope_tables(t):
    rows = t // GRID_W
    row = jnp.repeat(jnp.arange(rows, dtype=jnp.int32), GRID_W)
    col = jnp.tile(jnp.arange(GRID_W, dtype=jnp.int32), rows)
    nf = HEAD_DIM // 4
    inv = ROPE_THETA ** (-jnp.arange(nf, dtype=F32) / nf)
    ar = row.astype(F32)[:, None] * inv
    ac = col.astype(F32)[:, None] * inv
    cos = jnp.concatenate([jnp.cos(ar)] * 2 + [jnp.cos(ac)] * 2, axis=1)
    sin = jnp.concatenate([-jnp.sin(ar), jnp.sin(ar), -jnp.sin(ac), jnp.sin(ac)], axis=1)
    return jnp.tile(cos, (1, 128 // HEAD_DIM)), jnp.tile(sin, (1, 128 // HEAD_DIM))


def kernel(x, c, ctx, c_ctx, w_ada, b_ada, w_in, q_norm, k_norm, hg_lb, hg_norm, ft_w, w_out, ln1_g, ln1_b, w_up,
           conv_w, conv_b, w_down, ln2_g, ln2_b):
    b, t, d = x.shape
    tc = ctx.shape[1]
    depth = w_ada.shape[0]
    d_ff = w_down.shape[1]
    hgw = hg_lb.shape[2]
    ftw = ft_w.shape[1] * ft_w.shape[2]
    kvw = ATTN_KV_HEADS * HEAD_DIM
    aw = w_in.shape[2] - 2 * kvw - 5 * hgw - ftw
    alpha = (2 * depth) ** 0.25

    sizes = [("q", aw), ("k", kvw), ("v", kvw), ("hq", hgw), ("hi", hgw), ("hf", 2 * hgw), ("hg", hgw), ("ft", ftw)]
    offs, lo = {}, 0
    for name, size in sizes:
        offs[name] = (lo, lo + size)
        lo += size

    mod_rows = -(-(b + 1) // 8) * 8
    cond = jnp.concatenate([c, c_ctx[None, :], jnp.zeros((mod_rows - b - 1, d), F32)], axis=0)
    mod = _ada(cond, w_ada, b_ada)
    mod3 = mod.reshape(depth * mod_rows, 1, 6 * d)

    lb_soft = jax.nn.softmax(hg_lb.astype(F32), axis=0)
    lower = (jnp.cumsum(lb_soft, axis=0) - lb_soft[:1]).reshape(depth, 2, 1, hgw)

    cos, sin = _rope_tables(t)
    bdq = _block_diag_mean(2 * 128, HEAD_DIM)
    bdk = _block_diag_mean(kvw, HEAD_DIM)
    bdh = _block_diag_mean(hgw, HG_DIM)
    tri = _hg_tri(min(HG_TILE, t))
    sel = _hg_select(hgw)
    dftc = _channel_dft(ftw // FT_DIM)
    cs_lat = _position_dft(t)
    cs_ctx = _position_dft(tc)

    cf = FFN_COLS
    tm = min(ROW_TILE, t)
    assert d_ff % cf == 0 and t % tm == 0
    w_in_b, w_out_b, w_up_b, w_down_b = (a.astype(BF16) for a in (w_in, w_out, w_up, w_down))

    for l in range(depth):
        need_ctx = l < depth - 1
        lat_row = lambda bi, l=l: l * mod_rows + bi
        ctx_row = lambda bi, l=l: l * mod_rows + b

        qg = jnp.tile(q_norm[l].astype(F32), aw // HEAD_DIM)[None, :]
        kg = jnp.tile(k_norm[l].astype(F32), kvw // HEAD_DIM)[None, :]
        proj_args = (w_in_b, cos, sin, qg, kg, bdq, bdk, dftc)
        q, k, v, hq, hi, hg, hf, hit, fab = _inproj(x, mod3, lat_row, l, *proj_args, offs=offs, rope=True, tm=tm)
        qc, kc, vc, hqc, hic, hgc, hfc, hitc, fabc = _inproj(ctx, mod3, ctx_row, l, *proj_args, offs=offs,
                                                               rope=False, tm=tc)

        attn = _attention(q, jnp.concatenate([kc, k], axis=3), jnp.concatenate([vc, v], axis=2), tq=tm)

        gn = jnp.tile(hg_norm[l].astype(F32), hgw // HG_DIM)[None, :]
        rec, rec_c = _hgrn((hq, hi, hit, hf, hg), (hqc, hic, hitc, hfc, hgc), lower[l], gn, tri, sel, bdh,
                           need_ctx=need_ctx)

        wbd = jax.scipy.linalg.block_diag(*[ft_w[l, g] for g in range(ft_w.shape[1])]).astype(BF16)
        four = _dft(cs_lat, fab, wbd, tf=tm)

        lg1, lb1 = ln1_g[l][None, :], ln1_b[l][None, :]
        lg2, lb2 = ln2_g[l][None, :], ln2_b[l][None, :]
        ffn_args = (l, w_up_b, conv_w[l], conv_b[l][None, :], w_down_b, lg2, lb2)

        x1 = _outproj(attn, rec, four, x, mod3, lat_row, l, w_out_b, lg1, lb1, alpha=alpha, tm=tm)
        x_next = _ffn(x1, mod3, lat_row, *ffn_args, alpha=alpha, tm=tm, cf=cf)

        if need_ctx:
            attn_c = _attention(qc, kc, vc, tq=tc)
            four_c = _dft(cs_ctx, fabc, wbd, tf=tc)
            ctx1 = _outproj(attn_c, rec_c, four_c, ctx, mod3, ctx_row, l, w_out_b, lg1, lb1, alpha=alpha, tm=tc)
            ctx = _ffn(ctx1, mod3, ctx_row, *ffn_args, alpha=alpha, tm=tc, cf=cf)
        x = x_next

    return x
```

```python
import functools

import numpy as np
import jax
import jax.numpy as jnp
from jax import lax
from jax.experimental import pallas as pl
from jax.experimental.pallas import tpu as pltpu

F32 = jnp.float32
BF16 = jnp.bfloat16

GRID_W = 64
HEAD_DIM = 64
ATTN_KV_HEADS = 2
HG_DIM = 64
FT_DIM = 64
ROPE_THETA = 10000.0
F_MIN = 1e-30
EPS = 1e-6
CONV_W = 3

HG_CHUNK = 64
HG_BLOCK = 16
HG_NB = HG_CHUNK // HG_BLOCK
HG_PAIR = 2 * HG_DIM
HG_TILE = 512
NEG = -1e30
LOG2_E = 1.4426950408889634

ROW_TILE = 512
FFN_COLS = 256

V7X_VMEM_BYTES = 64 * 1024 * 1024
VMEM_SHARE = 0.85
VMEM_LIMIT = int(V7X_VMEM_BYTES * VMEM_SHARE)
BF16_ROWS = 16

_NT = (((1,), (1,)), ((), ()))


def _cparams(*sem):
    return pltpu.CompilerParams(dimension_semantics=sem, vmem_limit_bytes=VMEM_LIMIT)


def _dot(a, b):
    return jnp.dot(a, b, preferred_element_type=F32)


def _silu(x):
    return x * jax.nn.sigmoid(x)


def _layer_norm(y, g, b):
    mu = jnp.mean(y, axis=-1, keepdims=True)
    d = y - mu
    var = jnp.mean(d * d, axis=-1, keepdims=True)
    return d * lax.rsqrt(var + EPS) * g + b


def _full(shape):
    n = len(shape)
    return pl.BlockSpec(shape, lambda *_: (0,) * n)


def _layer(w, l, buffers=2):
    n = w.ndim - 1
    return pl.BlockSpec((None,) + w.shape[1:], lambda *_: (l,) + (0,) * n, pipeline_mode=pl.Buffered(buffers))


def _ada_kernel(c_ref, w_ref, b_ref, o_ref):
    s = _silu(c_ref[...]).astype(BF16)
    o_ref[0] = _dot(s, w_ref[0].astype(BF16)) + b_ref[0]


def _ada(cond, w_ada, b_ada):
    depth, d, n = w_ada.shape
    rows = cond.shape[0]
    tn = n // 4
    return pl.pallas_call(
        _ada_kernel,
        out_shape=jax.ShapeDtypeStruct((depth, rows, n), F32),
        grid=(depth, n // tn),
        in_specs=[
            pl.BlockSpec((rows, d), lambda l, j: (0, 0)),
            pl.BlockSpec((1, d, tn), lambda l, j: (l, 0, j)),
            pl.BlockSpec((1, 1, tn), lambda l, j: (l, 0, j)),
        ],
        out_specs=pl.BlockSpec((1, rows, tn), lambda l, j: (l, 0, j)),
        compiler_params=_cparams("arbitrary", "arbitrary"),
        name="ada",
    )(cond, w_ada, b_ada.reshape(depth, 1, n))


def _rope128(y, cos, sin_signed):
    lane = lax.broadcasted_iota(jnp.int32, y.shape, 1)
    upper = (lane % 32) >= 16
    partner = jnp.where(upper, pltpu.roll(y, 16, 1), pltpu.roll(y, 112, 1))
    return y * cos + partner * sin_signed


def _inproj_kernel(x_ref, sh_ref, sc_ref, w_ref, cos_ref, sin_ref, qg_ref, kg_ref, bdq_ref, bdk_ref,
                   dftc_ref, q_ref, k_ref, v_ref, hq_ref, hi_ref, hg_ref, hf_ref, hit_ref, fab_ref, h_scr, qk_scr,
                   *, rope, offs):
    h_scr[...] = (x_ref[0] * (1.0 + sc_ref[0]) + sh_ref[0]).astype(BF16)

    def proj(name):
        lo, hi = offs[name]
        return _dot(h_scr[...], w_ref[:, lo:hi])

    q_lo, q_hi = offs["q"]
    k_lo, k_hi = offs["k"]

    def norm_rope(lo, hi, bd_ref, g_ref):
        base = q_lo if lo < q_hi else k_lo
        y = qk_scr[:, lo - q_lo:hi - q_lo]
        y = y * lax.rsqrt(_dot((y * y).astype(BF16), bd_ref[...]) + EPS) * g_ref[:, lo - base:hi - base]
        if rope:
            cos, sin = cos_ref[...], sin_ref[...]
            y = jnp.concatenate([_rope128(y[:, c:c + 128], cos, sin) for c in range(0, hi - lo, 128)], axis=1)
        return y

    q_scale = HEAD_DIM ** -0.5 * LOG2_E

    def q_part(lo, hi):
        q_ref[0, :, lo - q_lo:hi - q_lo] = (norm_rope(lo, hi, bdq_ref, qg_ref) * q_scale).astype(BF16)

    qk_scr[...] = _dot(h_scr[...], w_ref[:, q_lo:k_hi])
    v = proj("v")
    lane = lax.broadcasted_iota(jnp.int32, v.shape, 1)
    for hh in range(v.shape[1] // HEAD_DIM):
        vh = v if hh == 0 else pltpu.roll(v, v.shape[1] - hh * HEAD_DIM, 1)
        v_ref[0, hh] = jnp.where(lane < HEAD_DIM, vh, jnp.where(lane == HEAD_DIM, 1.0, 0.0)).astype(BF16)
    hq_ref[0] = proj("hq").astype(BF16)
    qw = bdq_ref.shape[0]
    q_part(q_lo, q_lo + qw)
    hi = proj("hi")
    hi_ref[0] = hi.astype(BF16)
    vt = hi.T.astype(BF16)
    for j in range(vt.shape[1] // HG_CHUNK):
        hit_ref[0, j] = vt[:, j * HG_CHUNK:(j + 1) * HG_CHUNK]
    for lo in range(q_lo + qw, q_hi, qw):
        q_part(lo, lo + qw)
    hg_ref[0] = proj("hg").astype(BF16)
    kt = norm_rope(k_lo, k_hi, bdk_ref, kg_ref).T.astype(BF16)
    for hh in range(kt.shape[0] // HEAD_DIM):
        k_ref[0, hh] = kt[hh * HEAD_DIM:(hh + 1) * HEAD_DIM, :]
    ab = _dot(proj("ft").astype(BF16), dftc_ref[...])
    half = ab.shape[1] // 2
    fab_ref[0, 0] = ab[:, :half].astype(BF16)
    fab_ref[0, 1] = ab[:, half:].astype(BF16)
    hf_ref[0] = proj("hf")


def _inproj(x, mod3, mod_row, l, w_in, cos, sin, qg, kg, bdq, bdk, dftc, *, offs, rope, tm):
    b, t, d = x.shape
    aw = offs["q"][1] - offs["q"][0]
    kw = offs["k"][1] - offs["k"][0]
    hw = offs["hq"][1] - offs["hq"][0]
    fw = offs["ft"][1] - offs["ft"][0]
    row = lambda bi: mod_row(bi)
    bt = lambda width: pl.BlockSpec((1, tm, width), lambda bi, i: (bi, i, 0))
    kvh = kw // HEAD_DIM
    out_shapes = (
        jax.ShapeDtypeStruct((b, t, aw), BF16), jax.ShapeDtypeStruct((b, kvh, HEAD_DIM, t), BF16),
        jax.ShapeDtypeStruct((b, kvh, t, 2 * HEAD_DIM), BF16), jax.ShapeDtypeStruct((b, t, hw), BF16),
        jax.ShapeDtypeStruct((b, t, hw), BF16), jax.ShapeDtypeStruct((b, t, hw), BF16),
        jax.ShapeDtypeStruct((b, t, 2 * hw), F32),
        jax.ShapeDtypeStruct((b, t // HG_CHUNK, hw, HG_CHUNK), BF16),
        jax.ShapeDtypeStruct((b, 2, t, fw), BF16),
    )
    out_specs = (
        bt(aw),
        pl.BlockSpec((1, kvh, HEAD_DIM, tm), lambda bi, i: (bi, 0, 0, i)),
        pl.BlockSpec((1, kvh, tm, 2 * HEAD_DIM), lambda bi, i: (bi, 0, i, 0)),
        bt(hw), bt(hw), bt(hw), bt(2 * hw),
        pl.BlockSpec((1, tm // HG_CHUNK, hw, HG_CHUNK), lambda bi, i: (bi, i, 0, 0)),
        pl.BlockSpec((1, 2, tm, fw), lambda bi, i: (bi, 0, i, 0)),
    )
    return pl.pallas_call(
        functools.partial(_inproj_kernel, rope=rope, offs=offs),
        out_shape=out_shapes,
        grid=(b, t // tm),
        in_specs=[
            pl.BlockSpec((1, tm, d), lambda bi, i: (bi, i, 0)),
            pl.BlockSpec((1, 1, d), lambda bi, i: (row(bi), 0, 0)),
            pl.BlockSpec((1, 1, d), lambda bi, i: (row(bi), 0, 1)),
            _layer(w_in, l),
            pl.BlockSpec((tm, 128), lambda bi, i: (i, 0)),
            pl.BlockSpec((tm, 128), lambda bi, i: (i, 0)),
            _full(qg.shape), _full(kg.shape), _full(bdq.shape), _full(bdk.shape), _full(dftc.shape),
        ],
        out_specs=out_specs,
        scratch_shapes=[pltpu.VMEM((tm, d), BF16), pltpu.VMEM((tm, aw + kw), F32)],
        compiler_params=_cparams("parallel", "parallel"),
        name="inproj",
    )(x, mod3, mod3, w_in, cos, sin, qg, kg, bdq, bdk, dftc)


def _attn_kernel(q_ref, kt_ref, v_ref, o_ref, s_scr, m_scr, p_scr, *, group):
    kt = kt_ref[0, 0]
    v = v_ref[0, 0]
    half = (v.shape[0] // 2) // 256 * 256
    tq = q_ref.shape[1]
    nrg = 2 if tq % 512 == 0 else 1
    rg = tq // nrg
    units = [(h, r) for h in range(group) for r in range(nrg)]

    def scores(h, r):
        rows = slice(r * rg, (r + 1) * rg)
        s = _dot(q_ref[0, rows, h * HEAD_DIM:(h + 1) * HEAD_DIM], kt)
        s_scr[h, rows, :] = s
        m_scr[h, rows, :] = jnp.max(s, axis=-1, keepdims=True)

    def probs(h, r):
        rows = slice(r * rg, (r + 1) * rg)
        p_scr[h, rows, :] = jnp.exp2(s_scr[h, rows, :] - m_scr[h, rows, :]).astype(BF16)

    def values(h, r):
        rows = slice(r * rg, (r + 1) * rg)
        if half:
            pv = _dot(p_scr[h, rows, :half], v[:half]) + _dot(p_scr[h, rows, half:], v[half:])
        else:
            pv = _dot(p_scr[h, rows, :], v)
        return pv[:, :HEAD_DIM] / pv[:, HEAD_DIM:HEAD_DIM + 1]

    outs = {}
    for step in range(len(units) + 2):
        if step < len(units):
            scores(*units[step])
        if 0 <= step - 1 < len(units):
            probs(*units[step - 1])
        if 0 <= step - 2 < len(units):
            outs[units[step - 2]] = values(*units[step - 2])
    o_ref[0] = jnp.concatenate(
        [jnp.concatenate([outs[(h, r)] for r in range(nrg)], axis=0) for h in range(group)], axis=1).astype(BF16)


def _attention(q, kt_all, v_all, *, tq):
    b, t, aw = q.shape
    kvh, s = v_all.shape[1], v_all.shape[2]
    group = aw // HEAD_DIM // kvh
    gw = group * HEAD_DIM
    return pl.pallas_call(
        functools.partial(_attn_kernel, group=group),
        out_shape=jax.ShapeDtypeStruct((b, t, aw), BF16),
        grid=(b, kvh, t // tq),
        in_specs=[
            pl.BlockSpec((1, tq, gw), lambda bi, g, i: (bi, i, g)),
            pl.BlockSpec((1, 1, HEAD_DIM, s), lambda bi, g, i: (bi, g, 0, 0)),
            pl.BlockSpec((1, 1, s, 2 * HEAD_DIM), lambda bi, g, i: (bi, g, 0, 0)),
        ],
        out_specs=pl.BlockSpec((1, tq, gw), lambda bi, g, i: (bi, i, g)),
        scratch_shapes=[pltpu.VMEM((group, tq, s), F32), pltpu.VMEM((group, tq, 1), F32),
                        pltpu.VMEM((group, tq, s), BF16)],
        compiler_params=_cparams("parallel", "parallel", "parallel"),
        name="attention",
    )(q, kt_all, v_all)


def _hg_prepare(rev, hq, z, v, hit_ref, c0, lb, tri_ref, sel_ref, o_ref, row0, qb_ref, qs_ref, ks_ref, si_ref, dl_ref,
                d_ref):
    r, w = z.shape
    n = r // HG_CHUNK
    f = lb + (1.0 - lb) * jax.nn.sigmoid(z)
    logf = jnp.log2(jnp.maximum(f, F_MIN))
    kk = 1.0 - f
    qs = _silu(hq.astype(F32))
    hi = logf.astype(BF16)
    lo = (logf - hi.astype(F32)).astype(BF16)
    tri = tri_ref[0:r, 0:r]
    cum = _dot(tri, hi) + _dot(tri, lo)
    yield "local"

    c3 = cum.reshape(n, HG_CHUNK, w)
    q3 = qs.reshape(n, HG_CHUNK, w)
    k3 = kk.reshape(n, HG_CHUNK, w)

    def rows(a, lo_, hi_):
        parts = []
        if lo_ > 0:
            parts.append(jnp.zeros((n, lo_, w), F32))
        parts.append(a)
        if hi_ < HG_CHUNK:
            parts.append(jnp.zeros((n, HG_CHUNK - hi_, w), F32))
        return jnp.concatenate(parts, axis=1) if len(parts) > 1 else a

    e_last = c3[:, 0:1, :] if rev else c3[:, HG_CHUNK - 1:HG_CHUNK, :]
    qparts, kparts = [], []
    for g in range(1, HG_NB):
        j = HG_NB - g if rev else g - 1
        b0, b1 = j * HG_BLOCK, (j + 1) * HG_BLOCK
        e = c3[:, b0:b0 + 1, :] if rev else c3[:, b1 - 1:b1, :]
        p0, p1 = (0, b0) if rev else (b1, HG_CHUNK)
        qparts.append(rows(q3[:, p0:p1, :] * jnp.exp2(c3[:, p0:p1, :] - e), p0, p1))
        kparts.append(rows(k3[:, b0:b1, :] * jnp.exp2(e - c3[:, b0:b1, :]), b0, b1))
    qparts.append(q3 * jnp.exp2(c3))
    kparts.append(k3 * jnp.exp2(e_last - c3))
    dl = jnp.exp2(e_last).reshape(n, w)
    pairs = w // HG_PAIR
    ng = HG_NB - 1
    for pp in range(pairs):
        lanes = slice(pp * HG_PAIR, (pp + 1) * HG_PAIR)
        qb_ref[pp, 0:r, :] = jnp.concatenate([p[:, :, lanes] for p in qparts[:ng]], axis=-1).reshape(
            r, ng * HG_PAIR).astype(BF16)
        qs_ref[pp, 0:r, :] = qparts[ng][:, :, lanes].reshape(r, HG_PAIR).astype(BF16)
        ks_ref[pp, 0:r, :] = jnp.concatenate([p[:, :, lanes] for p in kparts], axis=-1).reshape(
            r, HG_NB * HG_PAIR).astype(BF16)
        dl_ref[pp, 0:n, :] = dl[:, lanes]
        yield "local"

    nb = r // HG_BLOCK
    hb = HG_BLOCK // 2
    c5 = cum.reshape(nb, 2, hb, w)
    q5 = qs.reshape(nb, 2, hb, w)
    d5 = (cum - jnp.log2(jnp.maximum(kk, 0.0))).reshape(nb, 2, hb, w)
    tau = lax.broadcasted_iota(jnp.int32, (nb, hb, w), 1)
    halves = []
    for half in range(2):
        sources = [sg for sg in range(HG_BLOCK) if not ((half > sg // hb) if rev else (half < sg // hb))]
        eparts = []
        for sg in sources:
            sh, sl = sg // hb, sg % hb
            arg = c5[:, half] - d5[:, sh, sl:sl + 1, :]
            if half == sh:
                arg = jnp.where((tau <= sl) if rev else (tau >= sl), arg, NEG)
            eparts.append((jnp.exp2(arg) * q5[:, half]).reshape(nb * hb, w).astype(BF16))
            if len(eparts) % 2 == 0:
                yield "local"
        lo_ = sources[0] * w
        halves.append(_dot(jnp.concatenate(eparts, axis=-1), sel_ref[lo_:lo_ + len(sources) * w, :]))
        yield "local"
    a = jnp.stack([h_.reshape(nb, hb, w) for h_ in halves], axis=1).reshape(r, w)
    ti = lax.broadcasted_iota(jnp.int32, (r, w), 0)
    ci = lax.broadcasted_iota(jnp.int32, (r, w), 1)
    same_block = ((ti % HG_CHUNK) // HG_BLOCK) == ((ci % HG_CHUNK) // HG_BLOCK)
    ad = jnp.where(same_block, a, 0.0).astype(BF16)

    def same_head(shape):
        return (lax.broadcasted_iota(jnp.int32, shape, 0) // HG_DIM) == (lax.broadcasted_iota(jnp.int32, shape, 1) // HG_DIM)

    pair_mask = same_head((HG_PAIR, HG_PAIR))
    head_mask = same_head((w, w))
    for c in range(n):
        rws = slice(c * HG_CHUNK, (c + 1) * HG_CHUNK)
        vt = hit_ref[0, c0 + c]
        for pp in range(pairs):
            x = _dot(vt[pp * HG_PAIR:(pp + 1) * HG_PAIR, :], ks_ref[pp, rws, :])
            d_ref[c, pp] = jnp.concatenate(
                [jnp.where(pair_mask, x[:, g * HG_PAIR:(g + 1) * HG_PAIR], 0.0).astype(BF16) for g in range(ng)],
                axis=1)
            si_ref[c, pp] = jnp.where(pair_mask, x[:, ng * HG_PAIR:], 0.0)
            yield "products"
    for c in range(n):
        rws = slice(c * HG_CHUNK, (c + 1) * HG_CHUNK)
        outs = [lax.dot_general(qb_ref[pp, rws, :], d_ref[c, pp], _NT, preferred_element_type=F32)
                for pp in range(pairs)]
        vc = v[rws]
        vbd = jnp.where(head_mask, jnp.concatenate([vc] * (w // HG_CHUNK), axis=0), jnp.zeros((), vc.dtype))
        o_ref[row0 + c * HG_CHUNK:row0 + (c + 1) * HG_CHUNK, :] = jnp.concatenate(outs, axis=1) + _dot(ad[rws], vbd)
        yield "products"


def _run_interleaved(first, second):
    live = [first, second]
    while live:
        for g in list(live):
            if next(g, None) is None:
                live.remove(g)


def _hg_step(qs_ref, si_ref, dl_ref, st_ref, o_ref, r0, g0, ci):
    outs = []
    for pp in range(st_ref.shape[0]):
        st = st_ref[pp]
        outs.append(lax.dot_general(qs_ref[pp, pl.ds(r0, HG_CHUNK), :], st.astype(BF16), _NT,
                                    preferred_element_type=F32))
        st_ref[pp] = st * dl_ref[pp, pl.ds(ci, 1), :] + si_ref[ci, pp]
    o_ref[pl.ds(g0, HG_CHUNK), :] += jnp.concatenate(outs, axis=1)


def _hgrn_kernel(hq_ref, hi_ref, hit_ref, hf_ref, hg_ref, hqc_ref, hic_ref, hitc_ref, hfc_ref, hgc_ref,
                 lb_ref, gn_ref, tri_ref, sel_ref, bd_ref, *rest, need_ctx):
    if need_ctx:
        rec_ref, recc_ref = rest[0], rest[1]
        scr = rest[2:]
    else:
        rec_ref, recc_ref = rest[0], None
        scr = rest[1:]
    of_ref, ob_ref, ofc_ref, obc_ref, stf_ref, stb_ref = scr[:6]
    fwd, bwd = scr[6:12], scr[12:18]
    w = hq_ref.shape[2]
    lb_f = lb_ref[0]
    lb_b = lb_ref[1]
    stf_ref[...] = jnp.zeros_like(stf_ref)
    stb_ref[...] = jnp.zeros_like(stb_ref)

    def run(hq, hi, hit, hf, o_f, o_b):
        t = hq.shape[1]
        tile = min(HG_TILE, t)
        nt = t // tile
        cpt = tile // HG_CHUNK
        for s in range(nt):
            rf = s * tile
            rb = (nt - 1 - s) * tile
            _run_interleaved(
                _hg_prepare(False, hq[0, rf:rf + tile, :], hf[0, rf:rf + tile, 0:w], hi[0, rf:rf + tile, :], hit,
                            rf // HG_CHUNK, lb_f, tri_ref.at[0], sel_ref, o_f, rf, *fwd),
                _hg_prepare(True, hq[0, rb:rb + tile, :], hf[0, rb:rb + tile, w:2 * w], hi[0, rb:rb + tile, :], hit,
                            rb // HG_CHUNK, lb_b, tri_ref.at[1], sel_ref, o_b, rb, *bwd))

            for i in range(cpt):
                cf, cb = i, cpt - 1 - i
                _hg_step(fwd[1], fwd[3], fwd[4], stf_ref, o_f, cf * HG_CHUNK, rf + cf * HG_CHUNK, cf)
                _hg_step(bwd[1], bwd[3], bwd[4], stb_ref, o_b, cb * HG_CHUNK, rb + cb * HG_CHUNK, cb)

    def finish(o_f, o_b, hg, out):
        t = hg.shape[1]
        tile = min(HG_TILE, t)
        for s in range(t // tile):
            rows = slice(s * tile, (s + 1) * tile)
            o = o_f[rows, :] + o_b[rows, :]
            ms = _dot((o * o).astype(BF16), bd_ref[...])
            on = o * lax.rsqrt(ms + EPS) * gn_ref[...]
            out[0, rows, :] = (on * _silu(hg[0, rows, :].astype(F32))).astype(BF16)

    run(hqc_ref, hic_ref, hitc_ref, hfc_ref, ofc_ref, obc_ref)
    run(hq_ref, hi_ref, hit_ref, hf_ref, of_ref, ob_ref)
    finish(of_ref, ob_ref, hg_ref, rec_ref)
    if need_ctx:
        finish(ofc_ref, obc_ref, hgc_ref, recc_ref)


def _hgrn(lat, cx, lb, gn, tri, sel, bd, *, need_ctx):
    hq, hi, hit, hf, hg = lat
    hqc, hic, hitc, hfc, hgc = cx
    b, t, w = hq.shape
    tc = hqc.shape[1]
    tile = min(HG_TILE, t)

    def seq(a):
        nd = a.ndim
        return pl.BlockSpec((1,) + a.shape[1:], lambda bi: (bi,) + (0,) * (nd - 1))

    out_shape = [jax.ShapeDtypeStruct((b, t, w), BF16)]
    out_specs = [pl.BlockSpec((1, t, w), lambda bi: (bi, 0, 0))]
    if need_ctx:
        out_shape.append(jax.ShapeDtypeStruct((b, tc, w), BF16))
        out_specs.append(pl.BlockSpec((1, tc, w), lambda bi: (bi, 0, 0)))
    pairs = w // HG_PAIR
    cpt = tile // HG_CHUNK
    dir_scratch = [
        pltpu.VMEM((pairs, tile, (HG_NB - 1) * HG_PAIR), BF16),
        pltpu.VMEM((pairs, tile, HG_PAIR), BF16),
        pltpu.VMEM((pairs, tile, HG_NB * HG_PAIR), BF16),
        pltpu.VMEM((cpt, pairs, HG_PAIR, HG_PAIR), F32),
        pltpu.VMEM((pairs, cpt, HG_PAIR), F32),
        pltpu.VMEM((cpt, pairs, HG_PAIR, (HG_NB - 1) * HG_PAIR), BF16),
    ]
    res = pl.pallas_call(
        functools.partial(_hgrn_kernel, need_ctx=need_ctx),
        out_shape=tuple(out_shape),
        grid=(b,),
        in_specs=[seq(hq), seq(hi), seq(hit), seq(hf), seq(hg), seq(hqc), seq(hic), seq(hitc), seq(hfc), seq(hgc),
                  _full(lb.shape), _full(gn.shape), _full(tri.shape), _full(sel.shape), _full(bd.shape)],
        out_specs=tuple(out_specs),
        scratch_shapes=[pltpu.VMEM((t, w), F32), pltpu.VMEM((t, w), F32),
                        pltpu.VMEM((tc, w), F32), pltpu.VMEM((tc, w), F32),
                        pltpu.VMEM((pairs, HG_PAIR, HG_PAIR), F32), pltpu.VMEM((pairs, HG_PAIR, HG_PAIR), F32)]
        + dir_scratch + dir_scratch,
        compiler_params=_cparams("parallel"),
        name="hgrn",
    )(hq, hi, hit, hf, hg, hqc, hic, hitc, hfc, hgc, lb, gn, tri, sel, bd)
    return res if need_ctx else (res[0], None)


def _mix_kernel(cos_ref, msin_ref, ab_ref, wf_ref, a_ref, r_ref, x_ref, g1_ref, w_ref, lg_ref, lb_ref, o_ref,
                z_scr, f_scr, mix_scr, *, alpha):
    aw = a_ref.shape[2]
    rw = r_ref.shape[2]
    tf = cos_ref.shape[0]
    nsub = 2 if tf % (2 * BF16_ROWS) == 0 else 1
    sub = tf // nsub
    group = lambda g: slice(g * sub, (g + 1) * sub)

    def dft(g):
        z_scr[group(g), :] = (_dot(cos_ref[group(g), :], ab_ref[0, 0])
                              + _dot(msin_ref[group(g), :], ab_ref[0, 1])).astype(BF16)

    def channel_map(g):
        f_scr[group(g), :] = _dot(z_scr[group(g), :], wf_ref[...]).astype(BF16)

    def mix(g):
        rows = group(g)
        mix_scr[rows, :] = (_dot(a_ref[0, rows, :], w_ref[0:aw, :]) + _dot(r_ref[0, rows, :], w_ref[aw:aw + rw, :])
                            + _dot(f_scr[rows, :], w_ref[aw + rw:, :]))

    def norm(g):
        rows = group(g)
        o_ref[0, rows, :] = _layer_norm(alpha * x_ref[0, rows, :] + g1_ref[0] * mix_scr[rows, :],
                                        lg_ref[...], lb_ref[...])

    stages = (dft, channel_map, mix, norm)
    for step in range(nsub + len(stages) - 1):
        for k, stage in enumerate(stages):
            if 0 <= step - k < nsub:
                stage(step - k)


def _mix(cs, fab, wbd, attn, rec, x, mod3, mod_row, l, w_out, lg, lb, *, alpha, tf):
    b, t, d = x.shape
    w = fab.shape[3]
    rows = lambda a: pl.BlockSpec((1, tf, a.shape[2]), lambda i, bi: (bi, i, 0))
    return pl.pallas_call(
        functools.partial(_mix_kernel, alpha=alpha),
        out_shape=jax.ShapeDtypeStruct((b, t, d), F32),
        grid=(t // tf, b),
        in_specs=[
            pl.BlockSpec((tf, t), lambda i, bi: (i, 0)),
            pl.BlockSpec((tf, t), lambda i, bi: (i, 0)),
            pl.BlockSpec((1, 2, t, w), lambda i, bi: (bi, 0, 0, 0)),
            _full(wbd.shape),
            rows(attn), rows(rec), rows(x),
            pl.BlockSpec((1, 1, d), lambda i, bi: (mod_row(bi), 0, 2)),
            _layer(w_out, l), _full(lg.shape), _full(lb.shape),
        ],
        out_specs=pl.BlockSpec((1, tf, d), lambda i, bi: (bi, i, 0)),
        scratch_shapes=[pltpu.VMEM((tf, w), BF16), pltpu.VMEM((tf, w), BF16), pltpu.VMEM((tf, d), F32)],
        compiler_params=_cparams("parallel", "parallel"),
        name="mix",
    )(cs[0], cs[1], fab, wbd, attn, rec, x, mod3, w_out, lg, lb)


def _ffn_kernel(x_ref, xp_ref, xn_ref, sh_ref, sc_ref, g2_ref, wup_ref, cw_ref, cb_ref, wdn_ref, lg_ref, lb_ref,
                o_ref, h_scr, u_scr, act_scr, y_scr, *, alpha):
    i = pl.program_id(1)
    last = pl.num_programs(1) - 1
    tm = x_ref.shape[1]
    halo = BF16_ROWS
    sc = sc_ref[0]
    sh = sh_ref[0]

    def mod(v):
        return (v * (1.0 + sc) + sh).astype(BF16)

    h_scr[0:halo, :] = jnp.where(i > 0, mod(xp_ref[0]), jnp.zeros((), BF16))
    h_scr[halo:halo + tm, :] = mod(x_ref[0])
    h_scr[halo + tm:, :] = jnp.where(i < last, mod(xn_ref[0]), jnp.zeros((), BF16))
    d_ff = wdn_ref.shape[0]
    cf = u_scr.shape[3]

    def conv(ub, cols):
        cw = cw_ref[:, cols]
        u = ub[...]
        rows = u.shape[0]
        prev = pltpu.roll(u, 1, 0)[halo:halo + tm]
        nxt = pltpu.roll(u, rows - 1, 0)[halo:halo + tm]
        return prev * cw[0:1, :] + u[halo:halo + tm] * cw[1:2, :] + nxt * cw[2:3, :] + cb_ref[:, cols]

    for j in range(d_ff // cf):
        val_cols = slice(j * cf, (j + 1) * cf)
        gate_cols = slice(d_ff + j * cf, d_ff + (j + 1) * cf)
        uv, ug = u_scr.at[j % 2, 0], u_scr.at[j % 2, 1]
        uv[...] = _dot(h_scr[...], wup_ref[:, val_cols])
        ug[...] = _dot(h_scr[...], wup_ref[:, gate_cols])
        act_scr[:, val_cols] = (conv(uv, val_cols) * _silu(conv(ug, gate_cols))).astype(BF16)
    nsub = 2 if tm % (2 * BF16_ROWS) == 0 else 1
    sub = tm // nsub
    for step in range(nsub + 1):
        if step < nsub:
            rows = slice(step * sub, (step + 1) * sub)
            y_scr[rows, :] = _dot(act_scr[rows, :], wdn_ref[...])
        if step >= 1:
            rows = slice((step - 1) * sub, step * sub)
            o_ref[0, rows, :] = _layer_norm(alpha * x_ref[0, rows, :] + g2_ref[0] * y_scr[rows, :],
                                            lg_ref[...], lb_ref[...])


def _ffn(x1, mod3, mod_row, l, wup, cw, cb, wdn, lg, lb, *, alpha, tm, cf):
    b, t, d = x1.shape
    halo = BF16_ROWS
    per = tm // halo
    nblk = t // halo
    return pl.pallas_call(
        functools.partial(_ffn_kernel, alpha=alpha),
        out_shape=jax.ShapeDtypeStruct((b, t, d), F32),
        grid=(b, t // tm),
        in_specs=[
            pl.BlockSpec((1, tm, d), lambda bi, i: (bi, i, 0)),
            pl.BlockSpec((1, halo, d), lambda bi, i: (bi, jnp.maximum(i * per - 1, 0), 0)),
            pl.BlockSpec((1, halo, d), lambda bi, i: (bi, jnp.minimum((i + 1) * per, nblk - 1), 0)),
            pl.BlockSpec((1, 1, d), lambda bi, i: (mod_row(bi), 0, 3)),
            pl.BlockSpec((1, 1, d), lambda bi, i: (mod_row(bi), 0, 4)),
            pl.BlockSpec((1, 1, d), lambda bi, i: (mod_row(bi), 0, 5)),
            _layer(wup, l, 1), _full(cw.shape), _full(cb.shape), _layer(wdn, l, 1), _full(lg.shape),
            _full(lb.shape),
        ],
        out_specs=pl.BlockSpec((1, tm, d), lambda bi, i: (bi, i, 0)),
        scratch_shapes=[pltpu.VMEM((tm + 2 * halo, d), BF16), pltpu.VMEM((2, 2, tm + 2 * halo, cf), F32),
                        pltpu.VMEM((tm, wdn.shape[1]), BF16), pltpu.VMEM((tm, d), F32)],
        compiler_params=_cparams("parallel", "arbitrary"),
        name="ffn",
    )(x1, x1, x1, mod3, mod3, mod3, wup, cw, cb, wdn, lg, lb)


def _block_diag_mean(width, seg):
    i = np.arange(width)
    return jnp.asarray(((i[:, None] // seg) == (i[None, :] // seg)).astype(np.float32) / seg, dtype=BF16)


def _hg_tri(n):
    i = np.arange(n)
    same = (i[:, None] // HG_CHUNK) == (i[None, :] // HG_CHUNK)
    lower = same & (i[None, :] <= i[:, None])
    upper = same & (i[None, :] >= i[:, None])
    return jnp.asarray(np.stack([lower, upper]).astype(np.float32), dtype=BF16)


def _hg_select(w):
    r = np.arange(HG_BLOCK * w)
    c = np.arange(w)
    sg, head_r = r // w, (r % w) // HG_DIM
    head_c, s = c // HG_DIM, c % HG_DIM
    m = (head_r[:, None] == head_c[None, :]) & ((s[None, :] % HG_BLOCK) == sg[:, None])
    return jnp.asarray(m.astype(np.float32), dtype=BF16)


def _position_dft(t):
    lo = 64
    f = jnp.arange(t, dtype=jnp.int32)[:, None]
    step = 2.0 * np.pi / t
    a = ((f * (lo * jnp.arange(t // lo, dtype=jnp.int32))[None, :]) % t).astype(F32) * step
    bb = ((f * jnp.arange(lo, dtype=jnp.int32)[None, :]) % t).astype(F32) * step
    ca, sa = jnp.cos(a)[:, :, None], jnp.sin(a)[:, :, None]
    cb, sb = jnp.cos(bb)[:, None, :], jnp.sin(bb)[:, None, :]
    norm = 1.0 / np.sqrt(t)
    cos = ((ca * cb - sa * sb) * norm).astype(BF16).reshape(t, t)
    msin = ((sa * cb + ca * sb) * -norm).astype(BF16).reshape(t, t)
    return cos, msin


def _channel_dft(groups):
    c = np.arange(FT_DIM, dtype=np.int64)
    angc = 2.0 * np.pi * ((c[:, None] * c[None, :]) % FT_DIM).astype(np.float64) / FT_DIM
    eye = np.eye(groups)
    dftc = np.concatenate([np.kron(eye, np.cos(angc)), np.kron(eye, np.sin(angc))], axis=1) / np.sqrt(FT_DIM)
    return jnp.asarray(dftc, dtype=F32).astype(BF16)


def _rope_tables(t):
    rows = t // GRID_W
    row = jnp.repeat(jnp.arange(rows, dtype=jnp.int32), GRID_W)
    col = jnp.tile(jnp.arange(GRID_W, dtype=jnp.int32), rows)
    nf = HEAD_DIM // 4
    inv = ROPE_THETA ** (-jnp.arange(nf, dtype=F32) / nf)
    ar = row.astype(F32)[:, None] * inv
    ac = col.astype(F32)[:, None] * inv
    cos = jnp.concatenate([jnp.cos(ar)] * 2 + [jnp.cos(ac)] * 2, axis=1)
    sin = jnp.concatenate([-jnp.sin(ar), jnp.sin(ar), -jnp.sin(ac), jnp.sin(ac)], axis=1)
    return jnp.tile(cos, (1, 128 // HEAD_DIM)), jnp.tile(sin, (1, 128 // HEAD_DIM))


def kernel(x, c, ctx, c_ctx, w_ada, b_ada, w_in, q_norm, k_norm, hg_lb, hg_norm, ft_w, w_out, ln1_g, ln1_b, w_up,
           conv_w, conv_b, w_down, ln2_g, ln2_b):
    b, t, d = x.shape
    tc = ctx.shape[1]
    depth = w_ada.shape[0]
    d_ff = w_down.shape[1]
    hgw = hg_lb.shape[2]
    ftw = ft_w.shape[1] * ft_w.shape[2]
    kvw = ATTN_KV_HEADS * HEAD_DIM
    aw = w_in.shape[2] - 2 * kvw - 5 * hgw - ftw
    alpha = (2 * depth) ** 0.25

    sizes = [("q", aw), ("k", kvw), ("v", kvw), ("hq", hgw), ("hi", hgw), ("hf", 2 * hgw), ("hg", hgw), ("ft", ftw)]
    offs, lo = {}, 0
    for name, size in sizes:
        offs[name] = (lo, lo + size)
        lo += size

    mod_rows = -(-(b + 1) // 8) * 8
    cond = jnp.concatenate([c, c_ctx[None, :], jnp.zeros((mod_rows - b - 1, d), F32)], axis=0)
    mod = _ada(cond, w_ada, b_ada)
    mod3 = mod.reshape(depth * mod_rows, 1, 6 * d)

    lb_soft = jax.nn.softmax(hg_lb.astype(F32), axis=0)
    lower = (jnp.cumsum(lb_soft, axis=0) - lb_soft[:1]).reshape(depth, 2, 1, hgw)

    cos, sin = _rope_tables(t)
    bdq = _block_diag_mean(2 * 128, HEAD_DIM)
    bdk = _block_diag_mean(kvw, HEAD_DIM)
    bdh = _block_diag_mean(hgw, HG_DIM)
    tri = _hg_tri(min(HG_TILE, t))
    sel = _hg_select(hgw)
    dftc = _channel_dft(ftw // FT_DIM)
    cs_lat = _position_dft(t)
    cs_ctx = _position_dft(tc)

    cf = FFN_COLS
    tm = min(ROW_TILE, t)
    assert d_ff % cf == 0 and t % tm == 0
    w_in_b, w_out_b, w_up_b, w_down_b = (a.astype(BF16) for a in (w_in, w_out, w_up, w_down))

    for l in range(depth):
        need_ctx = l < depth - 1
        lat_row = lambda bi, l=l: l * mod_rows + bi
        ctx_row = lambda bi, l=l: l * mod_rows + b

        qg = jnp.tile(q_norm[l].astype(F32), aw // HEAD_DIM)[None, :]
        kg = jnp.tile(k_norm[l].astype(F32), kvw // HEAD_DIM)[None, :]
        proj_args = (w_in_b, cos, sin, qg, kg, bdq, bdk, dftc)
        q, k, v, hq, hi, hg, hf, hit, fab = _inproj(x, mod3, lat_row, l, *proj_args, offs=offs, rope=True, tm=tm)
        qc, kc, vc, hqc, hic, hgc, hfc, hitc, fabc = _inproj(ctx, mod3, ctx_row, l, *proj_args, offs=offs,
                                                               rope=False, tm=tc)

        attn = _attention(q, jnp.concatenate([kc, k], axis=3), jnp.concatenate([vc, v], axis=2), tq=tm)

        gn = jnp.tile(hg_norm[l].astype(F32), hgw // HG_DIM)[None, :]
        rec, rec_c = _hgrn((hq, hi, hit, hf, hg), (hqc, hic, hitc, hfc, hgc), lower[l], gn, tri, sel, bdh,
                           need_ctx=need_ctx)

        wbd = jax.scipy.linalg.block_diag(*[ft_w[l, g] for g in range(ft_w.shape[1])]).astype(BF16)
        lg1, lb1 = ln1_g[l][None, :], ln1_b[l][None, :]
        lg2, lb2 = ln2_g[l][None, :], ln2_b[l][None, :]
        ffn_args = (l, w_up_b, conv_w[l], conv_b[l][None, :], w_down_b, lg2, lb2)

        x1 = _mix(cs_lat, fab, wbd, attn, rec, x, mod3, lat_row, l, w_out_b, lg1, lb1, alpha=alpha, tf=tm)
        x_next = _ffn(x1, mod3, lat_row, *ffn_args, alpha=alpha, tm=tm, cf=cf)

        if need_ctx:
            attn_c = _attention(qc, kc, vc, tq=tc)
            ctx1 = _mix(cs_ctx, fabc, wbd, attn_c, rec_c, ctx, mod3, ctx_row, l, w_out_b, lg1, lb1, alpha=alpha, tf=tc)
            ctx = _ffn(ctx1, mod3, ctx_row, *ffn_args, alpha=alpha, tm=tc, cf=cf)
        x = x_next

    return x
```
